```python
import jax, jax.numpy as jnp
from jax import lax
import numpy as np

D_MODEL = 1024
BATCH = 2
SEQ = 8192
DEPTH = 4

GRID_W = 64
CTX_LEN = 256
EPS = 1e-6
ROPE_BASE = 10000.0

A_HEADS = 8
A_KV_HEADS = 2
A_HEAD_DIM = 64
A_WINDOW = 128
A_BLOCK = 128
A_Q = A_HEADS * A_HEAD_DIM
A_KV = A_KV_HEADS * A_HEAD_DIM
A_OUT = A_Q

B_HEADS = 8
B_Q_RANK = 256
B_KV_RANK = 128
B_NOPE_DIM = 64
B_ROPE_DIM = 32
B_V_DIM = 64
B_QBLOCK = 128
B_OUT = B_HEADS * B_V_DIM

C_HEADS = 4
C_KEY_DIM = 64
C_VAL_DIM = 128
C_GATE_RANK = 16
C_GATE_TAU = 16.0
C_CHUNK = 64
C_QK = C_HEADS * C_KEY_DIM
C_V = C_HEADS * C_VAL_DIM
C_OUT = C_V

FFN_DIM = 2816
CONV_W = 3

IN_DIM = A_Q + 2 * A_KV + B_Q_RANK + B_KV_RANK + B_ROPE_DIM + 2 * C_QK + 2 * C_V + 2 * C_GATE_RANK + 3 * D_MODEL

kernel_name = 'hybrid_dit_gqa_mla_gla_convffn'


def _rmsnorm(x, w):
    xf = x.astype(jnp.float32)
    y = xf * lax.rsqrt(jnp.mean(xf * xf, axis=-1, keepdims=True) + EPS)
    return (y * w.astype(jnp.float32)).astype(x.dtype)


def _rope_2d(n_tokens, rot_dim):
    rows = n_tokens // GRID_W
    row = jnp.broadcast_to(jnp.arange(rows)[:, None], (rows, GRID_W)).reshape(-1).astype(jnp.float32)
    col = jnp.broadcast_to(jnp.arange(GRID_W)[None, :], (rows, GRID_W)).reshape(-1).astype(jnp.float32)
    n_freq = rot_dim // 4
    inv = ROPE_BASE ** (-jnp.arange(n_freq, dtype=jnp.float32) / n_freq)
    ang = jnp.concatenate([row[:, None] * inv, col[:, None] * inv], axis=-1)
    return jnp.cos(ang), jnp.sin(ang)


def _apply_rope(t, cos, sin):
    half = t.shape[-1] // 2
    tf = t.astype(jnp.float32)
    t1, t2 = tf[..., :half], tf[..., half:]
    return jnp.concatenate([t1 * cos - t2 * sin, t1 * sin + t2 * cos], axis=-1).astype(t.dtype)


def _split_in(proj):
    sizes = [A_Q, A_KV, A_KV, B_Q_RANK, B_KV_RANK, B_ROPE_DIM, C_QK, C_QK, C_V, C_V, 2 * C_GATE_RANK]
    return jnp.split(proj, np.cumsum(sizes).tolist(), axis=-1)


def _sink_softmax(s, sink):
    m = jnp.maximum(jnp.max(s, axis=-1, keepdims=True), sink)
    p = jnp.exp(s - m)
    return p / (jnp.sum(p, axis=-1, keepdims=True) + jnp.exp(sink - m))


def _window_gqa(q, k, v, kc, vc, sink):
    B, S, Hq, Dh = q.shape
    Hkv = k.shape[2]
    G = Hq // Hkv
    nb = S // A_BLOCK
    scale = Dh ** -0.5
    qb = q.reshape(B, nb, A_BLOCK, Hkv, G, Dh)

    def band(t):
        tp = jnp.pad(t, ((0, 0), (A_BLOCK, A_BLOCK), (0, 0), (0, 0))).reshape(B, nb + 2, A_BLOCK, Hkv, Dh)
        return jnp.concatenate([tp[:, :-2], tp[:, 1:-1], tp[:, 2:]], axis=2)

    kb, vb = band(k), band(v)
    s_loc = jnp.einsum('bnqhgd,bnkhd->bnhgqk', qb, kb, preferred_element_type=jnp.float32) * scale
    s_ctx = jnp.einsum('bnqhgd,bkhd->bnhgqk', qb, kc, preferred_element_type=jnp.float32) * scale
    n_loc = 3 * A_BLOCK
    rel = jnp.arange(n_loc)[None, :] - A_BLOCK - jnp.arange(A_BLOCK)[:, None]
    kpos = jnp.arange(nb)[:, None] * A_BLOCK - A_BLOCK + jnp.arange(n_loc)[None, :]
    valid = (jnp.abs(rel) <= A_WINDOW)[None] & ((kpos >= 0) & (kpos < S))[:, None, :]
    s_loc = jnp.where(valid[None, :, None, None], s_loc, -jnp.inf)
    p = _sink_softmax(jnp.concatenate([s_loc, s_ctx], axis=-1),
                      sink.astype(jnp.float32).reshape(1, 1, Hkv, G, 1, 1)).astype(v.dtype)
    o = (jnp.einsum('bnhgqk,bnkhd->bnqhgd', p[..., :n_loc], vb)
         + jnp.einsum('bnhgqk,bkhd->bnqhgd', p[..., n_loc:], vc))
    return o.reshape(B, S, Hq * Dh)


def _ctx_gqa(qc, kc, vc, sink):
    B, L, Hq, Dh = qc.shape
    Hkv = kc.shape[2]
    G = Hq // Hkv
    q = qc.reshape(B, L, Hkv, G, Dh)
    s = jnp.einsum('bqhgd,bkhd->bhgqk', q, kc, preferred_element_type=jnp.float32) * (Dh ** -0.5)
    p = _sink_softmax(s, sink.astype(jnp.float32).reshape(1, Hkv, G, 1, 1)).astype(vc.dtype)
    return jnp.einsum('bhgqk,bkhd->bqhgd', p, vc).reshape(B, L, Hq * Dh)


def _mla_queries(cq, P):
    B, T, _ = cq.shape
    q = (_rmsnorm(cq, P['b_q_norm']) @ P['b_w_uq']).reshape(B, T, B_HEADS, B_NOPE_DIM + B_ROPE_DIM)
    return q[..., :B_NOPE_DIM], q[..., B_NOPE_DIM:]


def _mla_keys_values(ckv, P):
    B, T, _ = ckv.shape
    kv = (_rmsnorm(ckv, P['b_kv_norm']) @ P['b_w_ukv']).reshape(B, T, B_HEADS, B_NOPE_DIM + B_V_DIM)
    return kv[..., :B_NOPE_DIM], kv[..., B_NOPE_DIM:]


def _mla_attend(qn, qr, kn, kr, v):
    s = (jnp.einsum('bqhd,bkhd->bhqk', qn, kn, preferred_element_type=jnp.float32)
         + jnp.einsum('bqhd,bkd->bhqk', qr, kr, preferred_element_type=jnp.float32)) * ((B_NOPE_DIM + B_ROPE_DIM) ** -0.5)
    p = jax.nn.softmax(s, axis=-1).astype(v.dtype)
    return jnp.einsum('bhqk,bkhd->bqhd', p, v)


def _mla_latent(qn, qr, kn, kr, v, kn_c, kr_c, v_c):
    B, S, H, _ = qn.shape
    nb = S // B_QBLOCK
    keys_n = jnp.concatenate([kn_c, kn], axis=1)
    keys_r = jnp.concatenate([kr_c, kr], axis=1)
    vals = jnp.concatenate([v_c, v], axis=1)

    def to_blocks(t):
        return jnp.moveaxis(t.reshape(B, nb, B_QBLOCK, *t.shape[2:]), 1, 0)

    o = lax.map(lambda a: _mla_attend(a[0], a[1], keys_n, keys_r, vals), (to_blocks(qn), to_blocks(qr)))
    return jnp.moveaxis(o, 0, 1).reshape(B, S, H * B_V_DIM)


def _heads(t, H):
    B, T, _ = t.shape
    return t.reshape(B, T, H, -1).transpose(0, 2, 1, 3).astype(jnp.float32)


def _flip(t):
    return jnp.flip(t, axis=2)


def _gla_log_decay(g_low, P, d):
    logit = g_low @ P['c_w_gate'][d] + P['c_b_gate'][d]
    return _heads(jax.nn.log_sigmoid(logit.astype(jnp.float32)) / C_GATE_TAU, C_HEADS)


def _gla_scan(q, k, v, g, s0, with_out):
    B, H, T, _ = q.shape
    nc = T // C_CHUNK
    causal = jnp.tril(jnp.ones((C_CHUNK, C_CHUNK), dtype=bool))

    def chunks(t):
        return t.reshape(B, H, nc, C_CHUNK, t.shape[-1]).transpose(2, 0, 1, 3, 4)

    def step(state, inp):
        qc, kc, vc, gc = inp
        b = jnp.cumsum(gc, axis=-2)
        b_last = b[..., -1:, :]
        new_state = (jnp.exp(b_last)[..., 0, :, None] * state
                     + jnp.einsum('bhcd,bhce->bhde', kc * jnp.exp(b_last - b), vc))
        if not with_out:
            return new_state, None
        o_inter = jnp.einsum('bhcd,bhde->bhce', qc * jnp.exp(b), state)
        diff = b[..., :, None, :] - b[..., None, :, :]
        decay = jnp.exp(jnp.where(causal[:, :, None], diff, -jnp.inf))
        att = jnp.einsum('bhid,bhjd,bhijd->bhij', qc, kc, decay)
        return new_state, o_inter + jnp.einsum('bhij,bhje->bhie', att, vc)

    s_fin, o = lax.scan(step, s0, (chunks(q), chunks(k), chunks(v), chunks(g)))
    if with_out:
        o = o.transpose(1, 2, 0, 3, 4).reshape(B, H, T, v.shape[-1])
    return s_fin, o


def _gla_out(o, r, gain):
    B, H, T, dv = o.shape
    o = o.transpose(0, 2, 1, 3)
    o = o * lax.rsqrt(jnp.mean(o * o, axis=-1, keepdims=True) + EPS) * gain.astype(jnp.float32).reshape(H, dv)
    return (o.reshape(B, T, H * dv) * jax.nn.silu(r.astype(jnp.float32))).astype(r.dtype)


def _merge(ya, yb, yc, gate_logits, P):
    ga, gb, gc = jnp.split(jax.nn.sigmoid(gate_logits), 3, axis=-1)
    m = ga * (ya @ P['w_br_a']) + gb * (yb @ P['w_br_b']) + gc * (yc @ P['w_br_c'])
    return m @ P['w_out']


def _mixer(hl, hc, P, rope_a, rope_b, ctx_out):
    B, S, _ = hl.shape
    Lc = hc.shape[1]
    aq_l, ak_l, av_l, bq_l, bkv_l, bkr_l, cq_l, ck_l, cv_l, cr_l, cg_l, gate_l = _split_in(hl @ P['w_in'])
    aq_c, ak_c, av_c, bq_c, bkv_c, bkr_c, cq_c, ck_c, cv_c, cr_c, cg_c, gate_c = _split_in(hc @ P['w_in'])

    cos_a, sin_a = rope_a
    qa = _apply_rope(aq_l.reshape(B, S, A_HEADS, A_HEAD_DIM), cos_a[:, None], sin_a[:, None])
    ka = _apply_rope(ak_l.reshape(B, S, A_KV_HEADS, A_HEAD_DIM), cos_a[:, None], sin_a[:, None])
    va = av_l.reshape(B, S, A_KV_HEADS, A_HEAD_DIM)
    ka_c = ak_c.reshape(B, Lc, A_KV_HEADS, A_HEAD_DIM)
    va_c = av_c.reshape(B, Lc, A_KV_HEADS, A_HEAD_DIM)
    ya_l = _window_gqa(qa, ka, va, ka_c, va_c, P['a_sink'])

    cos_b, sin_b = rope_b
    qn_l, qr_l = _mla_queries(bq_l, P)
    qr_l = _apply_rope(qr_l, cos_b[:, None], sin_b[:, None])
    kn_l, vb_l = _mla_keys_values(bkv_l, P)
    kr_l = _apply_rope(bkr_l, cos_b, sin_b)
    kn_c, vb_c = _mla_keys_values(bkv_c, P)
    yb_l = _mla_latent(qn_l, qr_l, kn_l, kr_l, vb_l, kn_c, bkr_c, vb_c)

    qs = C_KEY_DIM ** -0.5
    q_l, k_l, v_l = _heads(cq_l, C_HEADS) * qs, _heads(ck_l, C_HEADS), _heads(cv_l, C_HEADS)
    q_c, k_c, v_c = _heads(cq_c, C_HEADS) * qs, _heads(ck_c, C_HEADS), _heads(cv_c, C_HEADS)
    gf_l = _gla_log_decay(cg_l[..., :C_GATE_RANK], P, 0)
    gb_l = _gla_log_decay(cg_l[..., C_GATE_RANK:], P, 1)
    gf_c = _gla_log_decay(cg_c[..., :C_GATE_RANK], P, 0)
    gb_c = _gla_log_decay(cg_c[..., C_GATE_RANK:], P, 1)
    s0 = jnp.zeros((B, C_HEADS, C_KEY_DIM, C_VAL_DIM), jnp.float32)
    sf_c, of_c = _gla_scan(q_c, k_c, v_c, gf_c, s0, ctx_out)
    sb_c, ob_c = _gla_scan(_flip(q_c), _flip(k_c), _flip(v_c), _flip(gb_c), s0, ctx_out)
    _, of_l = _gla_scan(q_l, k_l, v_l, gf_l, sf_c, True)
    _, ob_l = _gla_scan(_flip(q_l), _flip(k_l), _flip(v_l), _flip(gb_l), sb_c, True)
    yc_l = _gla_out(of_l + _flip(ob_l), cr_l, P['c_head_norm'])

    y_l = _merge(ya_l, yb_l, yc_l, gate_l, P)
    if not ctx_out:
        return y_l, None
    ya_c = _ctx_gqa(aq_c.reshape(B, Lc, A_HEADS, A_HEAD_DIM), ka_c, va_c, P['a_sink'])
    qn_c, qr_c = _mla_queries(bq_c, P)
    yb_c = _mla_attend(qn_c, qr_c, kn_c, bkr_c, vb_c).reshape(B, Lc, B_OUT)
    yc_c = _gla_out(of_c + _flip(ob_c), cr_c, P['c_head_norm'])
    y_c = _merge(ya_c, yb_c, yc_c, gate_c, P)
    return y_l, y_c


def _conv_ffn(h, P):
    u = h @ P['w_up']
    up = jnp.pad(u, ((0, 0), (1, 1), (0, 0)))
    cw = P['conv_w']
    u = up[:, :-2] * cw[0] + up[:, 1:-1] * cw[1] + up[:, 2:] * cw[2] + P['conv_b']
    g, val = jnp.split(u, 2, axis=-1)
    return (jax.nn.silu(g) * val) @ P['w_down']


def _layer(xl, xc, mod_l, mod_c, P, rope_a, rope_b, ctx_out):
    sh1, sc1, g1, sh2, sc2, g2 = jnp.split(mod_l, 6, axis=-1)
    csh1, csc1, cg1, csh2, csc2, cg2 = jnp.split(mod_c, 6, axis=-1)
    hl = _rmsnorm(xl, P['norm_mix']) * (1 + sc1) + sh1
    hc = _rmsnorm(xc, P['norm_mix']) * (1 + csc1) + csh1
    y_l, y_c = _mixer(hl, hc, P, rope_a, rope_b, ctx_out)
    xl = xl + g1 * y_l
    xl = xl + g2 * _conv_ffn(_rmsnorm(xl, P['norm_ffn']) * (1 + sc2) + sh2, P)
    if ctx_out:
        xc = xc + cg1 * y_c
        xc = xc + cg2 * _conv_ffn(_rmsnorm(xc, P['norm_ffn']) * (1 + csc2) + csh2, P)
    return xl, xc


def setup_inputs(seed: int = 0) -> dict:
    key = jax.random.key(seed)
    ks = jax.random.split(key, 28)
    L, D = DEPTH, D_MODEL

    def nrm(k, shape, scale):
        return jax.random.normal(k, shape, jnp.float32) * scale

    def gain(k, shape):
        return 1.0 + 0.1 * jax.random.normal(k, shape, jnp.float32)

    return {
        'x': nrm(ks[0], (BATCH, SEQ, D), 1.0),
        'c': nrm(ks[1], (BATCH, D), 1.0),
        'ctx': nrm(ks[2], (BATCH, CTX_LEN, D), 1.0),
        'c_ctx': nrm(ks[3], (D,), 1.0),
        'w_mod': nrm(ks[4], (L, D, 6 * D), 0.5 * D ** -0.5),
        'b_mod': nrm(ks[5], (L, 6 * D), 0.02),
        'norm_mix': gain(ks[6], (L, D)),
        'norm_ffn': gain(ks[7], (L, D)),
        'w_in': nrm(ks[8], (L, D, IN_DIM), D ** -0.5),
        'a_sink': nrm(ks[9], (L, A_HEADS), 0.5),
        'b_q_norm': gain(ks[10], (L, B_Q_RANK)),
        'b_kv_norm': gain(ks[11], (L, B_KV_RANK)),
        'b_w_uq': nrm(ks[12], (L, B_Q_RANK, B_HEADS * (B_NOPE_DIM + B_ROPE_DIM)), B_Q_RANK ** -0.5),
        'b_w_ukv': nrm(ks[13], (L, B_KV_RANK, B_HEADS * (B_NOPE_DIM + B_V_DIM)), B_KV_RANK ** -0.5),
        'c_w_gate': nrm(ks[14], (L, 2, C_GATE_RANK, C_QK), C_GATE_RANK ** -0.5),
        'c_b_gate': nrm(ks[15], (L, 2, C_QK), 0.02),
        'c_head_norm': gain(ks[16], (L, C_V)),
        'w_br_a': nrm(ks[17], (L, A_OUT, D), A_OUT ** -0.5),
        'w_br_b': nrm(ks[18], (L, B_OUT, D), B_OUT ** -0.5),
        'w_br_c': nrm(ks[19], (L, C_OUT, D), C_OUT ** -0.5),
        'w_out': nrm(ks[20], (L, D, D), D ** -0.5),
        'w_up': nrm(ks[21], (L, D, 2 * FFN_DIM), D ** -0.5),
        'conv_w': nrm(ks[22], (L, CONV_W, 2 * FFN_DIM), CONV_W ** -0.5),
        'conv_b': nrm(ks[23], (L, 2 * FFN_DIM), 0.02),
        'w_down': nrm(ks[24], (L, FFN_DIM, D), FFN_DIM ** -0.5),
        'final_norm': gain(ks[25], (D,)),
    }


def reference(x, c, ctx, c_ctx, w_mod, b_mod, norm_mix, norm_ffn, w_in, a_sink, b_q_norm, b_kv_norm,
              b_w_uq, b_w_ukv, c_w_gate, c_b_gate, c_head_norm, w_br_a, w_br_b, w_br_c, w_out,
              w_up, conv_w, conv_b, w_down, final_norm):
    S = x.shape[1]
    rope_a = _rope_2d(S, A_HEAD_DIM)
    rope_b = _rope_2d(S, B_ROPE_DIM)
    silu_c = jax.nn.silu(c)
    silu_cc = jax.nn.silu(c_ctx)
    xl, xc = x, ctx
    for l in range(DEPTH):
        P = {'norm_mix': norm_mix[l], 'norm_ffn': norm_ffn[l], 'w_in': w_in[l], 'a_sink': a_sink[l],
             'b_q_norm': b_q_norm[l], 'b_kv_norm': b_kv_norm[l], 'b_w_uq': b_w_uq[l], 'b_w_ukv': b_w_ukv[l],
             'c_w_gate': c_w_gate[l], 'c_b_gate': c_b_gate[l], 'c_head_norm': c_head_norm[l],
             'w_br_a': w_br_a[l], 'w_br_b': w_br_b[l], 'w_br_c': w_br_c[l], 'w_out': w_out[l],
             'w_up': w_up[l], 'conv_w': conv_w[l], 'conv_b': conv_b[l], 'w_down': w_down[l]}
        mod_l = (silu_c @ w_mod[l] + b_mod[l])[:, None, :]
        mod_c = (silu_cc @ w_mod[l] + b_mod[l])[None, None, :]
        xl, xc = _layer(xl, xc, mod_l, mod_c, P, rope_a, rope_b, l < DEPTH - 1)
    return _rmsnorm(xl, final_norm)
```

```python
import functools

import jax
import jax.numpy as jnp
from jax import lax
from jax.experimental import pallas as pl
from jax.experimental.pallas import tpu as pltpu

GRID_W = 64
EPS = 1e-6
ROPE_BASE = 10000.0
A_HEADS, A_KV_HEADS, A_HEAD_DIM, A_WINDOW = 8, 2, 64, 128
B_HEADS, B_Q_RANK, B_KV_RANK, B_NOPE_DIM, B_ROPE_DIM, B_V_DIM = 8, 256, 128, 64, 32, 64
C_HEADS, C_KEY_DIM, C_VAL_DIM, C_GATE_RANK, C_GATE_TAU, C_CHUNK = 4, 64, 128, 16, 16.0, 64
CONV_W = 3

LANES = 128
SUBLANES = 8
VMEM_BYTES = 64 * 1024 * 1024

TM = 256
GLA_SUB = 16
MOD_COLS = 1536
F32 = jnp.float32
BF16 = jnp.bfloat16
_HI = lax.Precision.HIGHEST
_NEG_INF = float("-inf")


def _dot(a, b):
    return jnp.dot(a, b, preferred_element_type=F32)


def _dot_nt(a, b):
    return lax.dot_general(a, b, (((1,), (1,)), ((), ())), preferred_element_type=F32)


def _rms(x, w):
    return x * lax.rsqrt(jnp.mean(x * x, axis=-1, keepdims=True) + EPS) * w


def _swap_lane_groups(x, half):
    lane = lax.broadcasted_iota(jnp.int32, x.shape, 1)
    up = pltpu.roll(x, LANES - half, 1)
    down = pltpu.roll(x, half, 1)
    return jnp.where((lane & half) == 0, up, down)


def _vmem_limit(nbytes):
    return int(min(VMEM_BYTES - (4 << 20), max(nbytes, 32 << 20)))


def _const_spec(shape):
    nd = len(shape)
    return pl.BlockSpec(shape, lambda *_: (0,) * nd, pipeline_mode=pl.Buffered(1))


def _mod_kernel(c_ref, w_ref, b_ref, o_ref):
    cv = c_ref[...]
    sc = cv * jax.nn.sigmoid(cv)
    o_ref[...] = jnp.dot(sc, w_ref[...], precision=_HI, preferred_element_type=F32) + b_ref[...]


def _modulation(cvec, w_mod, b_mod):
    L, D, N = w_mod.shape
    nb = N // MOD_COLS
    return pl.pallas_call(
        _mod_kernel,
        grid=(L, nb),
        in_specs=[
            pl.BlockSpec((SUBLANES, D), lambda l, j: (0, 0)),
            pl.BlockSpec((None, D, MOD_COLS), lambda l, j: (l, 0, j)),
            pl.BlockSpec((None, 1, MOD_COLS), lambda l, j: (l, 0, j)),
        ],
        out_specs=pl.BlockSpec((None, SUBLANES, MOD_COLS), lambda l, j: (l, 0, j)),
        out_shape=jax.ShapeDtypeStruct((L, SUBLANES, N), F32),
        compiler_params=pltpu.CompilerParams(
            dimension_semantics=("parallel", "parallel"),
            vmem_limit_bytes=_vmem_limit(3 * D * MOD_COLS * 4)),
    )(cvec, w_mod, b_mod.reshape(L, 1, N))


_C_AQ = 0
_C_AK = _C_AQ + A_HEADS * A_HEAD_DIM
_C_AV = _C_AK + A_KV_HEADS * A_HEAD_DIM
_C_BQ = _C_AV + A_KV_HEADS * A_HEAD_DIM
_C_BKV = _C_BQ + B_Q_RANK
_C_BKR = _C_BKV + B_KV_RANK
_C_CQ = _C_BKR + LANES
_C_CK = _C_CQ + C_HEADS * C_KEY_DIM
_C_CV = _C_CK + C_HEADS * C_KEY_DIM
_C_CR = _C_CV + C_HEADS * C_VAL_DIM
_C_CG = _C_CR + C_HEADS * C_VAL_DIM
_C_GATE = _C_CG + LANES
B_HEAD_PAD = LANES


def _inproj_kernel(x_ref, mod_ref, nw_ref, w_ref, rope_ref, bqn_ref, bkvn_ref, wuq_ref, wkn_ref,
                   wvb_ref, wg_ref, bg_ref,
                   qa_ref, ka_ref, va_ref, qb_ref, kb_ref, vb_ref, cq_ref, ck_ref, cv_ref, cr_ref,
                   g_ref, gate_ref, *, d_model):
    mod = mod_ref[...]
    h = _rms(x_ref[...], nw_ref[...]) * (1.0 + mod[1:2]) + mod[0:1]
    hb = h.astype(BF16)

    def proj(lo, hi):
        return _dot(hb, w_ref[:, lo:hi])

    cos_a, sin_a, cos_b, sin_b = rope_ref[0], rope_ref[1], rope_ref[2], rope_ref[3]

    def rope_a(t):
        return t * cos_a + _swap_lane_groups(t, A_HEAD_DIM // 2) * sin_a

    def rope_b(t):
        return t * cos_b + _swap_lane_groups(t, B_ROPE_DIM // 2) * sin_b

    a_scale = A_HEAD_DIM ** -0.5
    for c in range(A_HEADS * A_HEAD_DIM // LANES):
        t = proj(_C_AQ + c * LANES, _C_AQ + (c + 1) * LANES)
        qa_ref[:, c * LANES:(c + 1) * LANES] = (rope_a(t) * a_scale).astype(BF16)
    ka_ref[...] = rope_a(proj(_C_AK, _C_AV)).astype(BF16)
    va_ref[...] = proj(_C_AV, _C_BQ).astype(BF16)

    b_scale = (B_NOPE_DIM + B_ROPE_DIM) ** -0.5
    cqn = _rms(proj(_C_BQ, _C_BKV), bqn_ref[...]).astype(BF16)
    ckvn = _rms(proj(_C_BKV, _C_BKR), bkvn_ref[...]).astype(BF16)
    kr = rope_b(proj(_C_BKR, _C_CQ))
    for hd in range(B_HEADS):
        sl = slice(hd * B_HEAD_PAD, (hd + 1) * B_HEAD_PAD)
        qb_ref[:, sl] = (rope_b(_dot(cqn, wuq_ref[:, sl])) * b_scale).astype(BF16)
        kb_ref[:, sl] = (_dot(ckvn, wkn_ref[:, sl]) + kr).astype(BF16)
    vb_ref[...] = _dot(ckvn, wvb_ref[...]).astype(BF16)

    cq_ref[...] = proj(_C_CQ, _C_CK) * (C_KEY_DIM ** -0.5)
    ck_ref[...] = proj(_C_CK, _C_CV)
    cv_ref[...] = proj(_C_CV, _C_CR).astype(BF16)
    cr_ref[...] = proj(_C_CR, _C_CG)
    z = jnp.dot(proj(_C_CG, _C_GATE), wg_ref[...], precision=_HI, preferred_element_type=F32) + bg_ref[...]
    g_ref[...] = (jnp.minimum(z, 0.0) - jnp.log1p(jnp.exp(-jnp.abs(z)))) * (1.0 / C_GATE_TAU)

    gate_ref[...] = proj(_C_GATE, _C_GATE + 3 * d_model)


def _mod_index(b, t):
    return (jnp.where(t == 0, 0, b + 1), 0, 0)


def _inproj(X, mod, nw, w, rope, bqn, bkvn, wuq, wkn, wvb, wg, bg):
    B, T, D = X.shape
    NT = T // TM
    n_in = w.shape[1]
    row = lambda n: pl.BlockSpec((None, TM, n), lambda b, t: (b, t, 0))
    widths = [(A_HEADS * A_HEAD_DIM, BF16), (A_KV_HEADS * A_HEAD_DIM, BF16), (A_KV_HEADS * A_HEAD_DIM, BF16),
              (B_HEADS * B_HEAD_PAD, BF16), (B_HEADS * B_HEAD_PAD, BF16), (B_HEADS * B_V_DIM, BF16),
              (C_HEADS * C_KEY_DIM, F32), (C_HEADS * C_KEY_DIM, F32), (C_HEADS * C_VAL_DIM, BF16),
              (C_HEADS * C_VAL_DIM, F32), (2 * C_HEADS * C_KEY_DIM, F32), (3 * D, F32)]
    out_bytes = sum(TM * n * jnp.dtype(dt).itemsize for n, dt in widths)
    vmem = w.size * 2 + 4 * out_bytes + 6 * TM * D * 4 + (8 << 20)
    return pl.pallas_call(
        functools.partial(_inproj_kernel, d_model=D),
        grid=(B, NT),
        in_specs=[
            row(D),
            pl.BlockSpec((None, 6, D), _mod_index),
            _const_spec((1, D)),
            _const_spec((D, n_in)),
            pl.BlockSpec((4, TM, LANES), lambda b, t: (0, t, 0)),
            _const_spec(bqn.shape), _const_spec(bkvn.shape), _const_spec(wuq.shape),
            _const_spec(wkn.shape), _const_spec(wvb.shape), _const_spec(wg.shape), _const_spec(bg.shape),
        ],
        out_specs=[row(n) for n, _ in widths],
        out_shape=[jax.ShapeDtypeStruct((B, T, n), dt) for n, dt in widths],
        compiler_params=pltpu.CompilerParams(
            dimension_semantics=("parallel", "parallel"), vmem_limit_bytes=_vmem_limit(vmem)),
    )(X, mod, nw, w, rope, bqn, bkvn, wuq, wkn, wvb, wg, bg)


def _attn_a_kernel(q_ref, kp_ref, km_ref, kn_ref, kc_ref, vp_ref, vm_ref, vn_ref, vc_ref, sink_ref,
                   o_ref, *, seq):
    t = pl.program_id(1)
    kloc = jnp.concatenate([kp_ref[...], km_ref[...], kn_ref[...]], axis=0)
    vloc = jnp.concatenate([vp_ref[...], vm_ref[...], vn_ref[...]], axis=0)
    kctx, vctx = kc_ref[...], vc_ref[...]
    nloc = kloc.shape[0]
    r = lax.broadcasted_iota(jnp.int32, (TM, nloc), 0)
    c = lax.broadcasted_iota(jnp.int32, (TM, nloc), 1)
    rel = c - A_WINDOW - r
    kpos = (t - 1) * TM - A_WINDOW + c
    valid = (jnp.abs(rel) <= A_WINDOW) & (kpos >= 0) & (kpos < seq) & (t >= 1)
    lane_q = lax.broadcasted_iota(jnp.int32, (TM, LANES), 1)
    lane_l = lax.broadcasted_iota(jnp.int32, (nloc, LANES), 1)
    lane_c = lax.broadcasted_iota(jnp.int32, (kctx.shape[0], LANES), 1)
    zero = jnp.zeros((), BF16)
    for col in range(A_HEADS * A_HEAD_DIM // LANES):
        qcol = q_ref[:, col * LANES:(col + 1) * LANES]
        out = jnp.zeros((TM, LANES), F32)
        for hf in range(2):
            lo, hi = hf * A_HEAD_DIM, (hf + 1) * A_HEAD_DIM
            qm = jnp.where((lane_q >= lo) & (lane_q < hi), qcol, zero)
            s_loc = jnp.where(valid, _dot_nt(qm, kloc), _NEG_INF)
            s_ctx = _dot_nt(qm, kctx)
            sink = sink_ref[2 * col + hf:2 * col + hf + 1, 0:1]
            m = jnp.maximum(jnp.maximum(jnp.max(s_loc, axis=1, keepdims=True),
                                        jnp.max(s_ctx, axis=1, keepdims=True)), sink)
            p_loc = jnp.exp(s_loc - m)
            p_ctx = jnp.exp(s_ctx - m)
            denom = (jnp.sum(p_loc, axis=1, keepdims=True) + jnp.sum(p_ctx, axis=1, keepdims=True)
                     + jnp.exp(sink - m))
            v_l = jnp.where((lane_l >= lo) & (lane_l < hi), vloc, zero)
            v_c = jnp.where((lane_c >= lo) & (lane_c < hi), vctx, zero)
            out = out + (_dot(p_loc.astype(BF16), v_l) + _dot(p_ctx.astype(BF16), v_c)) / denom
        o_ref[:, col * LANES:(col + 1) * LANES] = out.astype(BF16)


def _attn_a(qa, ka, va, sink, lc):
    B, T, _ = qa.shape
    NT = T // TM
    seq = T - lc
    kvw = A_KV_HEADS * A_HEAD_DIM
    per_tm = TM // A_WINDOW
    last = T // A_WINDOW - 1
    prev = pl.BlockSpec((None, A_WINDOW, kvw), lambda b, t: (b, jnp.maximum(t * per_tm - 1, 0), 0))
    main = pl.BlockSpec((None, TM, kvw), lambda b, t: (b, t, 0))
    nxt = pl.BlockSpec((None, A_WINDOW, kvw), lambda b, t: (b, jnp.minimum((t + 1) * per_tm, last), 0))
    ctx = pl.BlockSpec((None, lc, kvw), lambda b, t: (b, 0, 0))
    return pl.pallas_call(
        functools.partial(_attn_a_kernel, seq=seq),
        grid=(B, NT),
        in_specs=[pl.BlockSpec((None, TM, qa.shape[2]), lambda b, t: (b, t, 0)),
                  prev, main, nxt, ctx, prev, main, nxt, ctx,
                  pl.BlockSpec(sink.shape, lambda b, t: (0, 0))],
        out_specs=pl.BlockSpec((None, TM, qa.shape[2]), lambda b, t: (b, t, 0)),
        out_shape=jax.ShapeDtypeStruct(qa.shape, BF16),
        compiler_params=pltpu.CompilerParams(dimension_semantics=("parallel", "parallel")),
    )(qa, ka, ka, ka, ka, va, va, va, va, sink)


def _mla_kernel(q_ref, k_ref, v_ref, o_ref, m_ref, l_ref, acc_ref, *, lc, tk):
    t = pl.program_id(1)
    total = k_ref.shape[0]
    nchunks = jnp.where(t == 0, lc // tk, total // tk)
    m_ref[...] = jnp.full(m_ref.shape, _NEG_INF, F32)
    l_ref[...] = jnp.zeros(l_ref.shape, F32)
    acc_ref[...] = jnp.zeros(acc_ref.shape, F32)
    lane_q = lax.broadcasted_iota(jnp.int32, (TM, LANES), 1)
    lane_k = lax.broadcasted_iota(jnp.int32, (tk, LANES), 1)
    zero = jnp.zeros((), BF16)
    npairs = B_HEADS * B_V_DIM // LANES

    def body(j, carry):
        off = pl.multiple_of(j * tk, tk)
        for c in range(npairs):
            vpair = v_ref[pl.ds(off, tk), c * LANES:(c + 1) * LANES]
            pv = None
            alphas = []
            for hf in range(2):
                hd = 2 * c + hf
                s = _dot_nt(q_ref[:, hd * B_HEAD_PAD:(hd + 1) * B_HEAD_PAD],
                            k_ref[pl.ds(off, tk), hd * B_HEAD_PAD:(hd + 1) * B_HEAD_PAD])
                m_prev = m_ref[hd]
                m_new = jnp.maximum(m_prev, jnp.max(s, axis=1, keepdims=True))
                alpha = jnp.exp(m_prev - m_new)
                p = jnp.exp(s - m_new[:, 0:1])
                l_ref[hd] = alpha * l_ref[hd] + jnp.sum(p, axis=1, keepdims=True)
                m_ref[hd] = m_new
                alphas.append(alpha)
                vm = jnp.where((lane_k >= hf * B_V_DIM) & (lane_k < (hf + 1) * B_V_DIM), vpair, zero)
                d = _dot(p.astype(BF16), vm)
                pv = d if pv is None else pv + d
            alpha_pair = jnp.where(lane_q < B_V_DIM, alphas[0], alphas[1])
            sl = slice(c * LANES, (c + 1) * LANES)
            acc_ref[:, sl] = acc_ref[:, sl] * alpha_pair + pv
        return carry

    lax.fori_loop(0, nchunks, body, 0)
    for c in range(npairs):
        l_pair = jnp.where(lane_q < B_V_DIM, l_ref[2 * c], l_ref[2 * c + 1])
        sl = slice(c * LANES, (c + 1) * LANES)
        o_ref[:, sl] = (acc_ref[:, sl] / l_pair).astype(BF16)


def _mla(qb, kb, vb, lc, tk=256):
    B, T, qw = qb.shape
    NT = T // TM
    vw = vb.shape[2]
    vmem = T * (qw + vw) * 2 + 4 * TM * qw * 2 + (2 * B_HEADS * TM * LANES + TM * vw) * 4 + (16 << 20)
    return pl.pallas_call(
        functools.partial(_mla_kernel, lc=lc, tk=tk),
        grid=(B, NT),
        in_specs=[pl.BlockSpec((None, TM, qw), lambda b, t: (b, t, 0)),
                  pl.BlockSpec((None, T, qw), lambda b, t: (b, 0, 0), pipeline_mode=pl.Buffered(1)),
                  pl.BlockSpec((None, T, vw), lambda b, t: (b, 0, 0), pipeline_mode=pl.Buffered(1))],
        out_specs=pl.BlockSpec((None, TM, vw), lambda b, t: (b, t, 0)),
        out_shape=jax.ShapeDtypeStruct((B, T, vw), BF16),
        scratch_shapes=[pltpu.VMEM((B_HEADS, TM, LANES), F32), pltpu.VMEM((B_HEADS, TM, LANES), F32),
                        pltpu.VMEM((TM, vw), F32)],
        compiler_params=pltpu.CompilerParams(
            dimension_semantics=("parallel", "arbitrary"), vmem_limit_bytes=_vmem_limit(vmem)),
    )(qb, kb, vb)


def _gla_chunk(q, k, v2, g, state, reverse):
    C = q.shape[0]
    dk2 = q.shape[1]
    dv2 = v2.shape[1]
    ii = lax.broadcasted_iota(jnp.int32, (C, C), 0)
    jj = lax.broadcasted_iota(jnp.int32, (C, C), 1)
    tri = jnp.where((jj >= ii) if reverse else (jj <= ii), 1.0, 0.0).astype(F32)
    b = jnp.dot(tri, g, precision=_HI, preferred_element_type=F32)
    b_end = b[0:1] if reverse else b[C - 1:C]
    lane = lax.broadcasted_iota(jnp.int32, (GLA_SUB, dk2), 1)
    srow = lax.broadcasted_iota(jnp.int32, (dk2, dv2), 0)
    scol = lax.broadcasted_iota(jnp.int32, (dk2, dv2), 1)
    on_diag = (srow < C_KEY_DIM) == (scol < C_VAL_DIM)

    o = _dot((q * jnp.exp(b)).astype(BF16), state.astype(BF16))
    kd_t = (k * jnp.exp(b_end - b)).T.astype(BF16)
    dec = jnp.exp(jnp.broadcast_to(b_end, (dk2, dk2)).T)
    dec2 = jnp.concatenate([dec] * (dv2 // dk2), axis=1)
    new_state = dec2 * state + jnp.where(on_diag, _dot(kd_t, v2), 0.0)

    rows = []
    for blk in range(C // GLA_SUB):
        r0 = blk * GLA_SUB
        n = r0 + GLA_SUB - 1 if reverse else r0
        k0, k1 = (r0, C) if reverse else (0, r0 + GLA_SUB)
        bn = b[n:n + 1]
        qs = q[r0:r0 + GLA_SUB] * jnp.exp(b[r0:r0 + GLA_SUB] - bn)
        ks = (k[k0:k1] * jnp.exp(bn - b[k0:k1])).astype(BF16)
        qi = lax.broadcasted_iota(jnp.int32, (GLA_SUB, k1 - k0), 0) + r0
        kj = lax.broadcasted_iota(jnp.int32, (GLA_SUB, k1 - k0), 1) + k0
        keep = (kj >= qi) if reverse else (kj <= qi)
        halves = []
        for hf in range(2):
            qm = jnp.where((lane >= hf * C_KEY_DIM) & (lane < (hf + 1) * C_KEY_DIM), qs, 0.0).astype(BF16)
            att = jnp.where(keep, _dot_nt(qm, ks), 0.0).astype(BF16)
            halves.append(_dot(att, v2[k0:k1, hf * C_VAL_DIM:(hf + 1) * C_VAL_DIM]))
        rows.append(jnp.concatenate(halves, axis=1))
    return o + jnp.concatenate(rows, axis=0), new_state


def _gla_kernel(qf_ref, kf_ref, vf_ref, gf_ref, qr_ref, kr_ref, vr_ref, gr_ref, of_ref, or_ref,
                sf_ref, sr_ref):
    t = pl.program_id(1)

    @pl.when(t == 0)
    def _():
        sf_ref[...] = jnp.zeros(sf_ref.shape, F32)
        sr_ref[...] = jnp.zeros(sr_ref.shape, F32)

    nchunk = TM // C_CHUNK
    npair = C_HEADS // 2
    kw, vw = 2 * C_KEY_DIM, 2 * C_VAL_DIM

    def run(c, q_ref, k_ref, v_ref, g_ref, o_ref, s_ref, reverse):
        rows = pl.ds(pl.multiple_of(c * C_CHUNK, C_CHUNK), C_CHUNK)
        for p in range(npair):
            o, s_new = _gla_chunk(q_ref[rows, p * kw:(p + 1) * kw], k_ref[rows, p * kw:(p + 1) * kw],
                                  v_ref[rows, p * vw:(p + 1) * vw], g_ref[rows, p * kw:(p + 1) * kw],
                                  s_ref[p], reverse)
            s_ref[p] = s_new
            o_ref[rows, p * vw:(p + 1) * vw] = o

    def body(c, carry):
        run(c, qf_ref, kf_ref, vf_ref, gf_ref, of_ref, sf_ref, False)
        run(nchunk - 1 - c, qr_ref, kr_ref, vr_ref, gr_ref, or_ref, sr_ref, True)
        return carry

    lax.fori_loop(0, nchunk, body, 0)


def _gla(cq, ck, cv, g):
    B, T, kw = cq.shape
    NT = T // TM
    vw = cv.shape[2]

    def fwd(n, col=0):
        return pl.BlockSpec((None, TM, n), lambda b, t: (b, t, col))

    def rev(n, col=0):
        return pl.BlockSpec((None, TM, n), lambda b, t: (b, jnp.where(t == 0, 0, NT - t), col))

    return pl.pallas_call(
        _gla_kernel,
        grid=(B, NT),
        in_specs=[fwd(kw), fwd(kw), fwd(vw), fwd(kw, 0), rev(kw), rev(kw), rev(vw), rev(kw, 1)],
        out_specs=[fwd(vw), rev(vw)],
        out_shape=[jax.ShapeDtypeStruct((B, T, vw), F32)] * 2,
        scratch_shapes=[pltpu.VMEM((C_HEADS // 2, 2 * C_KEY_DIM, 2 * C_VAL_DIM), F32)] * 2,
        compiler_params=pltpu.CompilerParams(dimension_semantics=("parallel", "arbitrary")),
    )(cq, ck, cv, g, cq, ck, cv, g)


def _merge_kernel(x_ref, mod_ref, ya_ref, yb_ref, of_ref, or_ref, cr_ref, gate_ref, hn_ref,
                  wa_ref, wb_ref, wc_ref, wo_ref, o_ref):
    d = x_ref.shape[1]
    o = of_ref[...] + or_ref[...]
    r = cr_ref[...]
    parts = []
    for hd in range(C_HEADS):
        sl = slice(hd * C_VAL_DIM, (hd + 1) * C_VAL_DIM)
        parts.append(_rms(o[:, sl], hn_ref[:, sl]))
    yc = (jnp.concatenate(parts, axis=1) * (r * jax.nn.sigmoid(r))).astype(BF16)
    gates = jax.nn.sigmoid(gate_ref[...])
    m = (gates[:, 0:d] * _dot(ya_ref[...], wa_ref[...])
         + gates[:, d:2 * d] * _dot(yb_ref[...], wb_ref[...])
         + gates[:, 2 * d:3 * d] * _dot(yc, wc_ref[...]))
    y = _dot(m.astype(BF16), wo_ref[...])
    o_ref[...] = x_ref[...] + mod_ref[2:3, :] * y


def _merge(X, mod, ya, yb, of, orv, cr, gates, hn, wa, wb, wc, wo):
    B, T, D = X.shape
    NT = T // TM
    row = lambda n: pl.BlockSpec((None, TM, n), lambda b, t: (b, t, 0))
    vmem = (wa.size + wb.size + wc.size + wo.size) * 2 + 2 * TM * (2 * D + 3 * D + 4 * ya.shape[2]) * 4 + (16 << 20)
    return pl.pallas_call(
        _merge_kernel,
        grid=(B, NT),
        in_specs=[row(D), pl.BlockSpec((None, 6, D), _mod_index),
                  row(ya.shape[2]), row(yb.shape[2]), row(of.shape[2]), row(orv.shape[2]),
                  row(cr.shape[2]), row(gates.shape[2]), _const_spec(hn.shape),
                  _const_spec(wa.shape), _const_spec(wb.shape), _const_spec(wc.shape), _const_spec(wo.shape)],
        out_specs=row(D),
        out_shape=jax.ShapeDtypeStruct(X.shape, F32),
        compiler_params=pltpu.CompilerParams(
            dimension_semantics=("parallel", "parallel"), vmem_limit_bytes=_vmem_limit(vmem)),
    )(X, mod, ya, yb, of, orv, cr, gates, hn, wa, wb, wc, wo)


def _ffn_kernel(x_ref, xp_ref, xn_ref, mod_ref, nw_ref, wup_ref, cw_ref, cb_ref, wdn_ref, o_ref,
                h_ref, ug_ref, uv_ref, *, nt, ncb):
    t = pl.program_id(1)
    mod = mod_ref[...]
    nw = nw_ref[...]
    halo = xp_ref.shape[0]
    ffn = wdn_ref.shape[0]
    cb = ffn // ncb

    def norm(xv):
        return _rms(xv, nw) * (1.0 + mod[4:5]) + mod[3:4]

    has_prev = (t >= 2)
    has_next = (t >= 1) & (t < nt - 1)
    h_ref[0:halo, :] = jnp.where(has_prev, norm(xp_ref[...]), 0.0).astype(BF16)
    h_ref[halo:halo + TM, :] = norm(x_ref[...]).astype(BF16)
    h_ref[halo + TM:2 * halo + TM, :] = jnp.where(has_next, norm(xn_ref[...]), 0.0).astype(BF16)
    hb = h_ref[...]

    acc = jnp.zeros((TM, x_ref.shape[1]), F32)
    for j in range(ncb):
        ug_ref[...] = _dot(hb, wup_ref[:, j * cb:(j + 1) * cb])
        uv_ref[...] = _dot(hb, wup_ref[:, ffn + j * cb:ffn + (j + 1) * cb])

        def conv(u_ref, lo):
            w = cw_ref[:, lo:lo + cb]
            return (u_ref[halo - 1:halo - 1 + TM, :] * w[0:1] + u_ref[halo:halo + TM, :] * w[1:2]
                    + u_ref[halo + 1:halo + 1 + TM, :] * w[2:3] + cb_ref[:, lo:lo + cb])

        gt = conv(ug_ref, j * cb)
        val = conv(uv_ref, ffn + j * cb)
        act = (gt * jax.nn.sigmoid(gt) * val).astype(BF16)
        acc = acc + _dot(act, wdn_ref[j * cb:(j + 1) * cb, :])
    o_ref[...] = x_ref[...] + mod[5:6] * acc


def _ffn(X, mod, nw, wup, cw, cb, wdn, ncb=2):
    B, T, D = X.shape
    NT = T // TM
    halo = 2 * SUBLANES
    per = TM // halo
    last = T // halo - 1
    ffn = wdn.shape[0]
    cbw = ffn // ncb
    vmem = (wup.size + wdn.size) * 2 + 2 * (TM + 2 * halo) * cbw * 4 + 8 * TM * D * 4 + (16 << 20)
    return pl.pallas_call(
        functools.partial(_ffn_kernel, nt=NT, ncb=ncb),
        grid=(B, NT),
        in_specs=[pl.BlockSpec((None, TM, D), lambda b, t: (b, t, 0)),
                  pl.BlockSpec((None, halo, D), lambda b, t: (b, jnp.maximum(t * per - 1, 0), 0)),
                  pl.BlockSpec((None, halo, D), lambda b, t: (b, jnp.minimum((t + 1) * per, last), 0)),
                  pl.BlockSpec((None, 6, D), _mod_index),
                  _const_spec(nw.shape), _const_spec(wup.shape), _const_spec(cw.shape), _const_spec(cb.shape),
                  _const_spec(wdn.shape)],
        out_specs=pl.BlockSpec((None, TM, D), lambda b, t: (b, t, 0)),
        out_shape=jax.ShapeDtypeStruct(X.shape, F32),
        scratch_shapes=[pltpu.VMEM((TM + 2 * halo, D), BF16), pltpu.VMEM((TM + 2 * halo, cbw), F32),
                        pltpu.VMEM((TM + 2 * halo, cbw), F32)],
        compiler_params=pltpu.CompilerParams(
            dimension_semantics=("parallel", "parallel"), vmem_limit_bytes=_vmem_limit(vmem)),
    )(X, X, X, mod, nw, wup, cw, cb, wdn)


def _final_kernel(x_ref, w_ref, o_ref):
    o_ref[...] = _rms(x_ref[...], w_ref[...])


def _final_norm(X, w, lc):
    B, T, D = X.shape
    S = T - lc
    skip = lc // TM
    return pl.pallas_call(
        _final_kernel,
        grid=(B, S // TM),
        in_specs=[pl.BlockSpec((None, TM, D), lambda b, t: (b, t + skip, 0)), _const_spec(w.shape)],
        out_specs=pl.BlockSpec((None, TM, D), lambda b, t: (b, t, 0)),
        out_shape=jax.ShapeDtypeStruct((B, S, D), F32),
        compiler_params=pltpu.CompilerParams(dimension_semantics=("parallel", "parallel")),
    )(X, w)


def _rope_tables(seq, lc):
    rows = seq // GRID_W
    row = jnp.broadcast_to(jnp.arange(rows)[:, None], (rows, GRID_W)).reshape(-1).astype(F32)
    col = jnp.broadcast_to(jnp.arange(GRID_W)[None, :], (rows, GRID_W)).reshape(-1).astype(F32)

    def cs(rot_dim):
        n_freq = rot_dim // 4
        inv = ROPE_BASE ** (-jnp.arange(n_freq, dtype=F32) / n_freq)
        ang = jnp.concatenate([row[:, None] * inv, col[:, None] * inv], axis=-1)
        return jnp.cos(ang), jnp.sin(ang)

    ca, sa = cs(A_HEAD_DIM)
    cb, sb = cs(B_ROPE_DIM)
    reps = LANES // A_HEAD_DIM
    cos_a = jnp.tile(jnp.concatenate([ca, ca], axis=1), (1, reps))
    sin_a = jnp.tile(jnp.concatenate([-sa, sa], axis=1), (1, reps))
    one = jnp.ones((seq, B_NOPE_DIM), F32)
    pad = LANES - B_NOPE_DIM - B_ROPE_DIM
    cos_b = jnp.concatenate([one, cb, cb, jnp.ones((seq, pad), F32)], axis=1)
    sin_b = jnp.concatenate([0 * one, -sb, sb, jnp.zeros((seq, pad), F32)], axis=1)
    ident = jnp.stack([jnp.ones((lc, LANES), F32), jnp.zeros((lc, LANES), F32)] * 2)
    return jnp.concatenate([ident, jnp.stack([cos_a, sin_a, cos_b, sin_b])], axis=1)


def _a_head_perm():
    g = A_HEADS // A_KV_HEADS
    return [h for c in range(g) for h in (c, c + g)]


def _prep_w_in(w_in):
    L, D, _ = w_in.shape
    sizes = [A_HEADS * A_HEAD_DIM, A_KV_HEADS * A_HEAD_DIM, A_KV_HEADS * A_HEAD_DIM, B_Q_RANK, B_KV_RANK,
             B_ROPE_DIM, C_HEADS * C_KEY_DIM, C_HEADS * C_KEY_DIM, C_HEADS * C_VAL_DIM, C_HEADS * C_VAL_DIM,
             2 * C_GATE_RANK, 3 * D]
    offs = [0]
    for s in sizes:
        offs.append(offs[-1] + s)
    aq, ak, av, bq, bkv, bkr, cq, ck, cv, cr, cg, gate = [w_in[:, :, offs[i]:offs[i + 1]] for i in range(len(sizes))]
    aq = aq.reshape(L, D, A_HEADS, A_HEAD_DIM)[:, :, jnp.array(_a_head_perm())].reshape(L, D, -1)
    bkr = jnp.pad(bkr, ((0, 0), (0, 0), (B_NOPE_DIM, LANES - B_NOPE_DIM - B_ROPE_DIM)))
    cg = jnp.pad(cg, ((0, 0), (0, 0), (0, LANES - 2 * C_GATE_RANK)))
    return jnp.concatenate([aq, ak, av, bq, bkv, bkr, cq, ck, cv, cr, cg, gate], axis=-1).astype(BF16)


def kernel(x, c, ctx, c_ctx, w_mod, b_mod, norm_mix, norm_ffn, w_in, a_sink, b_q_norm, b_kv_norm, b_w_uq, b_w_ukv, c_w_gate, c_b_gate, c_head_norm, w_br_a, w_br_b, w_br_c, w_out, w_up, conv_w, conv_b, w_down, final_norm):
    B, S, D = x.shape
    lc = ctx.shape[1]
    L = w_mod.shape[0]
    assert lc == TM and S % TM == 0 and S % GRID_W == 0 and B + 1 <= SUBLANES

    X = jnp.concatenate([ctx, x], axis=1)
    cvec = jnp.zeros((SUBLANES, D), F32).at[0].set(c_ctx).at[1:B + 1].set(c)
    mods = _modulation(cvec, w_mod, b_mod).reshape(L, SUBLANES, 6, D)

    rope = _rope_tables(S, lc)
    perm = jnp.array(_a_head_perm())
    w_in_r = _prep_w_in(w_in)
    sink = jnp.broadcast_to(a_sink[:, perm][:, :, None], (L, A_HEADS, LANES)).astype(F32)
    wa = w_br_a.reshape(L, A_HEADS, A_HEAD_DIM, D)[:, perm].reshape(L, A_HEADS * A_HEAD_DIM, D).astype(BF16)
    wb, wc, wo = w_br_b.astype(BF16), w_br_c.astype(BF16), w_out.astype(BF16)
    qk = B_NOPE_DIM + B_ROPE_DIM
    wuq = jnp.pad(b_w_uq.reshape(L, B_Q_RANK, B_HEADS, qk),
                  ((0, 0), (0, 0), (0, 0), (0, B_HEAD_PAD - qk))).reshape(L, B_Q_RANK, -1).astype(BF16)
    ukv = b_w_ukv.reshape(L, B_KV_RANK, B_HEADS, B_NOPE_DIM + B_V_DIM)
    wkn = jnp.pad(ukv[..., :B_NOPE_DIM],
                  ((0, 0), (0, 0), (0, 0), (0, B_HEAD_PAD - B_NOPE_DIM))).reshape(L, B_KV_RANK, -1).astype(BF16)
    wvb = ukv[..., B_NOPE_DIM:].reshape(L, B_KV_RANK, -1).astype(BF16)
    nqk = C_HEADS * C_KEY_DIM
    wg = jnp.zeros((L, LANES, 2 * nqk), F32)
    wg = wg.at[:, 0:C_GATE_RANK, 0:nqk].set(c_w_gate[:, 0]).at[:, C_GATE_RANK:2 * C_GATE_RANK, nqk:].set(c_w_gate[:, 1])
    bg = c_b_gate.reshape(L, 1, 2 * nqk)
    wup, wdn = w_up.astype(BF16), w_down.astype(BF16)

    for l in range(L):
        mod = mods[l]
        (qa, ka, va, qb, kb, vb, cq, ck, cv, cr, g, gates) = _inproj(
            X, mod, norm_mix[l][None], w_in_r[l], rope, b_q_norm[l][None], b_kv_norm[l][None],
            wuq[l], wkn[l], wvb[l], wg[l], bg[l])
        ya = _attn_a(qa, ka, va, sink[l], lc)
        yb = _mla(qb, kb, vb, lc)
        of, orv = _gla(cq, ck, cv, g)
        X = _merge(X, mod, ya, yb, of, orv, cr, gates, c_head_norm[l][None], wa[l], wb[l], wc[l], wo[l])
        X = _ffn(X, mod, norm_ffn[l][None], wup[l], conv_w[l], conv_b[l][None], wdn[l])
    return _final_norm(X, final_norm[None], lc)
```

```python
import functools

import jax
import jax.numpy as jnp
from jax import lax
from jax.experimental import pallas as pl
from jax.experimental.pallas import tpu as pltpu

GRID_W = 64
EPS = 1e-6
ROPE_BASE = 10000.0
A_HEADS, A_KV_HEADS, A_HEAD_DIM, A_WINDOW = 8, 2, 64, 128
B_HEADS, B_Q_RANK, B_KV_RANK, B_NOPE_DIM, B_ROPE_DIM, B_V_DIM = 8, 256, 128, 64, 32, 64
C_HEADS, C_KEY_DIM, C_VAL_DIM, C_GATE_RANK, C_GATE_TAU, C_CHUNK = 4, 64, 128, 16, 16.0, 64
CONV_W = 3

LANES = 128
SUBLANES = 8
VMEM_BYTES = 64 * 1024 * 1024

TM = 256
GLA_SUB = 16
MLA_TK = 128
MOD_COLS = 1536
F32 = jnp.float32
BF16 = jnp.bfloat16
_HI = lax.Precision.HIGHEST
_NEG_INF = float("-inf")


def _dot(a, b):
    return jnp.dot(a, b, preferred_element_type=F32)


def _dot_nt(a, b):
    return lax.dot_general(a, b, (((1,), (1,)), ((), ())), preferred_element_type=F32)


def _rms(x, w):
    return x * lax.rsqrt(jnp.mean(x * x, axis=-1, keepdims=True) + EPS) * w


def _swap_lane_groups(x, half):
    lane = lax.broadcasted_iota(jnp.int32, x.shape, 1)
    up = pltpu.roll(x, LANES - half, 1)
    down = pltpu.roll(x, half, 1)
    return jnp.where((lane & half) == 0, up, down)


def _vmem_limit(nbytes):
    return int(min(VMEM_BYTES - (4 << 20), max(nbytes, 32 << 20)))


def _const_spec(shape):
    nd = len(shape)
    return pl.BlockSpec(shape, lambda *_: (0,) * nd, pipeline_mode=pl.Buffered(1))


def _mod_kernel(c_ref, w_ref, b_ref, o_ref):
    cv = c_ref[...]
    sc = cv * jax.nn.sigmoid(cv)
    o_ref[...] = jnp.dot(sc, w_ref[...], precision=_HI, preferred_element_type=F32) + b_ref[...]


def _modulation(cvec, w_mod, b_mod):
    L, D, N = w_mod.shape
    nb = N // MOD_COLS
    return pl.pallas_call(
        _mod_kernel,
        grid=(L, nb),
        in_specs=[
            pl.BlockSpec((SUBLANES, D), lambda l, j: (0, 0)),
            pl.BlockSpec((None, D, MOD_COLS), lambda l, j: (l, 0, j)),
            pl.BlockSpec((None, 1, MOD_COLS), lambda l, j: (l, 0, j)),
        ],
        out_specs=pl.BlockSpec((None, SUBLANES, MOD_COLS), lambda l, j: (l, 0, j)),
        out_shape=jax.ShapeDtypeStruct((L, SUBLANES, N), F32),
        compiler_params=pltpu.CompilerParams(
            dimension_semantics=("parallel", "parallel"),
            vmem_limit_bytes=_vmem_limit(3 * D * MOD_COLS * 4)),
    )(cvec, w_mod, b_mod.reshape(L, 1, N))


_C_AQ = 0
_C_AK = _C_AQ + A_HEADS * A_HEAD_DIM
_C_AV = _C_AK + A_KV_HEADS * A_HEAD_DIM
_C_BQ = _C_AV + A_KV_HEADS * A_HEAD_DIM
_C_BKV = _C_BQ + B_Q_RANK
_C_BKR = _C_BKV + B_KV_RANK
_C_CQ = _C_BKR + LANES
_C_CK = _C_CQ + C_HEADS * C_KEY_DIM
_C_CV = _C_CK + C_HEADS * C_KEY_DIM
_C_CR = _C_CV + C_HEADS * C_VAL_DIM
_C_CG = _C_CR + C_HEADS * C_VAL_DIM
_C_GATE = _C_CG + LANES
B_HEAD_PAD = LANES


def _inproj_kernel(x_ref, mod_ref, nw_ref, w_ref, rope_ref, bqn_ref, bkvn_ref, wuq_ref, wkn_ref,
                   wvb_ref, wg_ref, bg_ref,
                   qa_ref, ka_ref, va_ref, qb_ref, kb_ref, vb_ref, cq_ref, ck_ref, cv_ref, cr_ref,
                   g_ref, gate_ref, *, d_model):
    mod = mod_ref[...]
    h = _rms(x_ref[...], nw_ref[...]) * (1.0 + mod[1:2]) + mod[0:1]
    hb = h.astype(BF16)

    def proj(lo, hi):
        return _dot(hb, w_ref[:, lo:hi])

    cos_a, sin_a, cos_b, sin_b = rope_ref[0], rope_ref[1], rope_ref[2], rope_ref[3]

    def rope_a(t):
        return t * cos_a + _swap_lane_groups(t, A_HEAD_DIM // 2) * sin_a

    def rope_b(t):
        return t * cos_b + _swap_lane_groups(t, B_ROPE_DIM // 2) * sin_b

    a_scale = A_HEAD_DIM ** -0.5
    for c in range(A_HEADS * A_HEAD_DIM // LANES):
        t = proj(_C_AQ + c * LANES, _C_AQ + (c + 1) * LANES)
        qa_ref[:, c * LANES:(c + 1) * LANES] = (rope_a(t) * a_scale).astype(BF16)
    ka_ref[...] = rope_a(proj(_C_AK, _C_AV)).astype(BF16)
    va_ref[...] = proj(_C_AV, _C_BQ).astype(BF16)

    b_scale = (B_NOPE_DIM + B_ROPE_DIM) ** -0.5
    cqn = _rms(proj(_C_BQ, _C_BKV), bqn_ref[...]).astype(BF16)
    ckvn = _rms(proj(_C_BKV, _C_BKR), bkvn_ref[...]).astype(BF16)
    kr = rope_b(proj(_C_BKR, _C_CQ))
    for hd in range(B_HEADS):
        sl = slice(hd * B_HEAD_PAD, (hd + 1) * B_HEAD_PAD)
        qb_ref[sl, :] = (rope_b(_dot(cqn, wuq_ref[:, sl])) * b_scale).T.astype(BF16)
        kb_ref[:, sl] = (_dot(ckvn, wkn_ref[:, sl]) + kr).astype(BF16)
    vb = _dot(ckvn, wvb_ref[...])
    tk = vb_ref.shape[2]
    for cc in range(vb_ref.shape[0]):
        for c in range(vb.shape[1] // LANES):
            vb_ref[cc, c * LANES:(c + 1) * LANES, :] = (
                vb[cc * tk:(cc + 1) * tk, c * LANES:(c + 1) * LANES].T.astype(BF16))

    cq_ref[...] = proj(_C_CQ, _C_CK) * (C_KEY_DIM ** -0.5)
    ck_ref[...] = proj(_C_CK, _C_CV)
    cv_ref[...] = proj(_C_CV, _C_CR).astype(BF16)
    cr_ref[...] = proj(_C_CR, _C_CG)
    z = jnp.dot(proj(_C_CG, _C_GATE), wg_ref[...], precision=_HI, preferred_element_type=F32) + bg_ref[...]
    g_ref[...] = (jnp.minimum(z, 0.0) - jnp.log1p(jnp.exp(-jnp.abs(z)))) * (1.0 / C_GATE_TAU)

    gate_ref[...] = proj(_C_GATE, _C_GATE + 3 * d_model)


def _mod_index(b, t):
    return (jnp.where(t == 0, 0, b + 1), 0, 0)


def _inproj(X, mod, nw, w, rope, bqn, bkvn, wuq, wkn, wvb, wg, bg):
    B, T, D = X.shape
    NT = T // TM
    n_in = w.shape[1]
    row = lambda n: pl.BlockSpec((None, TM, n), lambda b, t: (b, t, 0))
    widths = [(A_HEADS * A_HEAD_DIM, BF16), (A_KV_HEADS * A_HEAD_DIM, BF16), (A_KV_HEADS * A_HEAD_DIM, BF16),
              (B_HEADS * B_HEAD_PAD, BF16), (B_HEADS * B_HEAD_PAD, BF16), (B_HEADS * B_V_DIM, BF16),
              (C_HEADS * C_KEY_DIM, F32), (C_HEADS * C_KEY_DIM, F32), (C_HEADS * C_VAL_DIM, BF16),
              (C_HEADS * C_VAL_DIM, F32), (2 * C_HEADS * C_KEY_DIM, F32), (3 * D, F32)]
    out_bytes = sum(TM * n * jnp.dtype(dt).itemsize for n, dt in widths)
    vmem = w.size * 2 + 4 * out_bytes + 6 * TM * D * 4 + (8 << 20)
    out_specs = [row(n) for n, _ in widths]
    out_shape = [jax.ShapeDtypeStruct((B, T, n), dt) for n, dt in widths]
    qw, vw = B_HEADS * B_HEAD_PAD, B_HEADS * B_V_DIM
    out_specs[3] = pl.BlockSpec((None, qw, TM), lambda b, t: (b, 0, t))
    out_shape[3] = jax.ShapeDtypeStruct((B, qw, T), BF16)
    out_specs[5] = pl.BlockSpec((None, TM // MLA_TK, vw, MLA_TK), lambda b, t: (b, t, 0, 0))
    out_shape[5] = jax.ShapeDtypeStruct((B, T // MLA_TK, vw, MLA_TK), BF16)
    return pl.pallas_call(
        functools.partial(_inproj_kernel, d_model=D),
        grid=(B, NT),
        in_specs=[
            row(D),
            pl.BlockSpec((None, 6, D), _mod_index),
            _const_spec((1, D)),
            _const_spec((D, n_in)),
            pl.BlockSpec((4, TM, LANES), lambda b, t: (0, t, 0)),
            _const_spec(bqn.shape), _const_spec(bkvn.shape), _const_spec(wuq.shape),
            _const_spec(wkn.shape), _const_spec(wvb.shape), _const_spec(wg.shape), _const_spec(bg.shape),
        ],
        out_specs=out_specs,
        out_shape=out_shape,
        compiler_params=pltpu.CompilerParams(
            dimension_semantics=("parallel", "parallel"), vmem_limit_bytes=_vmem_limit(vmem)),
    )(X, mod, nw, w, rope, bqn, bkvn, wuq, wkn, wvb, wg, bg)


def _attn_a_kernel(q_ref, kp_ref, km_ref, kn_ref, kc_ref, vp_ref, vm_ref, vn_ref, vc_ref, sink_ref,
                   o_ref, *, seq):
    t = pl.program_id(1)
    kloc = jnp.concatenate([kp_ref[...], km_ref[...], kn_ref[...]], axis=0)
    vloc = jnp.concatenate([vp_ref[...], vm_ref[...], vn_ref[...]], axis=0)
    kctx, vctx = kc_ref[...], vc_ref[...]
    nloc = kloc.shape[0]
    r = lax.broadcasted_iota(jnp.int32, (TM, nloc), 0)
    c = lax.broadcasted_iota(jnp.int32, (TM, nloc), 1)
    rel = c - A_WINDOW - r
    kpos = (t - 1) * TM - A_WINDOW + c
    valid = (jnp.abs(rel) <= A_WINDOW) & (kpos >= 0) & (kpos < seq) & (t >= 1)
    lane_q = lax.broadcasted_iota(jnp.int32, (TM, LANES), 1)
    lane_l = lax.broadcasted_iota(jnp.int32, (nloc, LANES), 1)
    lane_c = lax.broadcasted_iota(jnp.int32, (kctx.shape[0], LANES), 1)
    zero = jnp.zeros((), BF16)
    v_l = [jnp.where((lane_l >= hf * A_HEAD_DIM) & (lane_l < (hf + 1) * A_HEAD_DIM), vloc, zero) for hf in range(2)]
    v_c = [jnp.where((lane_c >= hf * A_HEAD_DIM) & (lane_c < (hf + 1) * A_HEAD_DIM), vctx, zero) for hf in range(2)]
    heads = [(col, hf) for col in range(A_HEADS * A_HEAD_DIM // LANES) for hf in range(2)]

    def scores(col, hf):
        qcol = q_ref[:, col * LANES:(col + 1) * LANES]
        qm = jnp.where((lane_q >= hf * A_HEAD_DIM) & (lane_q < (hf + 1) * A_HEAD_DIM), qcol, zero)
        return _dot_nt(qm, kloc), _dot_nt(qm, kctx)

    cur = scores(*heads[0])
    out = None
    for i, (col, hf) in enumerate(heads):
        nxt = scores(*heads[i + 1]) if i + 1 < len(heads) else None
        s_loc = jnp.where(valid, cur[0], _NEG_INF)
        s_ctx = cur[1]
        sink = sink_ref[2 * col + hf:2 * col + hf + 1, 0:1]
        m = jnp.maximum(jnp.maximum(jnp.max(s_loc, axis=1, keepdims=True),
                                    jnp.max(s_ctx, axis=1, keepdims=True)), sink)
        p_loc = jnp.exp(s_loc - m)
        p_ctx = jnp.exp(s_ctx - m)
        denom = (jnp.sum(p_loc, axis=1, keepdims=True) + jnp.sum(p_ctx, axis=1, keepdims=True)
                 + jnp.exp(sink - m))
        o = (_dot(p_loc.astype(BF16), v_l[hf]) + _dot(p_ctx.astype(BF16), v_c[hf])) / denom
        out = o if hf == 0 else out + o
        if hf == 1:
            o_ref[:, col * LANES:(col + 1) * LANES] = out.astype(BF16)
        cur = nxt


def _attn_a(qa, ka, va, sink, lc):
    B, T, _ = qa.shape
    NT = T // TM
    seq = T - lc
    kvw = A_KV_HEADS * A_HEAD_DIM
    per_tm = TM // A_WINDOW
    last = T // A_WINDOW - 1
    prev = pl.BlockSpec((None, A_WINDOW, kvw), lambda b, t: (b, jnp.maximum(t * per_tm - 1, 0), 0))
    main = pl.BlockSpec((None, TM, kvw), lambda b, t: (b, t, 0))
    nxt = pl.BlockSpec((None, A_WINDOW, kvw), lambda b, t: (b, jnp.minimum((t + 1) * per_tm, last), 0))
    ctx = pl.BlockSpec((None, lc, kvw), lambda b, t: (b, 0, 0))
    return pl.pallas_call(
        functools.partial(_attn_a_kernel, seq=seq),
        grid=(B, NT),
        in_specs=[pl.BlockSpec((None, TM, qa.shape[2]), lambda b, t: (b, t, 0)),
                  prev, main, nxt, ctx, prev, main, nxt, ctx,
                  pl.BlockSpec(sink.shape, lambda b, t: (0, 0))],
        out_specs=pl.BlockSpec((None, TM, qa.shape[2]), lambda b, t: (b, t, 0)),
        out_shape=jax.ShapeDtypeStruct(qa.shape, BF16),
        compiler_params=pltpu.CompilerParams(dimension_semantics=("parallel", "parallel")),
    )(qa, ka, ka, ka, ka, va, va, va, va, sink)


def _mla_kernel(q_ref, k_ref, v_ref, o_ref, m_ref, l_ref, acc_ref, sa_ref, sb_ref, *, lc, tk):
    t = pl.program_id(1)
    total = k_ref.shape[0]
    nchunks = jnp.where(t == 0, lc // tk, total // tk)
    m_ref[...] = jnp.full(m_ref.shape, _NEG_INF, F32)
    l_ref[...] = jnp.zeros(l_ref.shape, F32)
    acc_ref[...] = jnp.zeros(acc_ref.shape, F32)

    def scores(j, hd):
        qs = slice(hd * B_HEAD_PAD, (hd + 1) * B_HEAD_PAD)
        return _dot(k_ref[pl.ds(pl.multiple_of(j * tk, tk), tk), qs], q_ref[qs, :])

    for hd in range(B_HEADS):
        sa_ref[hd] = scores(0, hd)

    def step(j, cur_ref, nxt_ref):
        nxt = jnp.minimum(j + 1, nchunks - 1)
        for hd in range(B_HEADS):
            vs = slice(hd * B_V_DIM, (hd + 1) * B_V_DIM)
            nxt_ref[hd] = scores(nxt, hd)
            s = cur_ref[hd]
            m_prev = m_ref[hd]
            m_new = jnp.maximum(m_prev, jnp.max(s, axis=0, keepdims=True))
            alpha = jnp.exp(m_prev - m_new)
            p = jnp.exp(s - m_new[0:1])
            l_ref[hd] = alpha * l_ref[hd] + jnp.sum(p, axis=0, keepdims=True)
            m_ref[hd] = m_new
            acc_ref[vs, :] = acc_ref[vs, :] * alpha[0:1] + _dot(v_ref[j, vs, :], p.astype(BF16))

    def body(j2, carry):
        step(2 * j2, sa_ref, sb_ref)
        step(2 * j2 + 1, sb_ref, sa_ref)
        return carry

    lax.fori_loop(0, nchunks // 2, body, 0)
    per = LANES // B_V_DIM
    for c in range(B_HEADS // per):
        o_t = jnp.concatenate(
            [acc_ref[hd * B_V_DIM:(hd + 1) * B_V_DIM, :] / l_ref[hd][0:1]
             for hd in range(c * per, (c + 1) * per)], axis=0)
        o_ref[:, c * LANES:(c + 1) * LANES] = o_t.T.astype(BF16)


def _mla(qb, kb, vb, lc):
    B, qw, T = qb.shape
    NT = T // TM
    _, nck, vw, tk = vb.shape
    vmem = (T * (qw + vw) * 2 + 4 * TM * qw * 2 + (2 * B_HEADS * SUBLANES * TM + 3 * TM * vw) * 4
            + 2 * B_HEADS * tk * TM * 4 + (16 << 20))
    return pl.pallas_call(
        functools.partial(_mla_kernel, lc=lc, tk=tk),
        grid=(B, NT),
        in_specs=[pl.BlockSpec((None, qw, TM), lambda b, t: (b, 0, t)),
                  pl.BlockSpec((None, T, qw), lambda b, t: (b, 0, 0), pipeline_mode=pl.Buffered(1)),
                  pl.BlockSpec((None, nck, vw, tk), lambda b, t: (b, 0, 0, 0), pipeline_mode=pl.Buffered(1))],
        out_specs=pl.BlockSpec((None, TM, vw), lambda b, t: (b, t, 0)),
        out_shape=jax.ShapeDtypeStruct((B, T, vw), BF16),
        scratch_shapes=[pltpu.VMEM((B_HEADS, SUBLANES, TM), F32), pltpu.VMEM((B_HEADS, SUBLANES, TM), F32),
                        pltpu.VMEM((vw, TM), F32),
                        pltpu.VMEM((B_HEADS, tk, TM), F32), pltpu.VMEM((B_HEADS, tk, TM), F32)],
        compiler_params=pltpu.CompilerParams(
            dimension_semantics=("parallel", "arbitrary"), vmem_limit_bytes=_vmem_limit(vmem)),
    )(qb, kb, vb)


def _gla_prep(q, k, b, reverse):
    C = q.shape[0]
    dk2 = q.shape[1]
    nsub = C // GLA_SUB
    b_end = b[0:1] if reverse else b[C - 1:C]
    row = lax.broadcasted_iota(jnp.int32, (C, dk2), 0)
    lane = lax.broadcasted_iota(jnp.int32, (C, dk2), 1)

    qe = (q * jnp.exp(b)).astype(BF16)
    kd_t = (k * jnp.exp(b_end - b)).T.astype(BF16)
    dec = jnp.exp(jnp.broadcast_to(b_end, (dk2, dk2)).T)
    dec2 = jnp.concatenate([dec] * (2 * C_VAL_DIM // dk2), axis=1)

    refs = [blk * GLA_SUB + (GLA_SUB - 1 if reverse else 0) for blk in range(nsub)]
    b_ref = jnp.concatenate([jnp.broadcast_to(b[n:n + 1], (GLA_SUB, dk2)) for n in refs], axis=0)
    q_sc = q * jnp.exp(b - b_ref)
    k_parts = []
    for blk, n in enumerate(refs):
        in_range = (row >= blk * GLA_SUB) if reverse else (row < (blk + 1) * GLA_SUB)
        k_parts.append(jnp.where(in_range, k * jnp.exp(b[n:n + 1] - b), 0.0))
    k_big = jnp.concatenate(k_parts, axis=1).astype(BF16)
    q_halves = []
    for hf in range(2):
        in_half = (lane >= hf * C_KEY_DIM) & (lane < (hf + 1) * C_KEY_DIM)
        q_halves.append(jnp.concatenate(
            [jnp.where(in_half & (row >= blk * GLA_SUB) & (row < (blk + 1) * GLA_SUB), q_sc, 0.0)
             for blk in range(nsub)], axis=1))
    q_big = jnp.concatenate(q_halves, axis=0).astype(BF16)
    return qe, kd_t, dec2, q_big, k_big


def _gla_kernel(qf_ref, kf_ref, vf_ref, gf_ref, qr_ref, kr_ref, vr_ref, gr_ref, of_ref, or_ref,
                sf_ref, sr_ref):
    t = pl.program_id(1)

    @pl.when(t == 0)
    def _():
        sf_ref[...] = jnp.zeros(sf_ref.shape, F32)
        sr_ref[...] = jnp.zeros(sr_ref.shape, F32)

    C = C_CHUNK
    nchunk = TM // C
    npair = C_HEADS // 2
    kw, vw = 2 * C_KEY_DIM, 2 * C_VAL_DIM
    dirs = ((qf_ref, kf_ref, vf_ref, gf_ref, of_ref, sf_ref, False),
            (qr_ref, kr_ref, vr_ref, gr_ref, or_ref, sr_ref, True))

    ii = lax.broadcasted_iota(jnp.int32, (TM, TM), 0)
    jj = lax.broadcasted_iota(jnp.int32, (TM, TM), 1)
    same_chunk = (ii // C) == (jj // C)
    b_all = []
    for (_, _, _, g_ref, _, _, reverse) in dirs:
        tri = jnp.where(same_chunk & ((jj >= ii) if reverse else (jj <= ii)), 1.0, 0.0).astype(F32)
        b_all.append(jnp.dot(tri, g_ref[...], precision=_HI, preferred_element_type=F32))

    units = []
    for d, (q_ref, k_ref, v_ref, _, o_ref, _, reverse) in enumerate(dirs):
        for p in range(npair):
            ks, vsl = slice(p * kw, (p + 1) * kw), slice(p * vw, (p + 1) * vw)
            for c in range(nchunk):
                rows = slice(c * C, (c + 1) * C)
                prep = _gla_prep(q_ref[rows, ks], k_ref[rows, ks], b_all[d][rows, ks], reverse)
                units.append((d, p, c, rows, vsl, v_ref, o_ref, reverse, prep))

    srow = lax.broadcasted_iota(jnp.int32, (kw, vw), 0)
    scol = lax.broadcasted_iota(jnp.int32, (kw, vw), 1)
    on_diag = (srow < C_KEY_DIM) == (scol < C_VAL_DIM)
    qi = lax.broadcasted_iota(jnp.int32, (2 * C, C), 0) % C
    kj = lax.broadcasted_iota(jnp.int32, (2 * C, C), 1)
    kvs, atts = [], []
    for (d, p, c, rows, vsl, v_ref, o_ref, reverse, prep) in units:
        _, kd_t, _, q_big, k_big = prep
        kvs.append(jnp.where(on_diag, _dot(kd_t, v_ref[rows, vsl]), 0.0))
        keep = (kj >= qi) if reverse else (kj <= qi)
        atts.append(jnp.where(keep, _dot_nt(q_big, k_big), 0.0).astype(BF16))

    intra = {}
    for u, (d, p, c, rows, vsl, v_ref, o_ref, reverse, prep) in enumerate(units):
        pv = _dot(atts[u], v_ref[rows, vsl])
        intra[(d, p, c)] = (u, jnp.concatenate([pv[0:C, 0:C_VAL_DIM], pv[C:2 * C, C_VAL_DIM:]], axis=1))

    states = {(d, p): dirs[d][5][p] for d in range(len(dirs)) for p in range(npair)}
    for step in range(nchunk):
        for d in range(len(dirs)):
            reverse = dirs[d][6]
            c = nchunk - 1 - step if reverse else step
            for p in range(npair):
                u, o_intra = intra[(d, p, c)]
                _, _, _, rows, vsl, _, o_ref, _, prep = units[u]
                qe, _, dec2, _, _ = prep
                o_ref[rows, vsl] = o_intra + _dot(qe, states[(d, p)].astype(BF16))
                states[(d, p)] = dec2 * states[(d, p)] + kvs[u]
    for d in range(len(dirs)):
        for p in range(npair):
            dirs[d][5][p] = states[(d, p)]


def _gla(cq, ck, cv, g):
    B, T, kw = cq.shape
    NT = T // TM
    vw = cv.shape[2]

    def fwd(n, col=0):
        return pl.BlockSpec((None, TM, n), lambda b, t: (b, t, col))

    def rev(n, col=0):
        return pl.BlockSpec((None, TM, n), lambda b, t: (b, jnp.where(t == 0, 0, NT - t), col))

    return pl.pallas_call(
        _gla_kernel,
        grid=(B, NT),
        in_specs=[fwd(kw), fwd(kw), fwd(vw), fwd(kw, 0), rev(kw), rev(kw), rev(vw), rev(kw, 1)],
        out_specs=[fwd(vw), rev(vw)],
        out_shape=[jax.ShapeDtypeStruct((B, T, vw), F32)] * 2,
        scratch_shapes=[pltpu.VMEM((C_HEADS // 2, 2 * C_KEY_DIM, 2 * C_VAL_DIM), F32)] * 2,
        compiler_params=pltpu.CompilerParams(dimension_semantics=("parallel", "arbitrary")),
    )(cq, ck, cv, g, cq, ck, cv, g)


def _merge_kernel(x_ref, mod_ref, ya_ref, yb_ref, of_ref, or_ref, cr_ref, gate_ref, hn_ref,
                  wa_ref, wb_ref, wc_ref, wo_ref, o_ref):
    d = x_ref.shape[1]
    o = of_ref[...] + or_ref[...]
    r = cr_ref[...]
    parts = []
    for hd in range(C_HEADS):
        sl = slice(hd * C_VAL_DIM, (hd + 1) * C_VAL_DIM)
        parts.append(_rms(o[:, sl], hn_ref[:, sl]))
    yc = (jnp.concatenate(parts, axis=1) * (r * jax.nn.sigmoid(r))).astype(BF16)
    gates = jax.nn.sigmoid(gate_ref[...])
    m = (gates[:, 0:d] * _dot(ya_ref[...], wa_ref[...])
         + gates[:, d:2 * d] * _dot(yb_ref[...], wb_ref[...])
         + gates[:, 2 * d:3 * d] * _dot(yc, wc_ref[...]))
    y = _dot(m.astype(BF16), wo_ref[...])
    o_ref[...] = x_ref[...] + mod_ref[2:3, :] * y


def _merge(X, mod, ya, yb, of, orv, cr, gates, hn, wa, wb, wc, wo):
    B, T, D = X.shape
    NT = T // TM
    row = lambda n: pl.BlockSpec((None, TM, n), lambda b, t: (b, t, 0))
    vmem = (wa.size + wb.size + wc.size + wo.size) * 2 + 2 * TM * (2 * D + 3 * D + 4 * ya.shape[2]) * 4 + (16 << 20)
    return pl.pallas_call(
        _merge_kernel,
        grid=(B, NT),
        in_specs=[row(D), pl.BlockSpec((None, 6, D), _mod_index),
                  row(ya.shape[2]), row(yb.shape[2]), row(of.shape[2]), row(orv.shape[2]),
                  row(cr.shape[2]), row(gates.shape[2]), _const_spec(hn.shape),
                  _const_spec(wa.shape), _const_spec(wb.shape), _const_spec(wc.shape), _const_spec(wo.shape)],
        out_specs=row(D),
        out_shape=jax.ShapeDtypeStruct(X.shape, F32),
        compiler_params=pltpu.CompilerParams(
            dimension_semantics=("parallel", "parallel"), vmem_limit_bytes=_vmem_limit(vmem)),
    )(X, mod, ya, yb, of, orv, cr, gates, hn, wa, wb, wc, wo)


def _ffn_kernel(x_ref, xp_ref, xn_ref, mod_ref, nw_ref, wup_ref, cw_ref, cb_ref, wdn_ref, o_ref,
                h_ref, ug_ref, uv_ref, *, nt, ncb):
    t = pl.program_id(1)
    mod = mod_ref[...]
    nw = nw_ref[...]
    halo = xp_ref.shape[0]
    ffn = wdn_ref.shape[0]
    cb = ffn // ncb

    def norm(xv):
        return _rms(xv, nw) * (1.0 + mod[4:5]) + mod[3:4]

    has_prev = (t >= 2)
    has_next = (t >= 1) & (t < nt - 1)
    h_ref[0:halo, :] = jnp.where(has_prev, norm(xp_ref[...]), 0.0).astype(BF16)
    h_ref[halo:halo + TM, :] = norm(x_ref[...]).astype(BF16)
    h_ref[halo + TM:2 * halo + TM, :] = jnp.where(has_next, norm(xn_ref[...]), 0.0).astype(BF16)
    hb = h_ref[...]

    acc = jnp.zeros((TM, x_ref.shape[1]), F32)
    for j in range(ncb):
        ug_ref[...] = _dot(hb, wup_ref[:, j * cb:(j + 1) * cb])
        uv_ref[...] = _dot(hb, wup_ref[:, ffn + j * cb:ffn + (j + 1) * cb])

        def conv(u_ref, lo):
            w = cw_ref[:, lo:lo + cb]
            return (u_ref[halo - 1:halo - 1 + TM, :] * w[0:1] + u_ref[halo:halo + TM, :] * w[1:2]
                    + u_ref[halo + 1:halo + 1 + TM, :] * w[2:3] + cb_ref[:, lo:lo + cb])

        gt = conv(ug_ref, j * cb)
        val = conv(uv_ref, ffn + j * cb)
        act = (gt * jax.nn.sigmoid(gt) * val).astype(BF16)
        acc = acc + _dot(act, wdn_ref[j * cb:(j + 1) * cb, :])
    o_ref[...] = x_ref[...] + mod[5:6] * acc


def _ffn(X, mod, nw, wup, cw, cb, wdn, ncb=2):
    B, T, D = X.shape
    NT = T // TM
    halo = 2 * SUBLANES
    per = TM // halo
    last = T // halo - 1
    ffn = wdn.shape[0]
    cbw = ffn // ncb
    vmem = (wup.size + wdn.size) * 2 + 2 * (TM + 2 * halo) * cbw * 4 + 8 * TM * D * 4 + (16 << 20)
    return pl.pallas_call(
        functools.partial(_ffn_kernel, nt=NT, ncb=ncb),
        grid=(B, NT),
        in_specs=[pl.BlockSpec((None, TM, D), lambda b, t: (b, t, 0)),
                  pl.BlockSpec((None, halo, D), lambda b, t: (b, jnp.maximum(t * per - 1, 0), 0)),
                  pl.BlockSpec((None, halo, D), lambda b, t: (b, jnp.minimum((t + 1) * per, last), 0)),
                  pl.BlockSpec((None, 6, D), _mod_index),
                  _const_spec(nw.shape), _const_spec(wup.shape), _const_spec(cw.shape), _const_spec(cb.shape),
                  _const_spec(wdn.shape)],
        out_specs=pl.BlockSpec((None, TM, D), lambda b, t: (b, t, 0)),
        out_shape=jax.ShapeDtypeStruct(X.shape, F32),
        scratch_shapes=[pltpu.VMEM((TM + 2 * halo, D), BF16), pltpu.VMEM((TM + 2 * halo, cbw), F32),
                        pltpu.VMEM((TM + 2 * halo, cbw), F32)],
        compiler_params=pltpu.CompilerParams(
            dimension_semantics=("parallel", "parallel"), vmem_limit_bytes=_vmem_limit(vmem)),
    )(X, X, X, mod, nw, wup, cw, cb, wdn)


def _final_kernel(x_ref, w_ref, o_ref):
    o_ref[...] = _rms(x_ref[...], w_ref[...])


def _final_norm(X, w, lc):
    B, T, D = X.shape
    S = T - lc
    skip = lc // TM
    return pl.pallas_call(
        _final_kernel,
        grid=(B, S // TM),
        in_specs=[pl.BlockSpec((None, TM, D), lambda b, t: (b, t + skip, 0)), _const_spec(w.shape)],
        out_specs=pl.BlockSpec((None, TM, D), lambda b, t: (b, t, 0)),
        out_shape=jax.ShapeDtypeStruct((B, S, D), F32),
        compiler_params=pltpu.CompilerParams(dimension_semantics=("parallel", "parallel")),
    )(X, w)


def _rope_tables(seq, lc):
    rows = seq // GRID_W
    row = jnp.broadcast_to(jnp.arange(rows)[:, None], (rows, GRID_W)).reshape(-1).astype(F32)
    col = jnp.broadcast_to(jnp.arange(GRID_W)[None, :], (rows, GRID_W)).reshape(-1).astype(F32)

    def cs(rot_dim):
        n_freq = rot_dim // 4
        inv = ROPE_BASE ** (-jnp.arange(n_freq, dtype=F32) / n_freq)
        ang = jnp.concatenate([row[:, None] * inv, col[:, None] * inv], axis=-1)
        return jnp.cos(ang), jnp.sin(ang)

    ca, sa = cs(A_HEAD_DIM)
    cb, sb = cs(B_ROPE_DIM)
    reps = LANES // A_HEAD_DIM
    cos_a = jnp.tile(jnp.concatenate([ca, ca], axis=1), (1, reps))
    sin_a = jnp.tile(jnp.concatenate([-sa, sa], axis=1), (1, reps))
    one = jnp.ones((seq, B_NOPE_DIM), F32)
    pad = LANES - B_NOPE_DIM - B_ROPE_DIM
    cos_b = jnp.concatenate([one, cb, cb, jnp.ones((seq, pad), F32)], axis=1)
    sin_b = jnp.concatenate([0 * one, -sb, sb, jnp.zeros((seq, pad), F32)], axis=1)
    ident = jnp.stack([jnp.ones((lc, LANES), F32), jnp.zeros((lc, LANES), F32)] * 2)
    return jnp.concatenate([ident, jnp.stack([cos_a, sin_a, cos_b, sin_b])], axis=1)


def _a_head_perm():
    g = A_HEADS // A_KV_HEADS
    return [h for c in range(g) for h in (c, c + g)]


def _prep_w_in(w_in):
    L, D, _ = w_in.shape
    sizes = [A_HEADS * A_HEAD_DIM, A_KV_HEADS * A_HEAD_DIM, A_KV_HEADS * A_HEAD_DIM, B_Q_RANK, B_KV_RANK,
             B_ROPE_DIM, C_HEADS * C_KEY_DIM, C_HEADS * C_KEY_DIM, C_HEADS * C_VAL_DIM, C_HEADS * C_VAL_DIM,
             2 * C_GATE_RANK, 3 * D]
    offs = [0]
    for s in sizes:
        offs.append(offs[-1] + s)
    aq, ak, av, bq, bkv, bkr, cq, ck, cv, cr, cg, gate = [w_in[:, :, offs[i]:offs[i + 1]] for i in range(len(sizes))]
    aq = aq.reshape(L, D, A_HEADS, A_HEAD_DIM)[:, :, jnp.array(_a_head_perm())].reshape(L, D, -1)
    bkr = jnp.pad(bkr, ((0, 0), (0, 0), (B_NOPE_DIM, LANES - B_NOPE_DIM - B_ROPE_DIM)))
    cg = jnp.pad(cg, ((0, 0), (0, 0), (0, LANES - 2 * C_GATE_RANK)))
    return jnp.concatenate([aq, ak, av, bq, bkv, bkr, cq, ck, cv, cr, cg, gate], axis=-1).astype(BF16)


def kernel(x, c, ctx, c_ctx, w_mod, b_mod, norm_mix, norm_ffn, w_in, a_sink, b_q_norm, b_kv_norm, b_w_uq, b_w_ukv, c_w_gate, c_b_gate, c_head_norm, w_br_a, w_br_b, w_br_c, w_out, w_up, conv_w, conv_b, w_down, final_norm):
    B, S, D = x.shape
    lc = ctx.shape[1]
    L = w_mod.shape[0]
    assert lc == TM and S % TM == 0 and S % GRID_W == 0 and B + 1 <= SUBLANES

    X = jnp.concatenate([ctx, x], axis=1)
    cvec = jnp.zeros((SUBLANES, D), F32).at[0].set(c_ctx).at[1:B + 1].set(c)
    mods = _modulation(cvec, w_mod, b_mod).reshape(L, SUBLANES, 6, D)

    rope = _rope_tables(S, lc)
    perm = jnp.array(_a_head_perm())
    w_in_r = _prep_w_in(w_in)
    sink = jnp.broadcast_to(a_sink[:, perm][:, :, None], (L, A_HEADS, LANES)).astype(F32)
    wa = w_br_a.reshape(L, A_HEADS, A_HEAD_DIM, D)[:, perm].reshape(L, A_HEADS * A_HEAD_DIM, D).astype(BF16)
    wb, wc, wo = w_br_b.astype(BF16), w_br_c.astype(BF16), w_out.astype(BF16)
    qk = B_NOPE_DIM + B_ROPE_DIM
    wuq = jnp.pad(b_w_uq.reshape(L, B_Q_RANK, B_HEADS, qk),
                  ((0, 0), (0, 0), (0, 0), (0, B_HEAD_PAD - qk))).reshape(L, B_Q_RANK, -1).astype(BF16)
    ukv = b_w_ukv.reshape(L, B_KV_RANK, B_HEADS, B_NOPE_DIM + B_V_DIM)
    wkn = jnp.pad(ukv[..., :B_NOPE_DIM],
                  ((0, 0), (0, 0), (0, 0), (0, B_HEAD_PAD - B_NOPE_DIM))).reshape(L, B_KV_RANK, -1).astype(BF16)
    wvb = ukv[..., B_NOPE_DIM:].reshape(L, B_KV_RANK, -1).astype(BF16)
    nqk = C_HEADS * C_KEY_DIM
    wg = jnp.zeros((L, LANES, 2 * nqk), F32)
    wg = wg.at[:, 0:C_GATE_RANK, 0:nqk].set(c_w_gate[:, 0]).at[:, C_GATE_RANK:2 * C_GATE_RANK, nqk:].set(c_w_gate[:, 1])
    bg = c_b_gate.reshape(L, 1, 2 * nqk)
    wup, wdn = w_up.astype(BF16), w_down.astype(BF16)

    for l in range(L):
        mod = mods[l]
        (qa, ka, va, qb, kb, vb, cq, ck, cv, cr, g, gates) = _inproj(
            X, mod, norm_mix[l][None], w_in_r[l], rope, b_q_norm[l][None], b_kv_norm[l][None],
            wuq[l], wkn[l], wvb[l], wg[l], bg[l])
        ya = _attn_a(qa, ka, va, sink[l], lc)
        yb = _mla(qb, kb, vb, lc)
        of, orv = _gla(cq, ck, cv, g)
        X = _merge(X, mod, ya, yb, of, orv, cr, gates, c_head_norm[l][None], wa[l], wb[l], wc[l], wo[l])
        X = _ffn(X, mod, norm_ffn[l][None], wup[l], conv_w[l], conv_b[l][None], wdn[l])
    return _final_norm(X, final_norm[None], lc)
```

```python
import functools

import jax
import jax.numpy as jnp
from jax import lax
from jax.experimental import pallas as pl
from jax.experimental.pallas import tpu as pltpu

GRID_W = 64
EPS = 1e-6
ROPE_BASE = 10000.0
A_HEADS, A_KV_HEADS, A_HEAD_DIM, A_WINDOW = 8, 2, 64, 128
B_HEADS, B_Q_RANK, B_KV_RANK, B_NOPE_DIM, B_ROPE_DIM, B_V_DIM = 8, 256, 128, 64, 32, 64
C_HEADS, C_KEY_DIM, C_VAL_DIM, C_GATE_RANK, C_GATE_TAU, C_CHUNK = 4, 64, 128, 16, 16.0, 64
CONV_W = 3

LANES = 128
SUBLANES = 8
VMEM_BYTES = 64 * 1024 * 1024

TM = 256
GLA_SUB = 16
MLA_TK = 128
MOD_COLS = 1536
F32 = jnp.float32
BF16 = jnp.bfloat16
_HI = lax.Precision.HIGHEST
_NEG_INF = float("-inf")


def _dot(a, b):
    return jnp.dot(a, b, preferred_element_type=F32)


def _dot_nt(a, b):
    return lax.dot_general(a, b, (((1,), (1,)), ((), ())), preferred_element_type=F32)


def _split_bf16(x, terms):
    parts = []
    for _ in range(terms):
        p = x.astype(BF16)
        parts.append(p)
        x = x - p.astype(F32)
    return parts


def _dot_split(a, b):
    a_hi, a_lo = _split_bf16(a, 2)
    b_hi, b_lo = _split_bf16(b, 2)
    return _dot(a_hi, b_hi) + (_dot(a_lo, b_hi) + _dot(a_hi, b_lo))


def _rms(x, w):
    return x * lax.rsqrt(jnp.mean(x * x, axis=-1, keepdims=True) + EPS) * w


def _swap_lane_groups(x, half):
    lane = lax.broadcasted_iota(jnp.int32, x.shape, 1)
    up = pltpu.roll(x, LANES - half, 1)
    down = pltpu.roll(x, half, 1)
    return jnp.where((lane & half) == 0, up, down)


def _vmem_limit(nbytes):
    return int(min(VMEM_BYTES - (4 << 20), max(nbytes, 32 << 20)))


def _const_spec(shape):
    nd = len(shape)
    return pl.BlockSpec(shape, lambda *_: (0,) * nd, pipeline_mode=pl.Buffered(1))


def _mod_kernel(c_ref, w_ref, b_ref, o_ref):
    cv = c_ref[...]
    sc = cv * jax.nn.sigmoid(cv)
    o_ref[...] = jnp.dot(sc, w_ref[...], precision=_HI, preferred_element_type=F32) + b_ref[...]


def _modulation(cvec, w_mod, b_mod):
    L, D, N = w_mod.shape
    nb = N // MOD_COLS
    return pl.pallas_call(
        _mod_kernel,
        grid=(L, nb),
        in_specs=[
            pl.BlockSpec((SUBLANES, D), lambda l, j: (0, 0)),
            pl.BlockSpec((None, D, MOD_COLS), lambda l, j: (l, 0, j)),
            pl.BlockSpec((None, 1, MOD_COLS), lambda l, j: (l, 0, j)),
        ],
        out_specs=pl.BlockSpec((None, SUBLANES, MOD_COLS), lambda l, j: (l, 0, j)),
        out_shape=jax.ShapeDtypeStruct((L, SUBLANES, N), F32),
        compiler_params=pltpu.CompilerParams(
            dimension_semantics=("parallel", "parallel"),
            vmem_limit_bytes=_vmem_limit(3 * D * MOD_COLS * 4)),
    )(cvec, w_mod, b_mod.reshape(L, 1, N))


_C_AQ = 0
_C_AK = _C_AQ + A_HEADS * A_HEAD_DIM
_C_AV = _C_AK + A_KV_HEADS * A_HEAD_DIM
_C_BQ = _C_AV + A_KV_HEADS * A_HEAD_DIM
_C_BKV = _C_BQ + B_Q_RANK
_C_BKR = _C_BKV + B_KV_RANK
_C_CQ = _C_BKR + LANES
_C_CK = _C_CQ + C_HEADS * C_KEY_DIM
_C_CV = _C_CK + C_HEADS * C_KEY_DIM
_C_CR = _C_CV + C_HEADS * C_VAL_DIM
_C_CG = _C_CR + C_HEADS * C_VAL_DIM
_C_GATE = _C_CG + LANES
B_HEAD_PAD = LANES
B_V_ROWS = B_V_DIM + 2 * SUBLANES
A_V_ROWS = A_HEAD_DIM + 2 * SUBLANES
LOG2E = 1.4426950408889634


def _inproj_kernel(x_ref, mod_ref, nw_ref, w_ref, rope_ref, bqn_ref, bkvn_ref, wuq_ref, wkn_ref,
                   wvb_ref, wg_ref, bg_ref,
                   qa_ref, ka_ref, va_ref, qb_ref, kb_ref, vb_ref, cq_ref, ck_ref, cv_ref, cr_ref,
                   g_ref, gate_ref, *, d_model):
    mod = mod_ref[...]
    h = _rms(x_ref[...], nw_ref[...]) * (1.0 + mod[1:2]) + mod[0:1]
    hb = h.astype(BF16)

    def proj(lo, hi):
        return _dot(hb, w_ref[:, lo:hi])

    cos_a, sin_a, cos_b, sin_b = rope_ref[0], rope_ref[1], rope_ref[2], rope_ref[3]

    def rope_a(t):
        return t * cos_a + _swap_lane_groups(t, A_HEAD_DIM // 2) * sin_a

    def rope_b(t):
        return t * cos_b + _swap_lane_groups(t, B_ROPE_DIM // 2) * sin_b

    a_scale = A_HEAD_DIM ** -0.5 * LOG2E
    for c in range(A_HEADS * A_HEAD_DIM // LANES):
        t = proj(_C_AQ + c * LANES, _C_AQ + (c + 1) * LANES)
        qa_ref[c * LANES:(c + 1) * LANES, :] = (rope_a(t) * a_scale).T.astype(BF16)
    ka_ref[...] = rope_a(proj(_C_AK, _C_AV)).astype(BF16)
    va_t = proj(_C_AV, _C_BQ).T.astype(BF16)
    a_ones = jnp.where(lax.broadcasted_iota(jnp.int32, (A_V_ROWS - A_HEAD_DIM, va_t.shape[1]), 0) == 0,
                       1.0, 0.0).astype(BF16)
    for g in range(A_KV_HEADS):
        va_ref[g * A_V_ROWS:g * A_V_ROWS + A_HEAD_DIM, :] = va_t[g * A_HEAD_DIM:(g + 1) * A_HEAD_DIM]
        va_ref[g * A_V_ROWS + A_HEAD_DIM:(g + 1) * A_V_ROWS, :] = a_ones

    b_scale = (B_NOPE_DIM + B_ROPE_DIM) ** -0.5 * LOG2E
    cqn = _rms(proj(_C_BQ, _C_BKV), bqn_ref[...]).astype(BF16)
    ckvn = _rms(proj(_C_BKV, _C_BKR), bkvn_ref[...]).astype(BF16)
    kr = rope_b(proj(_C_BKR, _C_CQ))
    for hd in range(B_HEADS):
        sl = slice(hd * B_HEAD_PAD, (hd + 1) * B_HEAD_PAD)
        qb_ref[sl, :] = (rope_b(_dot(cqn, wuq_ref[:, sl])) * b_scale).T.astype(BF16)
        kb_ref[:, sl] = (_dot(ckvn, wkn_ref[:, sl]) + kr).astype(BF16)
    vb = _dot(ckvn, wvb_ref[...])
    tk = vb_ref.shape[2]
    per = LANES // B_V_DIM
    ones_rows = jnp.where(lax.broadcasted_iota(jnp.int32, (B_V_ROWS - B_V_DIM, tk), 0) == 0, 1.0, 0.0).astype(BF16)
    for cc in range(vb_ref.shape[0]):
        for c in range(vb.shape[1] // LANES):
            v_t = vb[cc * tk:(cc + 1) * tk, c * LANES:(c + 1) * LANES].T.astype(BF16)
            for i in range(per):
                r0 = (c * per + i) * B_V_ROWS
                vb_ref[cc, r0:r0 + B_V_DIM, :] = v_t[i * B_V_DIM:(i + 1) * B_V_DIM]
                vb_ref[cc, r0 + B_V_DIM:r0 + B_V_ROWS, :] = ones_rows

    cq_ref[...] = proj(_C_CQ, _C_CK) * (C_KEY_DIM ** -0.5)
    ck_ref[...] = proj(_C_CK, _C_CV)
    cv_ref[...] = proj(_C_CV, _C_CR).astype(BF16)
    cr_ref[...] = proj(_C_CR, _C_CG)
    z = _dot_split(proj(_C_CG, _C_GATE), wg_ref[...]) + bg_ref[...]
    g_ref[...] = (jnp.minimum(z, 0.0) - jnp.log1p(jnp.exp(-jnp.abs(z)))) * (1.0 / C_GATE_TAU)

    gate_ref[...] = proj(_C_GATE, _C_GATE + 3 * d_model)


def _mod_index(b, t):
    return (jnp.where(t == 0, 0, b + 1), 0, 0)


def _inproj(X, mod, nw, w, rope, bqn, bkvn, wuq, wkn, wvb, wg, bg):
    B, T, D = X.shape
    NT = T // TM
    n_in = w.shape[1]
    row = lambda n: pl.BlockSpec((None, TM, n), lambda b, t: (b, t, 0))
    widths = [(A_HEADS * A_HEAD_DIM, BF16), (A_KV_HEADS * A_HEAD_DIM, BF16), (A_KV_HEADS * A_HEAD_DIM, BF16),
              (B_HEADS * B_HEAD_PAD, BF16), (B_HEADS * B_HEAD_PAD, BF16), (B_HEADS * B_V_DIM, BF16),
              (C_HEADS * C_KEY_DIM, F32), (C_HEADS * C_KEY_DIM, F32), (C_HEADS * C_VAL_DIM, BF16),
              (C_HEADS * C_VAL_DIM, F32), (2 * C_HEADS * C_KEY_DIM, F32), (3 * D, F32)]
    out_bytes = sum(TM * n * jnp.dtype(dt).itemsize for n, dt in widths)
    vmem = w.size * 2 + 4 * out_bytes + 6 * TM * D * 4 + (8 << 20)
    out_specs = [row(n) for n, _ in widths]
    out_shape = [jax.ShapeDtypeStruct((B, T, n), dt) for n, dt in widths]
    qw, vw = B_HEADS * B_HEAD_PAD, B_HEADS * B_V_ROWS
    out_specs[3] = pl.BlockSpec((None, qw, TM), lambda b, t: (b, 0, t))
    out_shape[3] = jax.ShapeDtypeStruct((B, qw, T), BF16)
    out_specs[0] = pl.BlockSpec((None, widths[0][0], TM), lambda b, t: (b, 0, t))
    out_shape[0] = jax.ShapeDtypeStruct((B, widths[0][0], T), BF16)
    out_specs[2] = pl.BlockSpec((None, A_KV_HEADS * A_V_ROWS, TM), lambda b, t: (b, 0, t))
    out_shape[2] = jax.ShapeDtypeStruct((B, A_KV_HEADS * A_V_ROWS, T), BF16)
    out_specs[5] = pl.BlockSpec((None, TM // MLA_TK, vw, MLA_TK), lambda b, t: (b, t, 0, 0))
    out_shape[5] = jax.ShapeDtypeStruct((B, T // MLA_TK, vw, MLA_TK), BF16)
    return pl.pallas_call(
        functools.partial(_inproj_kernel, d_model=D),
        grid=(B, NT),
        in_specs=[
            row(D),
            pl.BlockSpec((None, 6, D), _mod_index),
            _const_spec((1, D)),
            _const_spec((D, n_in)),
            pl.BlockSpec((4, TM, LANES), lambda b, t: (0, t, 0)),
            _const_spec(bqn.shape), _const_spec(bkvn.shape), _const_spec(wuq.shape),
            _const_spec(wkn.shape), _const_spec(wvb.shape), _const_spec(wg.shape), _const_spec(bg.shape),
        ],
        out_specs=out_specs,
        out_shape=out_shape,
        compiler_params=pltpu.CompilerParams(
            dimension_semantics=("parallel", "parallel"), vmem_limit_bytes=_vmem_limit(vmem)),
    )(X, mod, nw, w, rope, bqn, bkvn, wuq, wkn, wvb, wg, bg)


def _attn_a_kernel(q_ref, kp_ref, km_ref, kn_ref, kc_ref, vp_ref, vm_ref, vn_ref, vc_ref, sink_ref,
                   o_ref, *, seq):
    t = pl.program_id(1)
    kloc = jnp.concatenate([kp_ref[...], km_ref[...], kn_ref[...]], axis=0)
    vloc = jnp.concatenate([vp_ref[...], vm_ref[...], vn_ref[...]], axis=1)
    kctx, vctx = kc_ref[...], vc_ref[...]
    nloc = kloc.shape[0]
    r = lax.broadcasted_iota(jnp.int32, (nloc, TM), 0)
    c = lax.broadcasted_iota(jnp.int32, (nloc, TM), 1)
    rel = r - A_WINDOW - c
    kpos = (t - 1) * TM - A_WINDOW + r
    valid = (jnp.abs(rel) <= A_WINDOW) & (kpos >= 0) & (kpos < seq) & (t >= 1)
    row_q = lax.broadcasted_iota(jnp.int32, (LANES, TM), 0)
    zero = jnp.zeros((), BF16)
    heads = [(col, hf) for col in range(A_HEADS * A_HEAD_DIM // LANES) for hf in range(2)]

    def scores(col, hf):
        q_t = q_ref[col * LANES:(col + 1) * LANES, :]
        qm = jnp.where((row_q >= hf * A_HEAD_DIM) & (row_q < (hf + 1) * A_HEAD_DIM), q_t, zero)
        return _dot(kloc, qm), _dot(kctx, qm)

    cur = scores(*heads[0])
    halves = []
    for i, (col, hf) in enumerate(heads):
        nxt = scores(*heads[i + 1]) if i + 1 < len(heads) else None
        s_loc = jnp.where(valid, cur[0], _NEG_INF)
        s_ctx = cur[1]
        sink = sink_ref[2 * col + hf:2 * col + hf + 1, 0:1] * LOG2E
        m = jnp.maximum(jnp.maximum(jnp.max(s_loc, axis=0, keepdims=True),
                                    jnp.max(s_ctx, axis=0, keepdims=True)), sink)
        p_loc = jnp.exp2((s_loc - m).astype(BF16))
        p_ctx = jnp.exp2((s_ctx - m).astype(BF16))
        vs = slice(hf * A_V_ROWS, (hf + 1) * A_V_ROWS)
        o_t = _dot(vloc[vs], p_loc) + _dot(vctx[vs], p_ctx)
        denom = o_t[A_HEAD_DIM:A_HEAD_DIM + 1] + jnp.exp2(sink - m)
        halves.append(o_t[0:A_HEAD_DIM] / denom)
        if hf == 1:
            o_ref[:, col * LANES:(col + 1) * LANES] = jnp.concatenate(halves, axis=0).T.astype(BF16)
            halves = []
        cur = nxt


def _attn_a(qa, ka, va, sink, lc):
    B, qw, T = qa.shape
    NT = T // TM
    seq = T - lc
    kvw = ka.shape[2]
    vrows = va.shape[1]
    per_tm = TM // A_WINDOW
    last = T // A_WINDOW - 1
    prev_i = lambda t: jnp.maximum(t * per_tm - 1, 0)
    next_i = lambda t: jnp.minimum((t + 1) * per_tm, last)
    kprev = pl.BlockSpec((None, A_WINDOW, kvw), lambda b, t: (b, prev_i(t), 0))
    kmain = pl.BlockSpec((None, TM, kvw), lambda b, t: (b, t, 0))
    knext = pl.BlockSpec((None, A_WINDOW, kvw), lambda b, t: (b, next_i(t), 0))
    kctx = pl.BlockSpec((None, lc, kvw), lambda b, t: (b, 0, 0))
    vprev = pl.BlockSpec((None, vrows, A_WINDOW), lambda b, t: (b, 0, prev_i(t)))
    vmain = pl.BlockSpec((None, vrows, TM), lambda b, t: (b, 0, t))
    vnext = pl.BlockSpec((None, vrows, A_WINDOW), lambda b, t: (b, 0, next_i(t)))
    vctx = pl.BlockSpec((None, vrows, lc), lambda b, t: (b, 0, 0))
    return pl.pallas_call(
        functools.partial(_attn_a_kernel, seq=seq),
        grid=(B, NT),
        in_specs=[pl.BlockSpec((None, qw, TM), lambda b, t: (b, 0, t)),
                  kprev, kmain, knext, kctx, vprev, vmain, vnext, vctx,
                  pl.BlockSpec(sink.shape, lambda b, t: (0, 0))],
        out_specs=pl.BlockSpec((None, TM, qw), lambda b, t: (b, t, 0)),
        out_shape=jax.ShapeDtypeStruct((B, T, qw), BF16),
        compiler_params=pltpu.CompilerParams(dimension_semantics=("parallel", "parallel")),
    )(qa, ka, ka, ka, ka, va, va, va, va, sink)


def _mla_kernel(q_ref, k_ref, v_ref, o_ref, m_ref, acc_ref, sa_ref, sb_ref, *, lc, tk):
    t = pl.program_id(1)
    total = k_ref.shape[0]
    nchunks = jnp.where(t == 0, lc // tk, total // tk)
    m_ref[...] = jnp.full(m_ref.shape, _NEG_INF, F32)
    acc_ref[...] = jnp.zeros(acc_ref.shape, F32)

    def scores(j, hd):
        qs = slice(hd * B_HEAD_PAD, (hd + 1) * B_HEAD_PAD)
        return _dot(k_ref[pl.ds(pl.multiple_of(j * tk, tk), tk), qs], q_ref[qs, :])

    for hd in range(B_HEADS):
        sa_ref[hd] = scores(0, hd)

    def step(j, cur_ref, nxt_ref):
        nxt = jnp.minimum(j + 1, nchunks - 1)
        for hd in range(B_HEADS):
            vs = slice(hd * B_V_ROWS, (hd + 1) * B_V_ROWS)
            nxt_ref[hd] = scores(nxt, hd)
            s = cur_ref[hd]
            m_prev = m_ref[hd]
            m_new = jnp.maximum(m_prev, jnp.max(s, axis=0, keepdims=True))
            alpha = jnp.exp2(m_prev - m_new)
            p = jnp.exp2((s - m_new[0:1]).astype(BF16))
            m_ref[hd] = m_new
            acc_ref[vs, :] = acc_ref[vs, :] * alpha[0:1] + _dot(v_ref[j, vs, :], p)

    def body(j2, carry):
        step(2 * j2, sa_ref, sb_ref)
        step(2 * j2 + 1, sb_ref, sa_ref)
        return carry

    lax.fori_loop(0, nchunks // 2, body, 0)
    per = LANES // B_V_DIM
    for c in range(B_HEADS // per):
        o_t = jnp.concatenate(
            [acc_ref[hd * B_V_ROWS:hd * B_V_ROWS + B_V_DIM, :]
             / acc_ref[hd * B_V_ROWS + B_V_DIM:hd * B_V_ROWS + B_V_DIM + 1, :]
             for hd in range(c * per, (c + 1) * per)], axis=0)
        o_ref[:, c * LANES:(c + 1) * LANES] = o_t.T.astype(BF16)


def _mla(qb, kb, vb, lc):
    B, qw, T = qb.shape
    NT = T // TM
    _, nck, vw, tk = vb.shape
    ow = B_HEADS * B_V_DIM
    vmem = (T * (qw + vw) * 2 + 4 * TM * qw * 2 + (B_HEADS * SUBLANES * TM + 3 * TM * vw) * 4
            + 2 * B_HEADS * tk * TM * 4 + (16 << 20))
    return pl.pallas_call(
        functools.partial(_mla_kernel, lc=lc, tk=tk),
        grid=(B, NT),
        in_specs=[pl.BlockSpec((None, qw, TM), lambda b, t: (b, 0, t)),
                  pl.BlockSpec((None, T, qw), lambda b, t: (b, 0, 0), pipeline_mode=pl.Buffered(1)),
                  pl.BlockSpec((None, nck, vw, tk), lambda b, t: (b, 0, 0, 0), pipeline_mode=pl.Buffered(1))],
        out_specs=pl.BlockSpec((None, TM, ow), lambda b, t: (b, t, 0)),
        out_shape=jax.ShapeDtypeStruct((B, T, ow), BF16),
        scratch_shapes=[pltpu.VMEM((B_HEADS, SUBLANES, TM), F32),
                        pltpu.VMEM((vw, TM), F32),
                        pltpu.VMEM((B_HEADS, tk, TM), F32), pltpu.VMEM((B_HEADS, tk, TM), F32)],
        compiler_params=pltpu.CompilerParams(
            dimension_semantics=("parallel", "arbitrary"), vmem_limit_bytes=_vmem_limit(vmem)),
    )(qb, kb, vb)


def _gla_prep(q, k, b, reverse):
    C = q.shape[0]
    dk2 = q.shape[1]
    nsub = C // GLA_SUB
    b_end = b[0:1] if reverse else b[C - 1:C]
    row = lax.broadcasted_iota(jnp.int32, (C, dk2), 0)
    lane = lax.broadcasted_iota(jnp.int32, (C, dk2), 1)

    qe = (q * jnp.exp(b)).astype(BF16)
    kd_t = (k * jnp.exp(b_end - b)).T.astype(BF16)
    dec = jnp.exp(jnp.broadcast_to(b_end, (dk2, dk2)).T)
    dec2 = jnp.concatenate([dec] * (2 * C_VAL_DIM // dk2), axis=1)

    refs = [blk * GLA_SUB + (GLA_SUB - 1 if reverse else 0) for blk in range(nsub)]
    b_ref = jnp.concatenate([jnp.broadcast_to(b[n:n + 1], (GLA_SUB, dk2)) for n in refs], axis=0)
    q_sc = q * jnp.exp(b - b_ref)
    k_parts = []
    for blk, n in enumerate(refs):
        in_range = (row >= blk * GLA_SUB) if reverse else (row < (blk + 1) * GLA_SUB)
        k_parts.append(jnp.where(in_range, k * jnp.exp(b[n:n + 1] - b), 0.0))
    k_big = jnp.concatenate(k_parts, axis=1).astype(BF16)
    q_halves = []
    for hf in range(2):
        in_half = (lane >= hf * C_KEY_DIM) & (lane < (hf + 1) * C_KEY_DIM)
        q_halves.append(jnp.concatenate(
            [jnp.where(in_half & (row >= blk * GLA_SUB) & (row < (blk + 1) * GLA_SUB), q_sc, 0.0)
             for blk in range(nsub)], axis=1))
    q_big = jnp.concatenate(q_halves, axis=0).astype(BF16)
    return qe, kd_t, dec2, q_big, k_big


def _gla_kernel(qf_ref, kf_ref, vf_ref, gf_ref, qr_ref, kr_ref, vr_ref, gr_ref, of_ref, or_ref,
                sf_ref, sr_ref):
    t = pl.program_id(1)

    @pl.when(t == 0)
    def _():
        sf_ref[...] = jnp.zeros(sf_ref.shape, F32)
        sr_ref[...] = jnp.zeros(sr_ref.shape, F32)

    C = C_CHUNK
    nchunk = TM // C
    npair = C_HEADS // 2
    kw, vw = 2 * C_KEY_DIM, 2 * C_VAL_DIM
    dirs = ((qf_ref, kf_ref, vf_ref, gf_ref, of_ref, sf_ref, False),
            (qr_ref, kr_ref, vr_ref, gr_ref, or_ref, sr_ref, True))

    ii = lax.broadcasted_iota(jnp.int32, (TM, TM), 0)
    jj = lax.broadcasted_iota(jnp.int32, (TM, TM), 1)
    same_chunk = (ii // C) == (jj // C)
    b_all = []
    for (_, _, _, g_ref, _, _, reverse) in dirs:
        tri = jnp.where(same_chunk & ((jj >= ii) if reverse else (jj <= ii)), 1.0, 0.0).astype(BF16)
        g_hi, g_mid, g_lo = _split_bf16(g_ref[...], 3)
        b_all.append(_dot(tri, g_hi) + (_dot(tri, g_mid) + _dot(tri, g_lo)))

    units = []
    for d, (q_ref, k_ref, v_ref, _, o_ref, _, reverse) in enumerate(dirs):
        for p in range(npair):
            ks, vsl = slice(p * kw, (p + 1) * kw), slice(p * vw, (p + 1) * vw)
            for c in range(nchunk):
                rows = slice(c * C, (c + 1) * C)
                prep = _gla_prep(q_ref[rows, ks], k_ref[rows, ks], b_all[d][rows, ks], reverse)
                units.append((d, p, c, rows, vsl, v_ref, o_ref, reverse, prep))

    srow = lax.broadcasted_iota(jnp.int32, (kw, vw), 0)
    scol = lax.broadcasted_iota(jnp.int32, (kw, vw), 1)
    on_diag = (srow < C_KEY_DIM) == (scol < C_VAL_DIM)
    qi = lax.broadcasted_iota(jnp.int32, (2 * C, C), 0) % C
    kj = lax.broadcasted_iota(jnp.int32, (2 * C, C), 1)
    kvs, atts = [], []
    for (d, p, c, rows, vsl, v_ref, o_ref, reverse, prep) in units:
        _, kd_t, _, q_big, k_big = prep
        kvs.append(jnp.where(on_diag, _dot(kd_t, v_ref[rows, vsl]), 0.0))
        keep = (kj >= qi) if reverse else (kj <= qi)
        atts.append(jnp.where(keep, _dot_nt(q_big, k_big), 0.0).astype(BF16))

    intra = {}
    for u, (d, p, c, rows, vsl, v_ref, o_ref, reverse, prep) in enumerate(units):
        pv = _dot(atts[u], v_ref[rows, vsl])
        intra[(d, p, c)] = (u, jnp.concatenate([pv[0:C, 0:C_VAL_DIM], pv[C:2 * C, C_VAL_DIM:]], axis=1))

    states = {(d, p): dirs[d][5][p] for d in range(len(dirs)) for p in range(npair)}
    for step in range(nchunk):
        for d in range(len(dirs)):
            reverse = dirs[d][6]
            c = nchunk - 1 - step if reverse else step
            for p in range(npair):
                u, o_intra = intra[(d, p, c)]
                _, _, _, rows, vsl, _, o_ref, _, prep = units[u]
                qe, _, dec2, _, _ = prep
                o_ref[rows, vsl] = o_intra + _dot(qe, states[(d, p)].astype(BF16))
                states[(d, p)] = dec2 * states[(d, p)] + kvs[u]
    for d in range(len(dirs)):
        for p in range(npair):
            dirs[d][5][p] = states[(d, p)]


def _gla(cq, ck, cv, g):
    B, T, kw = cq.shape
    NT = T // TM
    vw = cv.shape[2]

    def fwd(n, col=0):
        return pl.BlockSpec((None, TM, n), lambda b, t: (b, t, col))

    def rev(n, col=0):
        return pl.BlockSpec((None, TM, n), lambda b, t: (b, jnp.where(t == 0, 0, NT - t), col))

    return pl.pallas_call(
        _gla_kernel,
        grid=(B, NT),
        in_specs=[fwd(kw), fwd(kw), fwd(vw), fwd(kw, 0), rev(kw), rev(kw), rev(vw), rev(kw, 1)],
        out_specs=[fwd(vw), rev(vw)],
        out_shape=[jax.ShapeDtypeStruct((B, T, vw), F32)] * 2,
        scratch_shapes=[pltpu.VMEM((C_HEADS // 2, 2 * C_KEY_DIM, 2 * C_VAL_DIM), F32)] * 2,
        compiler_params=pltpu.CompilerParams(dimension_semantics=("parallel", "arbitrary")),
    )(cq, ck, cv, g, cq, ck, cv, g)


def _merge_kernel(x_ref, mod_ref, ya_ref, yb_ref, of_ref, or_ref, cr_ref, gate_ref, hn_ref,
                  wa_ref, wb_ref, wc_ref, wo_ref, o_ref):
    d = x_ref.shape[1]
    o = of_ref[...] + or_ref[...]
    r = cr_ref[...]
    parts = []
    for hd in range(C_HEADS):
        sl = slice(hd * C_VAL_DIM, (hd + 1) * C_VAL_DIM)
        parts.append(_rms(o[:, sl], hn_ref[:, sl]))
    yc = (jnp.concatenate(parts, axis=1) * (r * jax.nn.sigmoid(r))).astype(BF16)
    gates = jax.nn.sigmoid(gate_ref[...])
    m = (gates[:, 0:d] * _dot(ya_ref[...], wa_ref[...])
         + gates[:, d:2 * d] * _dot(yb_ref[...], wb_ref[...])
         + gates[:, 2 * d:3 * d] * _dot(yc, wc_ref[...]))
    y = _dot(m.astype(BF16), wo_ref[...])
    o_ref[...] = x_ref[...] + mod_ref[2:3, :] * y


def _merge(X, mod, ya, yb, of, orv, cr, gates, hn, wa, wb, wc, wo):
    B, T, D = X.shape
    NT = T // TM
    row = lambda n: pl.BlockSpec((None, TM, n), lambda b, t: (b, t, 0))
    vmem = (wa.size + wb.size + wc.size + wo.size) * 2 + 2 * TM * (2 * D + 3 * D + 4 * ya.shape[2]) * 4 + (16 << 20)
    return pl.pallas_call(
        _merge_kernel,
        grid=(B, NT),
        in_specs=[row(D), pl.BlockSpec((None, 6, D), _mod_index),
                  row(ya.shape[2]), row(yb.shape[2]), row(of.shape[2]), row(orv.shape[2]),
                  row(cr.shape[2]), row(gates.shape[2]), _const_spec(hn.shape),
                  _const_spec(wa.shape), _const_spec(wb.shape), _const_spec(wc.shape), _const_spec(wo.shape)],
        out_specs=row(D),
        out_shape=jax.ShapeDtypeStruct(X.shape, F32),
        compiler_params=pltpu.CompilerParams(
            dimension_semantics=("parallel", "parallel"), vmem_limit_bytes=_vmem_limit(vmem)),
    )(X, mod, ya, yb, of, orv, cr, gates, hn, wa, wb, wc, wo)


def _ffn_kernel(x_ref, xp_ref, xn_ref, mod_ref, nw_ref, wup_ref, cw_ref, cb_ref, wdn_ref, o_ref,
                h_ref, ug_ref, uv_ref, *, nt, ncb):
    t = pl.program_id(1)
    mod = mod_ref[...]
    nw = nw_ref[...]
    halo = xp_ref.shape[0]
    ffn = wdn_ref.shape[0]
    cb = ffn // ncb

    def norm(xv):
        return _rms(xv, nw) * (1.0 + mod[4:5]) + mod[3:4]

    has_prev = (t >= 2)
    has_next = (t >= 1) & (t < nt - 1)
    h_ref[0:halo, :] = jnp.where(has_prev, norm(xp_ref[...]), 0.0).astype(BF16)
    h_ref[halo:halo + TM, :] = norm(x_ref[...]).astype(BF16)
    h_ref[halo + TM:2 * halo + TM, :] = jnp.where(has_next, norm(xn_ref[...]), 0.0).astype(BF16)
    hb = h_ref[...]

    acc = jnp.zeros((TM, x_ref.shape[1]), F32)
    for j in range(ncb):
        ug_ref[...] = _dot(hb, wup_ref[:, j * cb:(j + 1) * cb])
        uv_ref[...] = _dot(hb, wup_ref[:, ffn + j * cb:ffn + (j + 1) * cb])

        def conv(u_ref, lo):
            w = cw_ref[:, lo:lo + cb]
            return (u_ref[halo - 1:halo - 1 + TM, :] * w[0:1] + u_ref[halo:halo + TM, :] * w[1:2]
                    + u_ref[halo + 1:halo + 1 + TM, :] * w[2:3] + cb_ref[:, lo:lo + cb])

        gt = conv(ug_ref, j * cb)
        val = conv(uv_ref, ffn + j * cb)
        act = (gt * jax.nn.sigmoid(gt) * val).astype(BF16)
        acc = acc + _dot(act, wdn_ref[j * cb:(j + 1) * cb, :])
    o_ref[...] = x_ref[...] + mod[5:6] * acc


def _ffn(X, mod, nw, wup, cw, cb, wdn, ncb=2):
    B, T, D = X.shape
    NT = T // TM
    halo = 2 * SUBLANES
    per = TM // halo
    last = T // halo - 1
    ffn = wdn.shape[0]
    cbw = ffn // ncb
    vmem = (wup.size + wdn.size) * 2 + 2 * (TM + 2 * halo) * cbw * 4 + 8 * TM * D * 4 + (16 << 20)
    return pl.pallas_call(
        functools.partial(_ffn_kernel, nt=NT, ncb=ncb),
        grid=(B, NT),
        in_specs=[pl.BlockSpec((None, TM, D), lambda b, t: (b, t, 0)),
                  pl.BlockSpec((None, halo, D), lambda b, t: (b, jnp.maximum(t * per - 1, 0), 0)),
                  pl.BlockSpec((None, halo, D), lambda b, t: (b, jnp.minimum((t + 1) * per, last), 0)),
                  pl.BlockSpec((None, 6, D), _mod_index),
                  _const_spec(nw.shape), _const_spec(wup.shape), _const_spec(cw.shape), _const_spec(cb.shape),
                  _const_spec(wdn.shape)],
        out_specs=pl.BlockSpec((None, TM, D), lambda b, t: (b, t, 0)),
        out_shape=jax.ShapeDtypeStruct(X.shape, F32),
        scratch_shapes=[pltpu.VMEM((TM + 2 * halo, D), BF16), pltpu.VMEM((TM + 2 * halo, cbw), F32),
                        pltpu.VMEM((TM + 2 * halo, cbw), F32)],
        compiler_params=pltpu.CompilerParams(
            dimension_semantics=("parallel", "parallel"), vmem_limit_bytes=_vmem_limit(vmem)),
    )(X, X, X, mod, nw, wup, cw, cb, wdn)


def _final_kernel(x_ref, w_ref, o_ref):
    o_ref[...] = _rms(x_ref[...], w_ref[...])


def _final_norm(X, w, lc):
    B, T, D = X.shape
    S = T - lc
    skip = lc // TM
    return pl.pallas_call(
        _final_kernel,
        grid=(B, S // TM),
        in_specs=[pl.BlockSpec((None, TM, D), lambda b, t: (b, t + skip, 0)), _const_spec(w.shape)],
        out_specs=pl.BlockSpec((None, TM, D), lambda b, t: (b, t, 0)),
        out_shape=jax.ShapeDtypeStruct((B, S, D), F32),
        compiler_params=pltpu.CompilerParams(dimension_semantics=("parallel", "parallel")),
    )(X, w)


def _rope_tables(seq, lc):
    rows = seq // GRID_W
    row = jnp.broadcast_to(jnp.arange(rows)[:, None], (rows, GRID_W)).reshape(-1).astype(F32)
    col = jnp.broadcast_to(jnp.arange(GRID_W)[None, :], (rows, GRID_W)).reshape(-1).astype(F32)

    def cs(rot_dim):
        n_freq = rot_dim // 4
        inv = ROPE_BASE ** (-jnp.arange(n_freq, dtype=F32) / n_freq)
        ang = jnp.concatenate([row[:, None] * inv, col[:, None] * inv], axis=-1)
        return jnp.cos(ang), jnp.sin(ang)

    ca, sa = cs(A_HEAD_DIM)
    cb, sb = cs(B_ROPE_DIM)
    reps = LANES // A_HEAD_DIM
    cos_a = jnp.tile(jnp.concatenate([ca, ca], axis=1), (1, reps))
    sin_a = jnp.tile(jnp.concatenate([-sa, sa], axis=1), (1, reps))
    one = jnp.ones((seq, B_NOPE_DIM), F32)
    pad = LANES - B_NOPE_DIM - B_ROPE_DIM
    cos_b = jnp.concatenate([one, cb, cb, jnp.ones((seq, pad), F32)], axis=1)
    sin_b = jnp.concatenate([0 * one, -sb, sb, jnp.zeros((seq, pad), F32)], axis=1)
    ident = jnp.stack([jnp.ones((lc, LANES), F32), jnp.zeros((lc, LANES), F32)] * 2)
    return jnp.concatenate([ident, jnp.stack([cos_a, sin_a, cos_b, sin_b])], axis=1)


def _a_head_perm():
    g = A_HEADS // A_KV_HEADS
    return [h for c in range(g) for h in (c, c + g)]


def _prep_w_in(w_in):
    L, D, _ = w_in.shape
    sizes = [A_HEADS * A_HEAD_DIM, A_KV_HEADS * A_HEAD_DIM, A_KV_HEADS * A_HEAD_DIM, B_Q_RANK, B_KV_RANK,
             B_ROPE_DIM, C_HEADS * C_KEY_DIM, C_HEADS * C_KEY_DIM, C_HEADS * C_VAL_DIM, C_HEADS * C_VAL_DIM,
             2 * C_GATE_RANK, 3 * D]
    offs = [0]
    for s in sizes:
        offs.append(offs[-1] + s)
    aq, ak, av, bq, bkv, bkr, cq, ck, cv, cr, cg, gate = [w_in[:, :, offs[i]:offs[i + 1]] for i in range(len(sizes))]
    aq = aq.reshape(L, D, A_HEADS, A_HEAD_DIM)[:, :, jnp.array(_a_head_perm())].reshape(L, D, -1)
    bkr = jnp.pad(bkr, ((0, 0), (0, 0), (B_NOPE_DIM, LANES - B_NOPE_DIM - B_ROPE_DIM)))
    cg = jnp.pad(cg, ((0, 0), (0, 0), (0, LANES - 2 * C_GATE_RANK)))
    return jnp.concatenate([aq, ak, av, bq, bkv, bkr, cq, ck, cv, cr, cg, gate], axis=-1).astype(BF16)


def kernel(x, c, ctx, c_ctx, w_mod, b_mod, norm_mix, norm_ffn, w_in, a_sink, b_q_norm, b_kv_norm, b_w_uq, b_w_ukv, c_w_gate, c_b_gate, c_head_norm, w_br_a, w_br_b, w_br_c, w_out, w_up, conv_w, conv_b, w_down, final_norm):
    B, S, D = x.shape
    lc = ctx.shape[1]
    L = w_mod.shape[0]
    assert lc == TM and S % TM == 0 and S % GRID_W == 0 and B + 1 <= SUBLANES

    X = jnp.concatenate([ctx, x], axis=1)
    cvec = jnp.zeros((SUBLANES, D), F32).at[0].set(c_ctx).at[1:B + 1].set(c)
    mods = _modulation(cvec, w_mod, b_mod).reshape(L, SUBLANES, 6, D)

    rope = _rope_tables(S, lc)
    perm = jnp.array(_a_head_perm())
    w_in_r = _prep_w_in(w_in)
    sink = jnp.broadcast_to(a_sink[:, perm][:, :, None], (L, A_HEADS, LANES)).astype(F32)
    wa = w_br_a.reshape(L, A_HEADS, A_HEAD_DIM, D)[:, perm].reshape(L, A_HEADS * A_HEAD_DIM, D).astype(BF16)
    wb, wc, wo = w_br_b.astype(BF16), w_br_c.astype(BF16), w_out.astype(BF16)
    qk = B_NOPE_DIM + B_ROPE_DIM
    wuq = jnp.pad(b_w_uq.reshape(L, B_Q_RANK, B_HEADS, qk),
                  ((0, 0), (0, 0), (0, 0), (0, B_HEAD_PAD - qk))).reshape(L, B_Q_RANK, -1).astype(BF16)
    ukv = b_w_ukv.reshape(L, B_KV_RANK, B_HEADS, B_NOPE_DIM + B_V_DIM)
    wkn = jnp.pad(ukv[..., :B_NOPE_DIM],
                  ((0, 0), (0, 0), (0, 0), (0, B_HEAD_PAD - B_NOPE_DIM))).reshape(L, B_KV_RANK, -1).astype(BF16)
    wvb = ukv[..., B_NOPE_DIM:].reshape(L, B_KV_RANK, -1).astype(BF16)
    nqk = C_HEADS * C_KEY_DIM
    wg = jnp.zeros((L, LANES, 2 * nqk), F32)
    wg = wg.at[:, 0:C_GATE_RANK, 0:nqk].set(c_w_gate[:, 0]).at[:, C_GATE_RANK:2 * C_GATE_RANK, nqk:].set(c_w_gate[:, 1])
    bg = c_b_gate.reshape(L, 1, 2 * nqk)
    wup, wdn = w_up.astype(BF16), w_down.astype(BF16)

    for l in range(L):
        mod = mods[l]
        (qa, ka, va, qb, kb, vb, cq, ck, cv, cr, g, gates) = _inproj(
            X, mod, norm_mix[l][None], w_in_r[l], rope, b_q_norm[l][None], b_kv_norm[l][None],
            wuq[l], wkn[l], wvb[l], wg[l], bg[l])
        ya = _attn_a(qa, ka, va, sink[l], lc)
        yb = _mla(qb, kb, vb, lc)
        of, orv = _gla(cq, ck, cv, g)
        X = _merge(X, mod, ya, yb, of, orv, cr, gates, c_head_norm[l][None], wa[l], wb[l], wc[l], wo[l])
        X = _ffn(X, mod, norm_ffn[l][None], wup[l], conv_w[l], conv_b[l][None], wdn[l])
    return _final_norm(X, final_norm[None], lc)
```

```python
import functools

import jax
import jax.numpy as jnp
from jax import lax
from jax.experimental import pallas as pl
from jax.experimental.pallas import tpu as pltpu

GRID_W = 64
EPS = 1e-6
ROPE_BASE = 10000.0
A_HEADS, A_KV_HEADS, A_HEAD_DIM, A_WINDOW = 8, 2, 64, 128
B_HEADS, B_Q_RANK, B_KV_RANK, B_NOPE_DIM, B_ROPE_DIM, B_V_DIM = 8, 256, 128, 64, 32, 64
C_HEADS, C_KEY_DIM, C_VAL_DIM, C_GATE_RANK, C_GATE_TAU, C_CHUNK = 4, 64, 128, 16, 16.0, 64
CONV_W = 3

LANES = 128
SUBLANES = 8
VMEM_BYTES = 64 * 1024 * 1024

TM = 256
GLA_SUB = 16
MLA_TK = 256
MOD_COLS = 1536
F32 = jnp.float32
BF16 = jnp.bfloat16
_HI = lax.Precision.HIGHEST
_NEG_INF = float("-inf")


def _dot(a, b):
    return jnp.dot(a, b, preferred_element_type=F32)


def _dot_nt(a, b):
    return lax.dot_general(a, b, (((1,), (1,)), ((), ())), preferred_element_type=F32)


def _split_bf16(x, terms):
    parts = []
    for _ in range(terms):
        p = x.astype(BF16)
        parts.append(p)
        x = x - p.astype(F32)
    return parts


def _dot_split(a, b):
    a_hi, a_lo = _split_bf16(a, 2)
    b_hi, b_lo = _split_bf16(b, 2)
    return _dot(a_hi, b_hi) + (_dot(a_lo, b_hi) + _dot(a_hi, b_lo))


def _rms(x, w):
    return x * lax.rsqrt(jnp.mean(x * x, axis=-1, keepdims=True) + EPS) * w


def _swap_lane_groups(x, half):
    lane = lax.broadcasted_iota(jnp.int32, x.shape, 1)
    up = pltpu.roll(x, LANES - half, 1)
    down = pltpu.roll(x, half, 1)
    return jnp.where((lane & half) == 0, up, down)


def _vmem_limit(nbytes):
    return int(min(VMEM_BYTES - (4 << 20), max(nbytes, 32 << 20)))


def _const_spec(shape):
    nd = len(shape)
    return pl.BlockSpec(shape, lambda *_: (0,) * nd, pipeline_mode=pl.Buffered(1))


def _mod_kernel(c_ref, w_ref, b_ref, o_ref):
    cv = c_ref[...]
    sc = cv * jax.nn.sigmoid(cv)
    o_ref[...] = jnp.dot(sc, w_ref[...], precision=_HI, preferred_element_type=F32) + b_ref[...]


def _modulation(cvec, w_mod, b_mod):
    L, D, N = w_mod.shape
    nb = N // MOD_COLS
    return pl.pallas_call(
        _mod_kernel,
        grid=(L, nb),
        in_specs=[
            pl.BlockSpec((SUBLANES, D), lambda l, j: (0, 0)),
            pl.BlockSpec((None, D, MOD_COLS), lambda l, j: (l, 0, j)),
            pl.BlockSpec((None, 1, MOD_COLS), lambda l, j: (l, 0, j)),
        ],
        out_specs=pl.BlockSpec((None, SUBLANES, MOD_COLS), lambda l, j: (l, 0, j)),
        out_shape=jax.ShapeDtypeStruct((L, SUBLANES, N), F32),
        compiler_params=pltpu.CompilerParams(
            dimension_semantics=("parallel", "parallel"),
            vmem_limit_bytes=_vmem_limit(3 * D * MOD_COLS * 4)),
    )(cvec, w_mod, b_mod.reshape(L, 1, N))


_C_AQ = 0
_C_AK = _C_AQ + A_HEADS * A_HEAD_DIM
_C_AV = _C_AK + A_KV_HEADS * A_HEAD_DIM
_C_BQ = _C_AV + A_KV_HEADS * A_HEAD_DIM
_C_BKV = _C_BQ + B_Q_RANK
_C_BKR = _C_BKV + B_KV_RANK
_C_CQ = _C_BKR + LANES
_C_CK = _C_CQ + C_HEADS * C_KEY_DIM
_C_CV = _C_CK + C_HEADS * C_KEY_DIM
_C_CR = _C_CV + C_HEADS * C_VAL_DIM
_C_CG = _C_CR + C_HEADS * C_VAL_DIM
_C_GATE = _C_CG + LANES
B_HEAD_PAD = LANES
B_V_ROWS = B_V_DIM + 2 * SUBLANES
A_V_ROWS = A_HEAD_DIM + 2 * SUBLANES
LOG2E = 1.4426950408889634


def _inproj_kernel(x_ref, mod_ref, nw_ref, w_ref, rope_ref, bqn_ref, bkvn_ref, wuq_ref, wkn_ref,
                   wvb_ref, wg_ref, bg_ref,
                   qa_ref, ka_ref, va_ref, qb_ref, kb_ref, vb_ref, cq_ref, ck_ref, cv_ref, cr_ref,
                   g_ref, gate_ref, *, d_model):
    mod = mod_ref[...]
    h = _rms(x_ref[...], nw_ref[...]) * (1.0 + mod[1:2]) + mod[0:1]
    hb = h.astype(BF16)

    def proj(lo, hi):
        return _dot(hb, w_ref[:, lo:hi])

    cos_a, sin_a, cos_b, sin_b = rope_ref[0], rope_ref[1], rope_ref[2], rope_ref[3]

    def rope_a(t):
        return t * cos_a + _swap_lane_groups(t, A_HEAD_DIM // 2) * sin_a

    def rope_b(t):
        return t * cos_b + _swap_lane_groups(t, B_ROPE_DIM // 2) * sin_b

    a_scale = A_HEAD_DIM ** -0.5 * LOG2E
    for c in range(A_HEADS * A_HEAD_DIM // LANES):
        t = proj(_C_AQ + c * LANES, _C_AQ + (c + 1) * LANES)
        qa_ref[c * LANES:(c + 1) * LANES, :] = (rope_a(t) * a_scale).T.astype(BF16)
    ka_ref[...] = rope_a(proj(_C_AK, _C_AV)).astype(BF16)
    va_t = proj(_C_AV, _C_BQ).T.astype(BF16)
    a_ones = jnp.where(lax.broadcasted_iota(jnp.int32, (A_V_ROWS - A_HEAD_DIM, va_t.shape[1]), 0) == 0,
                       1.0, 0.0).astype(BF16)
    for g in range(A_KV_HEADS):
        va_ref[g * A_V_ROWS:g * A_V_ROWS + A_HEAD_DIM, :] = va_t[g * A_HEAD_DIM:(g + 1) * A_HEAD_DIM]
        va_ref[g * A_V_ROWS + A_HEAD_DIM:(g + 1) * A_V_ROWS, :] = a_ones

    b_scale = (B_NOPE_DIM + B_ROPE_DIM) ** -0.5 * LOG2E
    cqn = _rms(proj(_C_BQ, _C_BKV), bqn_ref[...]).astype(BF16)
    ckvn = _rms(proj(_C_BKV, _C_BKR), bkvn_ref[...]).astype(BF16)
    kr = rope_b(proj(_C_BKR, _C_CQ))
    for hd in range(B_HEADS):
        sl = slice(hd * B_HEAD_PAD, (hd + 1) * B_HEAD_PAD)
        qb_ref[sl, :] = (rope_b(_dot(cqn, wuq_ref[:, sl])) * b_scale).T.astype(BF16)
        kb_ref[:, sl] = (_dot(ckvn, wkn_ref[:, sl]) + kr).astype(BF16)
    vb = _dot(ckvn, wvb_ref[...])
    tk = vb_ref.shape[2]
    per = LANES // B_V_DIM
    ones_rows = jnp.where(lax.broadcasted_iota(jnp.int32, (B_V_ROWS - B_V_DIM, tk), 0) == 0, 1.0, 0.0).astype(BF16)
    for cc in range(vb_ref.shape[0]):
        for c in range(vb.shape[1] // LANES):
            v_t = vb[cc * tk:(cc + 1) * tk, c * LANES:(c + 1) * LANES].T.astype(BF16)
            for i in range(per):
                r0 = (c * per + i) * B_V_ROWS
                vb_ref[cc, r0:r0 + B_V_DIM, :] = v_t[i * B_V_DIM:(i + 1) * B_V_DIM]
                vb_ref[cc, r0 + B_V_DIM:r0 + B_V_ROWS, :] = ones_rows

    cq_ref[...] = proj(_C_CQ, _C_CK) * (C_KEY_DIM ** -0.5)
    ck_ref[...] = proj(_C_CK, _C_CV)
    cv_ref[...] = proj(_C_CV, _C_CR).astype(BF16)
    cr_ref[...] = proj(_C_CR, _C_CG)
    z = _dot_split(proj(_C_CG, _C_GATE), wg_ref[...]) + bg_ref[...]
    g_ref[...] = (jnp.minimum(z, 0.0) - jnp.log1p(jnp.exp(-jnp.abs(z)))) * (1.0 / C_GATE_TAU)

    gate_ref[...] = proj(_C_GATE, _C_GATE + 3 * d_model)


def _mod_index(b, t):
    return (jnp.where(t == 0, 0, b + 1), 0, 0)


def _inproj(X, mod, nw, w, rope, bqn, bkvn, wuq, wkn, wvb, wg, bg):
    B, T, D = X.shape
    NT = T // TM
    n_in = w.shape[1]
    row = lambda n: pl.BlockSpec((None, TM, n), lambda b, t: (b, t, 0))
    widths = [(A_HEADS * A_HEAD_DIM, BF16), (A_KV_HEADS * A_HEAD_DIM, BF16), (A_KV_HEADS * A_HEAD_DIM, BF16),
              (B_HEADS * B_HEAD_PAD, BF16), (B_HEADS * B_HEAD_PAD, BF16), (B_HEADS * B_V_DIM, BF16),
              (C_HEADS * C_KEY_DIM, F32), (C_HEADS * C_KEY_DIM, F32), (C_HEADS * C_VAL_DIM, BF16),
              (C_HEADS * C_VAL_DIM, F32), (2 * C_HEADS * C_KEY_DIM, F32), (3 * D, F32)]
    out_bytes = sum(TM * n * jnp.dtype(dt).itemsize for n, dt in widths)
    vmem = w.size * 2 + 4 * out_bytes + 6 * TM * D * 4 + (8 << 20)
    out_specs = [row(n) for n, _ in widths]
    out_shape = [jax.ShapeDtypeStruct((B, T, n), dt) for n, dt in widths]
    qw, vw = B_HEADS * B_HEAD_PAD, B_HEADS * B_V_ROWS
    out_specs[3] = pl.BlockSpec((None, qw, TM), lambda b, t: (b, 0, t))
    out_shape[3] = jax.ShapeDtypeStruct((B, qw, T), BF16)
    out_specs[0] = pl.BlockSpec((None, widths[0][0], TM), lambda b, t: (b, 0, t))
    out_shape[0] = jax.ShapeDtypeStruct((B, widths[0][0], T), BF16)
    out_specs[2] = pl.BlockSpec((None, A_KV_HEADS * A_V_ROWS, TM), lambda b, t: (b, 0, t))
    out_shape[2] = jax.ShapeDtypeStruct((B, A_KV_HEADS * A_V_ROWS, T), BF16)
    out_specs[5] = pl.BlockSpec((None, TM // MLA_TK, vw, MLA_TK), lambda b, t: (b, t, 0, 0))
    out_shape[5] = jax.ShapeDtypeStruct((B, T // MLA_TK, vw, MLA_TK), BF16)
    return pl.pallas_call(
        functools.partial(_inproj_kernel, d_model=D),
        grid=(B, NT),
        in_specs=[
            row(D),
            pl.BlockSpec((None, 6, D), _mod_index),
            _const_spec((1, D)),
            _const_spec((D, n_in)),
            pl.BlockSpec((4, TM, LANES), lambda b, t: (0, t, 0)),
            _const_spec(bqn.shape), _const_spec(bkvn.shape), _const_spec(wuq.shape),
            _const_spec(wkn.shape), _const_spec(wvb.shape), _const_spec(wg.shape), _const_spec(bg.shape),
        ],
        out_specs=out_specs,
        out_shape=out_shape,
        compiler_params=pltpu.CompilerParams(
            dimension_semantics=("parallel", "parallel"), vmem_limit_bytes=_vmem_limit(vmem)),
    )(X, mod, nw, w, rope, bqn, bkvn, wuq, wkn, wvb, wg, bg)


def _attn_a_kernel(q_ref, kp_ref, km_ref, kn_ref, kc_ref, vp_ref, vm_ref, vn_ref, vc_ref, sink_ref,
                   o_ref, *, seq):
    t = pl.program_id(1)
    kloc = jnp.concatenate([kp_ref[...], km_ref[...], kn_ref[...]], axis=0)
    vloc = jnp.concatenate([vp_ref[...], vm_ref[...], vn_ref[...]], axis=1)
    kctx, vctx = kc_ref[...], vc_ref[...]
    nloc = kloc.shape[0]
    r = lax.broadcasted_iota(jnp.int32, (nloc, TM), 0)
    c = lax.broadcasted_iota(jnp.int32, (nloc, TM), 1)
    rel = r - A_WINDOW - c
    kpos = (t - 1) * TM - A_WINDOW + r
    valid = (jnp.abs(rel) <= A_WINDOW) & (kpos >= 0) & (kpos < seq) & (t >= 1)
    row_q = lax.broadcasted_iota(jnp.int32, (LANES, TM), 0)
    zero = jnp.zeros((), BF16)
    heads = [(col, hf) for col in range(A_HEADS * A_HEAD_DIM // LANES) for hf in range(2)]

    def scores(col, hf):
        q_t = q_ref[col * LANES:(col + 1) * LANES, :]
        qm = jnp.where((row_q >= hf * A_HEAD_DIM) & (row_q < (hf + 1) * A_HEAD_DIM), q_t, zero)
        return _dot(kloc, qm), _dot(kctx, qm)

    cur = scores(*heads[0])
    halves = []
    for i, (col, hf) in enumerate(heads):
        nxt = scores(*heads[i + 1]) if i + 1 < len(heads) else None
        s_loc = jnp.where(valid, cur[0], _NEG_INF)
        s_ctx = cur[1]
        sink = sink_ref[2 * col + hf:2 * col + hf + 1, 0:1] * LOG2E
        m = jnp.maximum(jnp.maximum(jnp.max(s_loc, axis=0, keepdims=True),
                                    jnp.max(s_ctx, axis=0, keepdims=True)), sink)
        p_loc = jnp.exp2((s_loc - m).astype(BF16))
        p_ctx = jnp.exp2((s_ctx - m).astype(BF16))
        vs = slice(hf * A_V_ROWS, (hf + 1) * A_V_ROWS)
        o_t = _dot(vloc[vs], p_loc) + _dot(vctx[vs], p_ctx)
        denom = o_t[A_HEAD_DIM:A_HEAD_DIM + 1] + jnp.exp2(sink - m)
        halves.append(o_t[0:A_HEAD_DIM] / denom)
        if hf == 1:
            o_ref[:, col * LANES:(col + 1) * LANES] = jnp.concatenate(halves, axis=0).T.astype(BF16)
            halves = []
        cur = nxt


def _attn_a(qa, ka, va, sink, lc):
    B, qw, T = qa.shape
    NT = T // TM
    seq = T - lc
    kvw = ka.shape[2]
    vrows = va.shape[1]
    per_tm = TM // A_WINDOW
    last = T // A_WINDOW - 1
    prev_i = lambda t: jnp.maximum(t * per_tm - 1, 0)
    next_i = lambda t: jnp.minimum((t + 1) * per_tm, last)
    kprev = pl.BlockSpec((None, A_WINDOW, kvw), lambda b, t: (b, prev_i(t), 0))
    kmain = pl.BlockSpec((None, TM, kvw), lambda b, t: (b, t, 0))
    knext = pl.BlockSpec((None, A_WINDOW, kvw), lambda b, t: (b, next_i(t), 0))
    kctx = pl.BlockSpec((None, lc, kvw), lambda b, t: (b, 0, 0))
    vprev = pl.BlockSpec((None, vrows, A_WINDOW), lambda b, t: (b, 0, prev_i(t)))
    vmain = pl.BlockSpec((None, vrows, TM), lambda b, t: (b, 0, t))
    vnext = pl.BlockSpec((None, vrows, A_WINDOW), lambda b, t: (b, 0, next_i(t)))
    vctx = pl.BlockSpec((None, vrows, lc), lambda b, t: (b, 0, 0))
    return pl.pallas_call(
        functools.partial(_attn_a_kernel, seq=seq),
        grid=(B, NT),
        in_specs=[pl.BlockSpec((None, qw, TM), lambda b, t: (b, 0, t)),
                  kprev, kmain, knext, kctx, vprev, vmain, vnext, vctx,
                  pl.BlockSpec(sink.shape, lambda b, t: (0, 0))],
        out_specs=pl.BlockSpec((None, TM, qw), lambda b, t: (b, t, 0)),
        out_shape=jax.ShapeDtypeStruct((B, T, qw), BF16),
        compiler_params=pltpu.CompilerParams(dimension_semantics=("parallel", "parallel")),
    )(qa, ka, ka, ka, ka, va, va, va, va, sink)


def _mla_kernel(q_ref, k_ref, v_ref, o_ref, m_ref, acc_ref, sa_ref, sb_ref, *, lc, tk):
    t = pl.program_id(1)
    total = k_ref.shape[0]
    nchunks = jnp.where(t == 0, lc // tk, total // tk)
    m_ref[...] = jnp.full(m_ref.shape, _NEG_INF, F32)
    acc_ref[...] = jnp.zeros(acc_ref.shape, F32)

    def scores(j, hd):
        qs = slice(hd * B_HEAD_PAD, (hd + 1) * B_HEAD_PAD)
        return _dot(k_ref[pl.ds(pl.multiple_of(j * tk, tk), tk), qs], q_ref[qs, :])

    for hd in range(B_HEADS):
        sa_ref[hd] = scores(0, hd)

    def step(j, cur_ref, nxt_ref, last=False):
        for hd in range(B_HEADS):
            vs = slice(hd * B_V_ROWS, (hd + 1) * B_V_ROWS)
            if not last:
                nxt_ref[hd] = scores(j + 1, hd)
            s = cur_ref[hd]
            m_prev = m_ref[hd]
            m_new = jnp.maximum(m_prev, jnp.max(s, axis=0, keepdims=True))
            alpha = jnp.exp2(m_prev - m_new)
            p = jnp.exp2((s - m_new[0:1]).astype(BF16))
            m_ref[hd] = m_new
            acc_ref[vs, :] = acc_ref[vs, :] * alpha[0:1] + _dot(v_ref[j, vs, :], p)

    def body(j2, carry):
        step(2 * j2, sa_ref, sb_ref)
        step(2 * j2 + 1, sb_ref, sa_ref)
        return carry

    npairs = (nchunks - 1) // 2
    lax.fori_loop(0, npairs, body, 0)
    assert (lc // tk) % 2 == (total // tk) % 2
    if (total // tk) % 2 == 0:
        step(2 * npairs, sa_ref, sb_ref)
        step(2 * npairs + 1, sb_ref, sa_ref, last=True)
    else:
        step(2 * npairs, sa_ref, sb_ref, last=True)
    per = LANES // B_V_DIM
    for c in range(B_HEADS // per):
        o_t = jnp.concatenate(
            [acc_ref[hd * B_V_ROWS:hd * B_V_ROWS + B_V_DIM, :]
             / acc_ref[hd * B_V_ROWS + B_V_DIM:hd * B_V_ROWS + B_V_DIM + 1, :]
             for hd in range(c * per, (c + 1) * per)], axis=0)
        o_ref[:, c * LANES:(c + 1) * LANES] = o_t.T.astype(BF16)


def _mla(qb, kb, vb, lc):
    B, qw, T = qb.shape
    NT = T // TM
    _, nck, vw, tk = vb.shape
    ow = B_HEADS * B_V_DIM
    vmem = (T * (qw + vw) * 2 + 4 * TM * qw * 2 + (B_HEADS * SUBLANES * TM + 3 * TM * vw) * 4
            + 2 * B_HEADS * tk * TM * 4 + (16 << 20))
    return pl.pallas_call(
        functools.partial(_mla_kernel, lc=lc, tk=tk),
        grid=(B, NT),
        in_specs=[pl.BlockSpec((None, qw, TM), lambda b, t: (b, 0, t)),
                  pl.BlockSpec((None, T, qw), lambda b, t: (b, 0, 0), pipeline_mode=pl.Buffered(1)),
                  pl.BlockSpec((None, nck, vw, tk), lambda b, t: (b, 0, 0, 0), pipeline_mode=pl.Buffered(1))],
        out_specs=pl.BlockSpec((None, TM, ow), lambda b, t: (b, t, 0)),
        out_shape=jax.ShapeDtypeStruct((B, T, ow), BF16),
        scratch_shapes=[pltpu.VMEM((B_HEADS, SUBLANES, TM), F32),
                        pltpu.VMEM((vw, TM), F32),
                        pltpu.VMEM((B_HEADS, tk, TM), F32), pltpu.VMEM((B_HEADS, tk, TM), F32)],
        compiler_params=pltpu.CompilerParams(
            dimension_semantics=("parallel", "arbitrary"), vmem_limit_bytes=_vmem_limit(vmem)),
    )(qb, kb, vb)


def _gla_prep(q, k, b, reverse):
    C = q.shape[0]
    dk2 = q.shape[1]
    nsub = C // GLA_SUB
    b_end = b[0:1] if reverse else b[C - 1:C]
    row = lax.broadcasted_iota(jnp.int32, (C, dk2), 0)
    lane = lax.broadcasted_iota(jnp.int32, (C, dk2), 1)

    qe = (q * jnp.exp(b)).astype(BF16)
    kd_t = (k * jnp.exp(b_end - b)).T.astype(BF16)
    dec = jnp.exp(jnp.broadcast_to(b_end, (dk2, dk2)).T)
    dec2 = jnp.concatenate([dec] * (2 * C_VAL_DIM // dk2), axis=1)

    refs = [blk * GLA_SUB + (GLA_SUB - 1 if reverse else 0) for blk in range(nsub)]
    b_ref = jnp.concatenate([jnp.broadcast_to(b[n:n + 1], (GLA_SUB, dk2)) for n in refs], axis=0)
    q_sc = q * jnp.exp(b - b_ref)
    k_parts = []
    for blk, n in enumerate(refs):
        in_range = (row >= blk * GLA_SUB) if reverse else (row < (blk + 1) * GLA_SUB)
        k_parts.append(jnp.where(in_range, k * jnp.exp(b[n:n + 1] - b), 0.0))
    k_big = jnp.concatenate(k_parts, axis=1).astype(BF16)
    q_halves = []
    for hf in range(2):
        in_half = (lane >= hf * C_KEY_DIM) & (lane < (hf + 1) * C_KEY_DIM)
        q_halves.append(jnp.concatenate(
            [jnp.where(in_half & (row >= blk * GLA_SUB) & (row < (blk + 1) * GLA_SUB), q_sc, 0.0)
             for blk in range(nsub)], axis=1))
    q_big = jnp.concatenate(q_halves, axis=0).astype(BF16)
    return qe, kd_t, dec2, q_big, k_big


def _gla_kernel(qf_ref, kf_ref, vf_ref, gf_ref, qr_ref, kr_ref, vr_ref, gr_ref, of_ref, or_ref,
                sf_ref, sr_ref):
    t = pl.program_id(1)

    @pl.when(t == 0)
    def _():
        sf_ref[...] = jnp.zeros(sf_ref.shape, F32)
        sr_ref[...] = jnp.zeros(sr_ref.shape, F32)

    C = C_CHUNK
    nchunk = TM // C
    npair = C_HEADS // 2
    kw, vw = 2 * C_KEY_DIM, 2 * C_VAL_DIM
    dirs = ((qf_ref, kf_ref, vf_ref, gf_ref, of_ref, sf_ref, False),
            (qr_ref, kr_ref, vr_ref, gr_ref, or_ref, sr_ref, True))

    ii = lax.broadcasted_iota(jnp.int32, (TM, TM), 0)
    jj = lax.broadcasted_iota(jnp.int32, (TM, TM), 1)
    same_chunk = (ii // C) == (jj // C)
    b_all = []
    for (_, _, _, g_ref, _, _, reverse) in dirs:
        tri = jnp.where(same_chunk & ((jj >= ii) if reverse else (jj <= ii)), 1.0, 0.0).astype(BF16)
        g_hi, g_mid, g_lo = _split_bf16(g_ref[...], 3)
        b_all.append(_dot(tri, g_hi) + (_dot(tri, g_mid) + _dot(tri, g_lo)))

    units = []
    for d, (q_ref, k_ref, v_ref, _, o_ref, _, reverse) in enumerate(dirs):
        for p in range(npair):
            ks, vsl = slice(p * kw, (p + 1) * kw), slice(p * vw, (p + 1) * vw)
            for c in range(nchunk):
                rows = slice(c * C, (c + 1) * C)
                prep = _gla_prep(q_ref[rows, ks], k_ref[rows, ks], b_all[d][rows, ks], reverse)
                units.append((d, p, c, rows, vsl, v_ref, o_ref, reverse, prep))

    srow = lax.broadcasted_iota(jnp.int32, (kw, vw), 0)
    scol = lax.broadcasted_iota(jnp.int32, (kw, vw), 1)
    on_diag = (srow < C_KEY_DIM) == (scol < C_VAL_DIM)
    qi = lax.broadcasted_iota(jnp.int32, (2 * C, C), 0) % C
    kj = lax.broadcasted_iota(jnp.int32, (2 * C, C), 1)
    kvs, atts = [], []
    for (d, p, c, rows, vsl, v_ref, o_ref, reverse, prep) in units:
        _, kd_t, _, q_big, k_big = prep
        kvs.append(jnp.where(on_diag, _dot(kd_t, v_ref[rows, vsl]), 0.0))
        keep = (kj >= qi) if reverse else (kj <= qi)
        atts.append(jnp.where(keep, _dot_nt(q_big, k_big), 0.0).astype(BF16))

    intra = {}
    for u, (d, p, c, rows, vsl, v_ref, o_ref, reverse, prep) in enumerate(units):
        pv = _dot(atts[u], v_ref[rows, vsl])
        intra[(d, p, c)] = (u, jnp.concatenate([pv[0:C, 0:C_VAL_DIM], pv[C:2 * C, C_VAL_DIM:]], axis=1))

    states = {(d, p): dirs[d][5][p] for d in range(len(dirs)) for p in range(npair)}
    for step in range(nchunk):
        for d in range(len(dirs)):
            reverse = dirs[d][6]
            c = nchunk - 1 - step if reverse else step
            for p in range(npair):
                u, o_intra = intra[(d, p, c)]
                _, _, _, rows, vsl, _, o_ref, _, prep = units[u]
                qe, _, dec2, _, _ = prep
                o_ref[rows, vsl] = o_intra + _dot(qe, states[(d, p)].astype(BF16))
                states[(d, p)] = dec2 * states[(d, p)] + kvs[u]
    for d in range(len(dirs)):
        for p in range(npair):
            dirs[d][5][p] = states[(d, p)]


def _gla(cq, ck, cv, g):
    B, T, kw = cq.shape
    NT = T // TM
    vw = cv.shape[2]

    def fwd(n, col=0):
        return pl.BlockSpec((None, TM, n), lambda b, t: (b, t, col))

    def rev(n, col=0):
        return pl.BlockSpec((None, TM, n), lambda b, t: (b, jnp.where(t == 0, 0, NT - t), col))

    return pl.pallas_call(
        _gla_kernel,
        grid=(B, NT),
        in_specs=[fwd(kw), fwd(kw), fwd(vw), fwd(kw, 0), rev(kw), rev(kw), rev(vw), rev(kw, 1)],
        out_specs=[fwd(vw), rev(vw)],
        out_shape=[jax.ShapeDtypeStruct((B, T, vw), F32)] * 2,
        scratch_shapes=[pltpu.VMEM((C_HEADS // 2, 2 * C_KEY_DIM, 2 * C_VAL_DIM), F32)] * 2,
        compiler_params=pltpu.CompilerParams(dimension_semantics=("parallel", "arbitrary")),
    )(cq, ck, cv, g, cq, ck, cv, g)


def _merge_kernel(x_ref, mod_ref, ya_ref, yb_ref, of_ref, or_ref, cr_ref, gate_ref, hn_ref,
                  wa_ref, wb_ref, wc_ref, wo_ref, o_ref):
    d = x_ref.shape[1]
    o = of_ref[...] + or_ref[...]
    r = cr_ref[...]
    parts = []
    for hd in range(C_HEADS):
        sl = slice(hd * C_VAL_DIM, (hd + 1) * C_VAL_DIM)
        parts.append(_rms(o[:, sl], hn_ref[:, sl]))
    yc = (jnp.concatenate(parts, axis=1) * (r * jax.nn.sigmoid(r))).astype(BF16)
    gates = jax.nn.sigmoid(gate_ref[...])
    m = (gates[:, 0:d] * _dot(ya_ref[...], wa_ref[...])
         + gates[:, d:2 * d] * _dot(yb_ref[...], wb_ref[...])
         + gates[:, 2 * d:3 * d] * _dot(yc, wc_ref[...]))
    y = _dot(m.astype(BF16), wo_ref[...])
    o_ref[...] = x_ref[...] + mod_ref[2:3, :] * y


def _merge(X, mod, ya, yb, of, orv, cr, gates, hn, wa, wb, wc, wo):
    B, T, D = X.shape
    NT = T // TM
    row = lambda n: pl.BlockSpec((None, TM, n), lambda b, t: (b, t, 0))
    vmem = (wa.size + wb.size + wc.size + wo.size) * 2 + 2 * TM * (2 * D + 3 * D + 4 * ya.shape[2]) * 4 + (16 << 20)
    return pl.pallas_call(
        _merge_kernel,
        grid=(B, NT),
        in_specs=[row(D), pl.BlockSpec((None, 6, D), _mod_index),
                  row(ya.shape[2]), row(yb.shape[2]), row(of.shape[2]), row(orv.shape[2]),
                  row(cr.shape[2]), row(gates.shape[2]), _const_spec(hn.shape),
                  _const_spec(wa.shape), _const_spec(wb.shape), _const_spec(wc.shape), _const_spec(wo.shape)],
        out_specs=row(D),
        out_shape=jax.ShapeDtypeStruct(X.shape, F32),
        compiler_params=pltpu.CompilerParams(
            dimension_semantics=("parallel", "parallel"), vmem_limit_bytes=_vmem_limit(vmem)),
    )(X, mod, ya, yb, of, orv, cr, gates, hn, wa, wb, wc, wo)


def _ffn_kernel(x_ref, xp_ref, xn_ref, mod_ref, nw_ref, wup_ref, cw_ref, cb_ref, wdn_ref, fw_ref, o_ref,
                ug_ref, uv_ref, *, nt, ncb, t0, final):
    t = pl.program_id(1) + t0
    mod = mod_ref[...]
    nw = nw_ref[...]
    halo = xp_ref.shape[0]
    ffn = wdn_ref.shape[0]
    cb = ffn // ncb

    def norm(xv):
        return _rms(xv, nw) * (1.0 + mod[4:5]) + mod[3:4]

    has_prev = (t >= 2)
    has_next = (t >= 1) & (t < nt - 1)
    hb = jnp.concatenate([jnp.where(has_prev, norm(xp_ref[...]), 0.0), norm(x_ref[...]),
                          jnp.where(has_next, norm(xn_ref[...]), 0.0)], axis=0).astype(BF16)

    acc = jnp.zeros((TM, x_ref.shape[1]), F32)
    for j in range(ncb):
        ug_ref[...] = _dot(hb, wup_ref[:, j * cb:(j + 1) * cb])
        uv_ref[...] = _dot(hb, wup_ref[:, ffn + j * cb:ffn + (j + 1) * cb])

        def conv(u_ref, lo):
            w = cw_ref[:, lo:lo + cb]
            return (u_ref[halo - 1:halo - 1 + TM, :] * w[0:1] + u_ref[halo:halo + TM, :] * w[1:2]
                    + u_ref[halo + 1:halo + 1 + TM, :] * w[2:3] + cb_ref[:, lo:lo + cb])

        gt = conv(ug_ref, j * cb)
        val = conv(uv_ref, ffn + j * cb)
        act = (gt * jax.nn.sigmoid(gt) * val).astype(BF16)
        acc = acc + _dot(act, wdn_ref[j * cb:(j + 1) * cb, :])
    y = x_ref[...] + mod[5:6] * acc
    o_ref[...] = _rms(y, fw_ref[...]) if final else y


def _ffn(X, mod, nw, wup, cw, cb, wdn, fw, final, ncb=2):
    B, T, D = X.shape
    NT = T // TM
    t0 = 1 if final else 0
    halo = SUBLANES
    per = TM // halo
    last = T // halo - 1
    ffn = wdn.shape[0]
    cbw = ffn // ncb
    vmem = (wup.size + wdn.size) * 2 + 2 * (TM + 2 * halo) * cbw * 4 + 8 * TM * D * 4 + (16 << 20)
    return pl.pallas_call(
        functools.partial(_ffn_kernel, nt=NT, ncb=ncb, t0=t0, final=final),
        grid=(B, NT - t0),
        in_specs=[pl.BlockSpec((None, TM, D), lambda b, t: (b, t + t0, 0)),
                  pl.BlockSpec((None, halo, D), lambda b, t: (b, jnp.maximum((t + t0) * per - 1, 0), 0)),
                  pl.BlockSpec((None, halo, D), lambda b, t: (b, jnp.minimum((t + t0 + 1) * per, last), 0)),
                  pl.BlockSpec((None, 6, D), lambda b, t: _mod_index(b, t + t0)),
                  _const_spec(nw.shape), _const_spec(wup.shape), _const_spec(cw.shape), _const_spec(cb.shape),
                  _const_spec(wdn.shape), _const_spec(fw.shape)],
        out_specs=pl.BlockSpec((None, TM, D), lambda b, t: (b, t, 0)),
        out_shape=jax.ShapeDtypeStruct((B, T - t0 * TM, D), F32),
        scratch_shapes=[pltpu.VMEM((TM + 2 * halo, cbw), F32), pltpu.VMEM((TM + 2 * halo, cbw), F32)],
        compiler_params=pltpu.CompilerParams(
            dimension_semantics=("parallel", "parallel"), vmem_limit_bytes=_vmem_limit(vmem)),
    )(X, X, X, mod, nw, wup, cw, cb, wdn, fw)


def _rope_tables(seq, lc):
    rows = seq // GRID_W
    row = jnp.broadcast_to(jnp.arange(rows)[:, None], (rows, GRID_W)).reshape(-1).astype(F32)
    col = jnp.broadcast_to(jnp.arange(GRID_W)[None, :], (rows, GRID_W)).reshape(-1).astype(F32)

    def cs(rot_dim):
        n_freq = rot_dim // 4
        inv = ROPE_BASE ** (-jnp.arange(n_freq, dtype=F32) / n_freq)
        ang = jnp.concatenate([row[:, None] * inv, col[:, None] * inv], axis=-1)
        return jnp.cos(ang), jnp.sin(ang)

    ca, sa = cs(A_HEAD_DIM)
    cb, sb = cs(B_ROPE_DIM)
    reps = LANES // A_HEAD_DIM
    cos_a = jnp.tile(jnp.concatenate([ca, ca], axis=1), (1, reps))
    sin_a = jnp.tile(jnp.concatenate([-sa, sa], axis=1), (1, reps))
    one = jnp.ones((seq, B_NOPE_DIM), F32)
    pad = LANES - B_NOPE_DIM - B_ROPE_DIM
    cos_b = jnp.concatenate([one, cb, cb, jnp.ones((seq, pad), F32)], axis=1)
    sin_b = jnp.concatenate([0 * one, -sb, sb, jnp.zeros((seq, pad), F32)], axis=1)
    ident = jnp.stack([jnp.ones((lc, LANES), F32), jnp.zeros((lc, LANES), F32)] * 2)
    return jnp.concatenate([ident, jnp.stack([cos_a, sin_a, cos_b, sin_b])], axis=1)


def _a_head_perm():
    g = A_HEADS // A_KV_HEADS
    return [h for c in range(g) for h in (c, c + g)]


def _prep_w_in(w_in):
    L, D, _ = w_in.shape
    sizes = [A_HEADS * A_HEAD_DIM, A_KV_HEADS * A_HEAD_DIM, A_KV_HEADS * A_HEAD_DIM, B_Q_RANK, B_KV_RANK,
             B_ROPE_DIM, C_HEADS * C_KEY_DIM, C_HEADS * C_KEY_DIM, C_HEADS * C_VAL_DIM, C_HEADS * C_VAL_DIM,
             2 * C_GATE_RANK, 3 * D]
    offs = [0]
    for s in sizes:
        offs.append(offs[-1] + s)
    aq, ak, av, bq, bkv, bkr, cq, ck, cv, cr, cg, gate = [w_in[:, :, offs[i]:offs[i + 1]] for i in range(len(sizes))]
    aq = aq.reshape(L, D, A_HEADS, A_HEAD_DIM)[:, :, jnp.array(_a_head_perm())].reshape(L, D, -1)
    bkr = jnp.pad(bkr, ((0, 0), (0, 0), (B_NOPE_DIM, LANES - B_NOPE_DIM - B_ROPE_DIM)))
    cg = jnp.pad(cg, ((0, 0), (0, 0), (0, LANES - 2 * C_GATE_RANK)))
    return jnp.concatenate([aq, ak, av, bq, bkv, bkr, cq, ck, cv, cr, cg, gate], axis=-1).astype(BF16)


def kernel(x, c, ctx, c_ctx, w_mod, b_mod, norm_mix, norm_ffn, w_in, a_sink, b_q_norm, b_kv_norm, b_w_uq, b_w_ukv, c_w_gate, c_b_gate, c_head_norm, w_br_a, w_br_b, w_br_c, w_out, w_up, conv_w, conv_b, w_down, final_norm):
    B, S, D = x.shape
    lc = ctx.shape[1]
    L = w_mod.shape[0]
    assert lc == TM and S % TM == 0 and S % GRID_W == 0 and B + 1 <= SUBLANES

    X = jnp.concatenate([ctx, x], axis=1)
    cvec = jnp.zeros((SUBLANES, D), F32).at[0].set(c_ctx).at[1:B + 1].set(c)
    mods = _modulation(cvec, w_mod, b_mod).reshape(L, SUBLANES, 6, D)

    rope = _rope_tables(S, lc)
    perm = jnp.array(_a_head_perm())
    w_in_r = _prep_w_in(w_in)
    sink = jnp.broadcast_to(a_sink[:, perm][:, :, None], (L, A_HEADS, LANES)).astype(F32)
    wa = w_br_a.reshape(L, A_HEADS, A_HEAD_DIM, D)[:, perm].reshape(L, A_HEADS * A_HEAD_DIM, D).astype(BF16)
    wb, wc, wo = w_br_b.astype(BF16), w_br_c.astype(BF16), w_out.astype(BF16)
    qk = B_NOPE_DIM + B_ROPE_DIM
    wuq = jnp.pad(b_w_uq.reshape(L, B_Q_RANK, B_HEADS, qk),
                  ((0, 0), (0, 0), (0, 0), (0, B_HEAD_PAD - qk))).reshape(L, B_Q_RANK, -1).astype(BF16)
    ukv = b_w_ukv.reshape(L, B_KV_RANK, B_HEADS, B_NOPE_DIM + B_V_DIM)
    wkn = jnp.pad(ukv[..., :B_NOPE_DIM],
                  ((0, 0), (0, 0), (0, 0), (0, B_HEAD_PAD - B_NOPE_DIM))).reshape(L, B_KV_RANK, -1).astype(BF16)
    wvb = ukv[..., B_NOPE_DIM:].reshape(L, B_KV_RANK, -1).astype(BF16)
    nqk = C_HEADS * C_KEY_DIM
    wg = jnp.zeros((L, LANES, 2 * nqk), F32)
    wg = wg.at[:, 0:C_GATE_RANK, 0:nqk].set(c_w_gate[:, 0]).at[:, C_GATE_RANK:2 * C_GATE_RANK, nqk:].set(c_w_gate[:, 1])
    bg = c_b_gate.reshape(L, 1, 2 * nqk)
    wup, wdn = w_up.astype(BF16), w_down.astype(BF16)

    for l in range(L):
        mod = mods[l]
        (qa, ka, va, qb, kb, vb, cq, ck, cv, cr, g, gates) = _inproj(
            X, mod, norm_mix[l][None], w_in_r[l], rope, b_q_norm[l][None], b_kv_norm[l][None],
            wuq[l], wkn[l], wvb[l], wg[l], bg[l])
        ya = _attn_a(qa, ka, va, sink[l], lc)
        yb = _mla(qb, kb, vb, lc)
        of, orv = _gla(cq, ck, cv, g)
        X = _merge(X, mod, ya, yb, of, orv, cr, gates, c_head_norm[l][None], wa[l], wb[l], wc[l], wo[l])
        X = _ffn(X, mod, norm_ffn[l][None], wup[l], conv_w[l], conv_b[l][None], wdn[l], final_norm[None],
                 final=(l == L - 1))
    return X
```

```python
import functools

import jax
import jax.numpy as jnp
from jax import lax
from jax.experimental import pallas as pl
from jax.experimental.pallas import tpu as pltpu

GRID_W = 64
EPS = 1e-6
ROPE_BASE = 10000.0
A_HEADS, A_KV_HEADS, A_HEAD_DIM, A_WINDOW = 8, 2, 64, 128
B_HEADS, B_Q_RANK, B_KV_RANK, B_NOPE_DIM, B_ROPE_DIM, B_V_DIM = 8, 256, 128, 64, 32, 64
C_HEADS, C_KEY_DIM, C_VAL_DIM, C_GATE_RANK, C_GATE_TAU, C_CHUNK = 4, 64, 128, 16, 16.0, 64
CONV_W = 3

LANES = 128
SUBLANES = 8
VMEM_BYTES = 64 * 1024 * 1024

TM = 256
GLA_SUB = 16
MLA_TK = 256
MOD_COLS = 1536
F32 = jnp.float32
BF16 = jnp.bfloat16
_HI = lax.Precision.HIGHEST
_NEG_INF = float("-inf")


def _dot(a, b):
    return jnp.dot(a, b, preferred_element_type=F32)


def _dot_nt(a, b):
    return lax.dot_general(a, b, (((1,), (1,)), ((), ())), preferred_element_type=F32)


def _split_bf16(x, terms):
    parts = []
    for _ in range(terms):
        p = x.astype(BF16)
        parts.append(p)
        x = x - p.astype(F32)
    return parts


def _dot_split(a, b):
    a_hi, a_lo = _split_bf16(a, 2)
    b_hi, b_lo = _split_bf16(b, 2)
    return _dot(a_hi, b_hi) + (_dot(a_lo, b_hi) + _dot(a_hi, b_lo))


def _rms(x, w):
    return x * lax.rsqrt(jnp.mean(x * x, axis=-1, keepdims=True) + EPS) * w


def _swap_lane_groups(x, half):
    lane = lax.broadcasted_iota(jnp.int32, x.shape, 1)
    up = pltpu.roll(x, LANES - half, 1)
    down = pltpu.roll(x, half, 1)
    return jnp.where((lane & half) == 0, up, down)


def _vmem_limit(nbytes):
    return int(min(VMEM_BYTES - (4 << 20), max(nbytes, 32 << 20)))


def _layer_spec(stacked, layer):
    shape = stacked.shape[1:]
    nd = len(shape)
    return pl.BlockSpec((None,) + shape, lambda *_: (layer,) + (0,) * nd, pipeline_mode=pl.Buffered(1))


def _const_spec(shape):
    nd = len(shape)
    return pl.BlockSpec(shape, lambda *_: (0,) * nd, pipeline_mode=pl.Buffered(1))


def _mod_kernel(c_ref, w_ref, b_ref, o_ref):
    cv = c_ref[...]
    sc = cv * jax.nn.sigmoid(cv)
    o_ref[...] = jnp.dot(sc, w_ref[...], precision=_HI, preferred_element_type=F32) + b_ref[...]


def _modulation(cvec, w_mod, b_mod):
    L, D, N = w_mod.shape
    nb = N // MOD_COLS
    return pl.pallas_call(
        _mod_kernel,
        grid=(L, nb),
        in_specs=[
            pl.BlockSpec((SUBLANES, D), lambda l, j: (0, 0)),
            pl.BlockSpec((None, D, MOD_COLS), lambda l, j: (l, 0, j)),
            pl.BlockSpec((None, 1, MOD_COLS), lambda l, j: (l, 0, j)),
        ],
        out_specs=pl.BlockSpec((None, SUBLANES, MOD_COLS), lambda l, j: (l, 0, j)),
        out_shape=jax.ShapeDtypeStruct((L, SUBLANES, N), F32),
        compiler_params=pltpu.CompilerParams(
            dimension_semantics=("parallel", "parallel"),
            vmem_limit_bytes=_vmem_limit(3 * D * MOD_COLS * 4)),
    )(cvec, w_mod, b_mod.reshape(L, 1, N))


_C_AQ = 0
_C_AK = _C_AQ + A_HEADS * A_HEAD_DIM
_C_AV = _C_AK + A_KV_HEADS * A_HEAD_DIM
_C_BQ = _C_AV + A_KV_HEADS * A_HEAD_DIM
_C_BKV = _C_BQ + B_Q_RANK
_C_BKR = _C_BKV + B_KV_RANK
_C_CQ = _C_BKR + LANES
_C_CK = _C_CQ + C_HEADS * C_KEY_DIM
_C_CV = _C_CK + C_HEADS * C_KEY_DIM
_C_CR = _C_CV + C_HEADS * C_VAL_DIM
_C_CG = _C_CR + C_HEADS * C_VAL_DIM
_C_GATE = _C_CG + LANES
B_HEAD_PAD = LANES
B_V_ROWS = B_V_DIM + 2 * SUBLANES
A_V_ROWS = A_HEAD_DIM + 2 * SUBLANES
LOG2E = 1.4426950408889634


def _stream_specs(xc, xl):
    off = 0 if xc is xl else 1
    d = xl.shape[2]
    spec_c = pl.BlockSpec((None, TM, d), lambda b, t: (b, 0, 0))
    spec_l = pl.BlockSpec((None, TM, d), lambda b, t: (b, jnp.maximum(t - off, 0), 0))
    return spec_c, spec_l, xl.shape[1] + off * TM


def _stream_tile(xc_ref, xl_ref):
    return jnp.where(pl.program_id(1) == 0, xc_ref[...], xl_ref[...])


def _inproj_kernel(xc_ref, xl_ref, mod_ref, nw_ref, w1_ref, wkr_ref, w2_ref, wcg_ref, w3_ref, rope_ref,
                   bqn_ref, bkvn_ref, wuq_ref, wkn_ref,
                   wvb_ref, wg_ref, bg_ref,
                   qa_ref, ka_ref, va_ref, qb_ref, kb_ref, vb_ref, cq_ref, ck_ref, cv_ref, cr_ref,
                   g_ref, gate_ref, *, d_model):
    mod = mod_ref[...]
    h = _rms(_stream_tile(xc_ref, xl_ref), nw_ref[...]) * (1.0 + mod[1:2]) + mod[0:1]
    hb = h.astype(BF16)

    pieces = ((_C_AQ, w1_ref), (_C_BKR, wkr_ref), (_C_CQ, w2_ref), (_C_CG, wcg_ref), (_C_GATE, w3_ref))

    def proj(lo, hi):
        base, ref = [(b0, r) for b0, r in pieces if b0 <= lo][-1]
        assert hi - base <= ref.shape[1]
        return _dot(hb, ref[:, lo - base:hi - base])

    cos_a, sin_a, cos_b, sin_b = rope_ref[0], rope_ref[1], rope_ref[2], rope_ref[3]

    def rope_a(t):
        return t * cos_a + _swap_lane_groups(t, A_HEAD_DIM // 2) * sin_a

    def rope_b(t):
        return t * cos_b + _swap_lane_groups(t, B_ROPE_DIM // 2) * sin_b

    a_scale = A_HEAD_DIM ** -0.5 * LOG2E
    for c in range(A_HEADS * A_HEAD_DIM // LANES):
        t = proj(_C_AQ + c * LANES, _C_AQ + (c + 1) * LANES)
        qa_ref[c * LANES:(c + 1) * LANES, :] = (rope_a(t) * a_scale).T.astype(BF16)
    ka_ref[...] = rope_a(proj(_C_AK, _C_AV)).astype(BF16)
    va_t = proj(_C_AV, _C_BQ).T.astype(BF16)
    a_ones = jnp.where(lax.broadcasted_iota(jnp.int32, (A_V_ROWS - A_HEAD_DIM, va_t.shape[1]), 0) == 0,
                       1.0, 0.0).astype(BF16)
    for g in range(A_KV_HEADS):
        va_ref[g * A_V_ROWS:g * A_V_ROWS + A_HEAD_DIM, :] = va_t[g * A_HEAD_DIM:(g + 1) * A_HEAD_DIM]
        va_ref[g * A_V_ROWS + A_HEAD_DIM:(g + 1) * A_V_ROWS, :] = a_ones

    b_scale = (B_NOPE_DIM + B_ROPE_DIM) ** -0.5 * LOG2E
    cqn = _rms(proj(_C_BQ, _C_BKV), bqn_ref[...]).astype(BF16)
    ckvn = _rms(proj(_C_BKV, _C_BKR), bkvn_ref[...]).astype(BF16)
    kr = rope_b(proj(_C_BKR, _C_CQ))
    for hd in range(B_HEADS):
        sl = slice(hd * B_HEAD_PAD, (hd + 1) * B_HEAD_PAD)
        qb_ref[sl, :] = (rope_b(_dot(cqn, wuq_ref[:, sl])) * b_scale).T.astype(BF16)
        kb_ref[:, sl] = (_dot(ckvn, wkn_ref[:, sl]) + kr).astype(BF16)
    vb = _dot(ckvn, wvb_ref[...])
    tk = vb_ref.shape[2]
    per = LANES // B_V_DIM
    ones_rows = jnp.where(lax.broadcasted_iota(jnp.int32, (B_V_ROWS - B_V_DIM, tk), 0) == 0, 1.0, 0.0).astype(BF16)
    for cc in range(vb_ref.shape[0]):
        for c in range(vb.shape[1] // LANES):
            v_t = vb[cc * tk:(cc + 1) * tk, c * LANES:(c + 1) * LANES].T.astype(BF16)
            for i in range(per):
                r0 = (c * per + i) * B_V_ROWS
                vb_ref[cc, r0:r0 + B_V_DIM, :] = v_t[i * B_V_DIM:(i + 1) * B_V_DIM]
                vb_ref[cc, r0 + B_V_DIM:r0 + B_V_ROWS, :] = ones_rows

    cq_ref[...] = proj(_C_CQ, _C_CK) * (C_KEY_DIM ** -0.5)
    ck_ref[...] = proj(_C_CK, _C_CV)
    cv_ref[...] = proj(_C_CV, _C_CR).astype(BF16)
    cr_ref[...] = proj(_C_CR, _C_CG)
    z = _dot_split(proj(_C_CG, _C_GATE), wg_ref[...]) + bg_ref[...]
    g_ref[...] = (jnp.minimum(z, 0.0) - jnp.log1p(jnp.exp(-jnp.abs(z)))) * (1.0 / C_GATE_TAU)

    gate_ref[...] = proj(_C_GATE, _C_GATE + 3 * d_model)


def _mod_index(b, t):
    return (jnp.where(t == 0, 0, b + 1), 0, 0)


def _inproj(Xc, Xl, mod, nw, layer, w_pieces, rope, bqn, bkvn, wuq, wkn, wvb, wg, bg):
    spec_c, spec_l, T = _stream_specs(Xc, Xl)
    B, _, D = Xl.shape
    NT = T // TM
    n_in = sum(w.shape[2] for w in w_pieces)
    row = lambda n: pl.BlockSpec((None, TM, n), lambda b, t: (b, t, 0))
    widths = [(A_HEADS * A_HEAD_DIM, BF16), (A_KV_HEADS * A_HEAD_DIM, BF16), (A_KV_HEADS * A_HEAD_DIM, BF16),
              (B_HEADS * B_HEAD_PAD, BF16), (B_HEADS * B_HEAD_PAD, BF16), (B_HEADS * B_V_DIM, BF16),
              (C_HEADS * C_KEY_DIM, F32), (C_HEADS * C_KEY_DIM, F32), (C_HEADS * C_VAL_DIM, BF16),
              (C_HEADS * C_VAL_DIM, F32), (2 * C_HEADS * C_KEY_DIM, F32), (3 * D, F32)]
    out_bytes = sum(TM * n * jnp.dtype(dt).itemsize for n, dt in widths)
    vmem = D * n_in * 2 + 4 * out_bytes + 6 * TM * D * 4 + (8 << 20)
    out_specs = [row(n) for n, _ in widths]
    out_shape = [jax.ShapeDtypeStruct((B, T, n), dt) for n, dt in widths]
    qw, vw = B_HEADS * B_HEAD_PAD, B_HEADS * B_V_ROWS
    out_specs[3] = pl.BlockSpec((None, qw, TM), lambda b, t: (b, 0, t))
    out_shape[3] = jax.ShapeDtypeStruct((B, qw, T), BF16)
    out_specs[0] = pl.BlockSpec((None, widths[0][0], TM), lambda b, t: (b, 0, t))
    out_shape[0] = jax.ShapeDtypeStruct((B, widths[0][0], T), BF16)
    out_specs[2] = pl.BlockSpec((None, A_KV_HEADS * A_V_ROWS, TM), lambda b, t: (b, 0, t))
    out_shape[2] = jax.ShapeDtypeStruct((B, A_KV_HEADS * A_V_ROWS, T), BF16)
    out_specs[5] = pl.BlockSpec((None, TM // MLA_TK, vw, MLA_TK), lambda b, t: (b, t, 0, 0))
    out_shape[5] = jax.ShapeDtypeStruct((B, T // MLA_TK, vw, MLA_TK), BF16)
    return pl.pallas_call(
        functools.partial(_inproj_kernel, d_model=D),
        grid=(B, NT),
        in_specs=[
            spec_c, spec_l,
            pl.BlockSpec((None, 6, D), _mod_index),
            _const_spec((1, D)),
            *[_layer_spec(w, layer) for w in w_pieces],
            pl.BlockSpec((4, TM, LANES), lambda b, t: (0, t, 0)),
            _const_spec(bqn.shape), _const_spec(bkvn.shape), _layer_spec(wuq, layer),
            _layer_spec(wkn, layer), _layer_spec(wvb, layer), _layer_spec(wg, layer), _const_spec(bg.shape),
        ],
        out_specs=out_specs,
        out_shape=out_shape,
        compiler_params=pltpu.CompilerParams(
            dimension_semantics=("parallel", "parallel"), vmem_limit_bytes=_vmem_limit(vmem)),
    )(Xc, Xl, mod, nw, *w_pieces, rope, bqn, bkvn, wuq, wkn, wvb, wg, bg)


def _attn_a_kernel(q_ref, kp_ref, km_ref, kn_ref, kc_ref, vp_ref, vm_ref, vn_ref, vc_ref, sink_ref,
                   o_ref, *, seq):
    t = pl.program_id(1)
    kloc = jnp.concatenate([kp_ref[...], km_ref[...], kn_ref[...]], axis=0)
    vloc = jnp.concatenate([vp_ref[...], vm_ref[...], vn_ref[...]], axis=1)
    kctx, vctx = kc_ref[...], vc_ref[...]
    nloc = kloc.shape[0]
    r = lax.broadcasted_iota(jnp.int32, (nloc, TM), 0)
    c = lax.broadcasted_iota(jnp.int32, (nloc, TM), 1)
    rel = r - A_WINDOW - c
    kpos = (t - 1) * TM - A_WINDOW + r
    valid = (jnp.abs(rel) <= A_WINDOW) & (kpos >= 0) & (kpos < seq) & (t >= 1)
    group = A_HEADS // A_KV_HEADS
    per = LANES // A_HEAD_DIM
    q_zero = jnp.zeros((A_HEAD_DIM, TM), BF16)

    def scores(hd):
        q_t = q_ref[hd * A_HEAD_DIM:(hd + 1) * A_HEAD_DIM, :]
        g = hd // group
        qm = jnp.concatenate([q_t if i == g else q_zero for i in range(A_KV_HEADS)], axis=0)
        return _dot(kloc, qm), _dot(kctx, qm)

    cur = scores(0)
    parts = []
    for hd in range(A_HEADS):
        nxt = scores(hd + 1) if hd + 1 < A_HEADS else None
        s_loc = jnp.where(valid, cur[0], _NEG_INF)
        s_ctx = cur[1]
        sink = sink_ref[hd:hd + 1, 0:1] * LOG2E
        m = jnp.maximum(jnp.maximum(jnp.max(s_loc, axis=0, keepdims=True),
                                    jnp.max(s_ctx, axis=0, keepdims=True)), sink)
        p_loc = jnp.exp2((s_loc - m).astype(BF16))
        p_ctx = jnp.exp2((s_ctx - m).astype(BF16))
        g = hd // group
        vs = slice(g * A_V_ROWS, (g + 1) * A_V_ROWS)
        o_t = _dot(vloc[vs], p_loc) + _dot(vctx[vs], p_ctx)
        denom = o_t[A_HEAD_DIM:A_HEAD_DIM + 1] + jnp.exp2(sink - m)
        parts.append(o_t[0:A_HEAD_DIM] / denom)
        if len(parts) == per:
            c = hd // per
            o_ref[:, c * LANES:(c + 1) * LANES] = jnp.concatenate(parts, axis=0).T.astype(BF16)
            parts = []
        cur = nxt


def _attn_a(qa, ka, va, sink, lc):
    B, qw, T = qa.shape
    NT = T // TM
    seq = T - lc
    kvw = ka.shape[2]
    vrows = va.shape[1]
    per_tm = TM // A_WINDOW
    last = T // A_WINDOW - 1
    prev_i = lambda t: jnp.maximum(t * per_tm - 1, 0)
    next_i = lambda t: jnp.minimum((t + 1) * per_tm, last)
    kprev = pl.BlockSpec((None, A_WINDOW, kvw), lambda b, t: (b, prev_i(t), 0))
    kmain = pl.BlockSpec((None, TM, kvw), lambda b, t: (b, t, 0))
    knext = pl.BlockSpec((None, A_WINDOW, kvw), lambda b, t: (b, next_i(t), 0))
    kctx = pl.BlockSpec((None, lc, kvw), lambda b, t: (b, 0, 0))
    vprev = pl.BlockSpec((None, vrows, A_WINDOW), lambda b, t: (b, 0, prev_i(t)))
    vmain = pl.BlockSpec((None, vrows, TM), lambda b, t: (b, 0, t))
    vnext = pl.BlockSpec((None, vrows, A_WINDOW), lambda b, t: (b, 0, next_i(t)))
    vctx = pl.BlockSpec((None, vrows, lc), lambda b, t: (b, 0, 0))
    return pl.pallas_call(
        functools.partial(_attn_a_kernel, seq=seq),
        grid=(B, NT),
        in_specs=[pl.BlockSpec((None, qw, TM), lambda b, t: (b, 0, t)),
                  kprev, kmain, knext, kctx, vprev, vmain, vnext, vctx,
                  pl.BlockSpec(sink.shape, lambda b, t: (0, 0))],
        out_specs=pl.BlockSpec((None, TM, qw), lambda b, t: (b, t, 0)),
        out_shape=jax.ShapeDtypeStruct((B, T, qw), BF16),
        compiler_params=pltpu.CompilerParams(dimension_semantics=("parallel", "parallel")),
    )(qa, ka, ka, ka, ka, va, va, va, va, sink)


def _mla_kernel(q_ref, k_ref, v_ref, o_ref, m_ref, acc_ref, sa_ref, sb_ref, *, lc, tk):
    t = pl.program_id(1)
    total = k_ref.shape[0]
    nchunks = jnp.where(t == 0, lc // tk, total // tk)
    m_ref[...] = jnp.full(m_ref.shape, _NEG_INF, F32)
    acc_ref[...] = jnp.zeros(acc_ref.shape, F32)

    def stage(dst_ref, j, hd):
        qs = slice(hd * B_HEAD_PAD, (hd + 1) * B_HEAD_PAD)
        s = _dot(k_ref[pl.ds(pl.multiple_of(j * tk, tk), tk), qs], q_ref[qs, :])
        dst_ref[hd, 0:tk, :] = s
        dst_ref[hd, tk:tk + SUBLANES, :] = jnp.broadcast_to(jnp.max(s, axis=0, keepdims=True), (SUBLANES, TM))

    for hd in range(B_HEADS):
        stage(sa_ref, 0, hd)

    def step(j, cur_ref, nxt_ref, last=False):
        for hd in range(B_HEADS):
            vs = slice(hd * B_V_ROWS, (hd + 1) * B_V_ROWS)
            if not last:
                stage(nxt_ref, j + 1, hd)
            m_prev = m_ref[hd]
            m_new = jnp.maximum(m_prev, cur_ref[hd, tk:tk + SUBLANES, :])
            alpha = jnp.exp2(m_prev - m_new)
            p = jnp.exp2((cur_ref[hd, 0:tk, :] - m_new[0:1]).astype(BF16))
            m_ref[hd] = m_new
            acc_ref[vs, :] = acc_ref[vs, :] * alpha[0:1] + _dot(v_ref[j, vs, :], p)

    def body(j2, carry):
        step(2 * j2, sa_ref, sb_ref)
        step(2 * j2 + 1, sb_ref, sa_ref)
        return carry

    npairs = (nchunks - 1) // 2
    lax.fori_loop(0, npairs, body, 0)
    assert (lc // tk) % 2 == (total // tk) % 2
    if (total // tk) % 2 == 0:
        step(2 * npairs, sa_ref, sb_ref)
        step(2 * npairs + 1, sb_ref, sa_ref, last=True)
    else:
        step(2 * npairs, sa_ref, sb_ref, last=True)
    per = LANES // B_V_DIM
    for c in range(B_HEADS // per):
        o_t = jnp.concatenate(
            [acc_ref[hd * B_V_ROWS:hd * B_V_ROWS + B_V_DIM, :]
             / acc_ref[hd * B_V_ROWS + B_V_DIM:hd * B_V_ROWS + B_V_DIM + 1, :]
             for hd in range(c * per, (c + 1) * per)], axis=0)
        o_ref[:, c * LANES:(c + 1) * LANES] = o_t.T.astype(BF16)


def _mla(qb, kb, vb, lc):
    B, qw, T = qb.shape
    NT = T // TM
    _, nck, vw, tk = vb.shape
    ow = B_HEADS * B_V_DIM
    vmem = (T * (qw + vw) * 2 + 4 * TM * qw * 2 + (B_HEADS * SUBLANES * TM + 3 * TM * vw) * 4
            + 2 * B_HEADS * (tk + SUBLANES) * TM * 4 + (16 << 20))
    return pl.pallas_call(
        functools.partial(_mla_kernel, lc=lc, tk=tk),
        grid=(B, NT),
        in_specs=[pl.BlockSpec((None, qw, TM), lambda b, t: (b, 0, t)),
                  pl.BlockSpec((None, T, qw), lambda b, t: (b, 0, 0), pipeline_mode=pl.Buffered(1)),
                  pl.BlockSpec((None, nck, vw, tk), lambda b, t: (b, 0, 0, 0), pipeline_mode=pl.Buffered(1))],
        out_specs=pl.BlockSpec((None, TM, ow), lambda b, t: (b, t, 0)),
        out_shape=jax.ShapeDtypeStruct((B, T, ow), BF16),
        scratch_shapes=[pltpu.VMEM((B_HEADS, SUBLANES, TM), F32),
                        pltpu.VMEM((vw, TM), F32),
                        pltpu.VMEM((B_HEADS, tk + SUBLANES, TM), F32),
                        pltpu.VMEM((B_HEADS, tk + SUBLANES, TM), F32)],
        compiler_params=pltpu.CompilerParams(
            dimension_semantics=("parallel", "arbitrary"), vmem_limit_bytes=_vmem_limit(vmem)),
    )(qb, kb, vb)


def _gla_prep(q, k, b, reverse):
    C = q.shape[0]
    dk2 = q.shape[1]
    nsub = C // GLA_SUB
    b_end = b[0:1] if reverse else b[C - 1:C]
    row = lax.broadcasted_iota(jnp.int32, (C, dk2), 0)
    lane = lax.broadcasted_iota(jnp.int32, (C, dk2), 1)

    qe = (q * jnp.exp(b)).astype(BF16)
    kd_t = (k * jnp.exp(b_end - b)).T.astype(BF16)
    dec = jnp.exp(jnp.broadcast_to(b_end, (dk2, dk2)).T)
    dec2 = jnp.concatenate([dec] * (2 * C_VAL_DIM // dk2), axis=1)

    refs = [blk * GLA_SUB + (GLA_SUB - 1 if reverse else 0) for blk in range(nsub)]
    b_ref = jnp.concatenate([jnp.broadcast_to(b[n:n + 1], (GLA_SUB, dk2)) for n in refs], axis=0)
    q_sc = q * jnp.exp(b - b_ref)
    k_parts = []
    for blk, n in enumerate(refs):
        in_range = (row >= blk * GLA_SUB) if reverse else (row < (blk + 1) * GLA_SUB)
        k_parts.append(jnp.where(in_range, k * jnp.exp(b[n:n + 1] - b), 0.0))
    k_big = jnp.concatenate(k_parts, axis=1).astype(BF16)
    q_halves = []
    for hf in range(2):
        in_half = (lane >= hf * C_KEY_DIM) & (lane < (hf + 1) * C_KEY_DIM)
        q_halves.append(jnp.concatenate(
            [jnp.where(in_half & (row >= blk * GLA_SUB) & (row < (blk + 1) * GLA_SUB), q_sc, 0.0)
             for blk in range(nsub)], axis=1))
    q_big = jnp.concatenate(q_halves, axis=0).astype(BF16)
    return qe, kd_t, dec2, q_big, k_big


def _gla_kernel(qf_ref, kf_ref, vf_ref, gf_ref, qr_ref, kr_ref, vr_ref, gr_ref, of_ref, or_ref,
                sf_ref, sr_ref):
    t = pl.program_id(1)

    @pl.when(t == 0)
    def _():
        sf_ref[...] = jnp.zeros(sf_ref.shape, F32)
        sr_ref[...] = jnp.zeros(sr_ref.shape, F32)

    C = C_CHUNK
    nchunk = TM // C
    npair = C_HEADS // 2
    kw, vw = 2 * C_KEY_DIM, 2 * C_VAL_DIM
    dirs = ((qf_ref, kf_ref, vf_ref, gf_ref, of_ref, sf_ref, False),
            (qr_ref, kr_ref, vr_ref, gr_ref, or_ref, sr_ref, True))

    ii = lax.broadcasted_iota(jnp.int32, (TM, TM), 0)
    jj = lax.broadcasted_iota(jnp.int32, (TM, TM), 1)
    same_chunk = (ii // C) == (jj // C)
    b_all = []
    for (_, _, _, g_ref, _, _, reverse) in dirs:
        tri = jnp.where(same_chunk & ((jj >= ii) if reverse else (jj <= ii)), 1.0, 0.0).astype(BF16)
        g_hi, g_mid, g_lo = _split_bf16(g_ref[...], 3)
        b_all.append(_dot(tri, g_hi) + (_dot(tri, g_mid) + _dot(tri, g_lo)))

    units = []
    for d, (q_ref, k_ref, v_ref, _, o_ref, _, reverse) in enumerate(dirs):
        for p in range(npair):
            ks, vsl = slice(p * kw, (p + 1) * kw), slice(p * vw, (p + 1) * vw)
            for c in range(nchunk):
                rows = slice(c * C, (c + 1) * C)
                prep = _gla_prep(q_ref[rows, ks], k_ref[rows, ks], b_all[d][rows, ks], reverse)
                units.append((d, p, c, rows, vsl, v_ref, o_ref, reverse, prep))

    srow = lax.broadcasted_iota(jnp.int32, (kw, vw), 0)
    scol = lax.broadcasted_iota(jnp.int32, (kw, vw), 1)
    on_diag = (srow < C_KEY_DIM) == (scol < C_VAL_DIM)
    qi = lax.broadcasted_iota(jnp.int32, (2 * C, C), 0) % C
    kj = lax.broadcasted_iota(jnp.int32, (2 * C, C), 1)
    kvs, atts = [], []
    for (d, p, c, rows, vsl, v_ref, o_ref, reverse, prep) in units:
        _, kd_t, _, q_big, k_big = prep
        kvs.append(jnp.where(on_diag, _dot(kd_t, v_ref[rows, vsl]), 0.0))
        keep = (kj >= qi) if reverse else (kj <= qi)
        atts.append(jnp.where(keep, _dot_nt(q_big, k_big), 0.0).astype(BF16))

    intra = {}
    for u, (d, p, c, rows, vsl, v_ref, o_ref, reverse, prep) in enumerate(units):
        pv = _dot(atts[u], v_ref[rows, vsl])
        intra[(d, p, c)] = (u, jnp.concatenate([pv[0:C, 0:C_VAL_DIM], pv[C:2 * C, C_VAL_DIM:]], axis=1))

    states = {(d, p): dirs[d][5][p] for d in range(len(dirs)) for p in range(npair)}
    for step in range(nchunk):
        for d in range(len(dirs)):
            reverse = dirs[d][6]
            c = nchunk - 1 - step if reverse else step
            for p in range(npair):
                u, o_intra = intra[(d, p, c)]
                _, _, _, rows, vsl, _, o_ref, _, prep = units[u]
                qe, _, dec2, _, _ = prep
                o_ref[rows, vsl] = o_intra + _dot(qe, states[(d, p)].astype(BF16))
                states[(d, p)] = dec2 * states[(d, p)] + kvs[u]
    for d in range(len(dirs)):
        for p in range(npair):
            dirs[d][5][p] = states[(d, p)]


def _gla(cq, ck, cv, g):
    B, T, kw = cq.shape
    NT = T // TM
    vw = cv.shape[2]

    def fwd(n, col=0):
        return pl.BlockSpec((None, TM, n), lambda b, t: (b, t, col))

    def rev(n, col=0):
        return pl.BlockSpec((None, TM, n), lambda b, t: (b, jnp.where(t == 0, 0, NT - t), col))

    return pl.pallas_call(
        _gla_kernel,
        grid=(B, NT),
        in_specs=[fwd(kw), fwd(kw), fwd(vw), fwd(kw, 0), rev(kw), rev(kw), rev(vw), rev(kw, 1)],
        out_specs=[fwd(vw), rev(vw)],
        out_shape=[jax.ShapeDtypeStruct((B, T, vw), F32)] * 2,
        scratch_shapes=[pltpu.VMEM((C_HEADS // 2, 2 * C_KEY_DIM, 2 * C_VAL_DIM), F32)] * 2,
        compiler_params=pltpu.CompilerParams(dimension_semantics=("parallel", "arbitrary")),
    )(cq, ck, cv, g, cq, ck, cv, g)


def _merge_kernel(xc_ref, xl_ref, mod_ref, ya_ref, yb_ref, of_ref, or_ref, cr_ref, gate_ref, hn_ref,
                  wa_ref, wb_ref, wc_ref, wo_ref, o_ref):
    d = xl_ref.shape[1]
    o = of_ref[...] + or_ref[...]
    r = cr_ref[...]
    parts = []
    for hd in range(C_HEADS):
        sl = slice(hd * C_VAL_DIM, (hd + 1) * C_VAL_DIM)
        parts.append(_rms(o[:, sl], hn_ref[:, sl]))
    yc = (jnp.concatenate(parts, axis=1) * (r * jax.nn.sigmoid(r))).astype(BF16)
    gates = jax.nn.sigmoid(gate_ref[...])
    m = (gates[:, 0:d] * _dot(ya_ref[...], wa_ref[...])
         + gates[:, d:2 * d] * _dot(yb_ref[...], wb_ref[...])
         + gates[:, 2 * d:3 * d] * _dot(yc, wc_ref[...]))
    y = _dot(m.astype(BF16), wo_ref[...])
    o_ref[...] = _stream_tile(xc_ref, xl_ref) + mod_ref[2:3, :] * y


def _merge(Xc, Xl, mod, ya, yb, of, orv, cr, gates, hn, layer, wa, wb, wc, wo):
    spec_c, spec_l, T = _stream_specs(Xc, Xl)
    B, _, D = Xl.shape
    NT = T // TM
    row = lambda n: pl.BlockSpec((None, TM, n), lambda b, t: (b, t, 0))
    wbytes = sum(w.size // w.shape[0] for w in (wa, wb, wc, wo)) * 2
    vmem = wbytes + 2 * TM * (2 * D + 3 * D + 4 * ya.shape[2]) * 4 + (16 << 20)
    return pl.pallas_call(
        _merge_kernel,
        grid=(B, NT),
        in_specs=[spec_c, spec_l, pl.BlockSpec((None, 6, D), _mod_index),
                  row(ya.shape[2]), row(yb.shape[2]), row(of.shape[2]), row(orv.shape[2]),
                  row(cr.shape[2]), row(gates.shape[2]), _const_spec(hn.shape),
                  _layer_spec(wa, layer), _layer_spec(wb, layer), _layer_spec(wc, layer), _layer_spec(wo, layer)],
        out_specs=row(D),
        out_shape=jax.ShapeDtypeStruct((B, T, D), F32),
        compiler_params=pltpu.CompilerParams(
            dimension_semantics=("parallel", "parallel"), vmem_limit_bytes=_vmem_limit(vmem)),
    )(Xc, Xl, mod, ya, yb, of, orv, cr, gates, hn, wa, wb, wc, wo)


def _ffn_kernel(x_ref, xp_ref, xn_ref, mod_ref, nw_ref, wup_ref, cw_ref, cb_ref, wdn_ref, fw_ref, o_ref,
                ug_ref, uv_ref, *, nt, ncb, t0, final):
    t = pl.program_id(1) + t0
    mod = mod_ref[...]
    nw = nw_ref[...]
    halo = xp_ref.shape[0]
    ffn = wdn_ref.shape[0]
    cb = ffn // ncb

    def norm(xv):
        return _rms(xv, nw) * (1.0 + mod[4:5]) + mod[3:4]

    has_prev = (t >= 2)
    has_next = (t >= 1) & (t < nt - 1)
    hb = jnp.concatenate([jnp.where(has_prev, norm(xp_ref[...]), 0.0), norm(x_ref[...]),
                          jnp.where(has_next, norm(xn_ref[...]), 0.0)], axis=0).astype(BF16)

    acc = jnp.zeros((TM, x_ref.shape[1]), F32)
    for j in range(ncb):
        ug_ref[...] = _dot(hb, wup_ref[:, j * cb:(j + 1) * cb])
        uv_ref[...] = _dot(hb, wup_ref[:, ffn + j * cb:ffn + (j + 1) * cb])

        def conv(u_ref, lo):
            w = cw_ref[:, lo:lo + cb]
            return (u_ref[halo - 1:halo - 1 + TM, :] * w[0:1] + u_ref[halo:halo + TM, :] * w[1:2]
                    + u_ref[halo + 1:halo + 1 + TM, :] * w[2:3] + cb_ref[:, lo:lo + cb])

        gt = conv(ug_ref, j * cb)
        val = conv(uv_ref, ffn + j * cb)
        act = (gt * jax.nn.sigmoid(gt) * val).astype(BF16)
        acc = acc + _dot(act, wdn_ref[j * cb:(j + 1) * cb, :])
    y = x_ref[...] + mod[5:6] * acc
    o_ref[...] = _rms(y, fw_ref[...]) if final else y


def _ffn(X, mod, nw, layer, wup, cw, cb, wdn, fw, final, ncb=2):
    B, T, D = X.shape
    NT = T // TM
    t0 = 1 if final else 0
    halo = SUBLANES
    per = TM // halo
    last = T // halo - 1
    ffn = wdn.shape[1]
    cbw = ffn // ncb
    vmem = 3 * D * ffn * 2 + 2 * (TM + 2 * halo) * cbw * 4 + 8 * TM * D * 4 + (16 << 20)
    return pl.pallas_call(
        functools.partial(_ffn_kernel, nt=NT, ncb=ncb, t0=t0, final=final),
        grid=(B, NT - t0),
        in_specs=[pl.BlockSpec((None, TM, D), lambda b, t: (b, t + t0, 0)),
                  pl.BlockSpec((None, halo, D), lambda b, t: (b, jnp.maximum((t + t0) * per - 1, 0), 0)),
                  pl.BlockSpec((None, halo, D), lambda b, t: (b, jnp.minimum((t + t0 + 1) * per, last), 0)),
                  pl.BlockSpec((None, 6, D), lambda b, t: _mod_index(b, t + t0)),
                  _const_spec(nw.shape), _layer_spec(wup, layer), _const_spec(cw.shape), _const_spec(cb.shape),
                  _layer_spec(wdn, layer), _const_spec(fw.shape)],
        out_specs=pl.BlockSpec((None, TM, D), lambda b, t: (b, t, 0)),
        out_shape=jax.ShapeDtypeStruct((B, T - t0 * TM, D), F32),
        scratch_shapes=[pltpu.VMEM((TM + 2 * halo, cbw), F32), pltpu.VMEM((TM + 2 * halo, cbw), F32)],
        compiler_params=pltpu.CompilerParams(
            dimension_semantics=("parallel", "parallel"), vmem_limit_bytes=_vmem_limit(vmem)),
    )(X, X, X, mod, nw, wup, cw, cb, wdn, fw)


def _rope_tables(seq, lc):
    rows = seq // GRID_W
    row = jnp.broadcast_to(jnp.arange(rows)[:, None], (rows, GRID_W)).reshape(-1).astype(F32)
    col = jnp.broadcast_to(jnp.arange(GRID_W)[None, :], (rows, GRID_W)).reshape(-1).astype(F32)

    def cs(rot_dim):
        n_freq = rot_dim // 4
        inv = ROPE_BASE ** (-jnp.arange(n_freq, dtype=F32) / n_freq)
        ang = jnp.concatenate([row[:, None] * inv, col[:, None] * inv], axis=-1)
        return jnp.cos(ang), jnp.sin(ang)

    ca, sa = cs(A_HEAD_DIM)
    cb, sb = cs(B_ROPE_DIM)
    reps = LANES // A_HEAD_DIM
    cos_a = jnp.tile(jnp.concatenate([ca, ca], axis=1), (1, reps))
    sin_a = jnp.tile(jnp.concatenate([-sa, sa], axis=1), (1, reps))
    one = jnp.ones((seq, B_NOPE_DIM), F32)
    pad = LANES - B_NOPE_DIM - B_ROPE_DIM
    cos_b = jnp.concatenate([one, cb, cb, jnp.ones((seq, pad), F32)], axis=1)
    sin_b = jnp.concatenate([0 * one, -sb, sb, jnp.zeros((seq, pad), F32)], axis=1)
    ident = jnp.stack([jnp.ones((lc, LANES), F32), jnp.zeros((lc, LANES), F32)] * 2)
    return jnp.concatenate([ident, jnp.stack([cos_a, sin_a, cos_b, sin_b])], axis=1)


def _prep_w_in(w_in):
    o_kr = _C_BKR
    o_cq = o_kr + B_ROPE_DIM
    o_cg = o_cq + (_C_CG - _C_CQ)
    o_gate = o_cg + 2 * C_GATE_RANK
    kr = jnp.pad(w_in[:, :, o_kr:o_cq], ((0, 0), (0, 0), (B_NOPE_DIM, LANES - B_NOPE_DIM - B_ROPE_DIM)))
    cg = jnp.pad(w_in[:, :, o_cg:o_gate], ((0, 0), (0, 0), (0, LANES - 2 * C_GATE_RANK)))
    return tuple(p.astype(BF16) for p in (w_in[:, :, :o_kr], kr, w_in[:, :, o_cq:o_cg], cg, w_in[:, :, o_gate:]))


def kernel(x, c, ctx, c_ctx, w_mod, b_mod, norm_mix, norm_ffn, w_in, a_sink, b_q_norm, b_kv_norm, b_w_uq, b_w_ukv, c_w_gate, c_b_gate, c_head_norm, w_br_a, w_br_b, w_br_c, w_out, w_up, conv_w, conv_b, w_down, final_norm):
    B, S, D = x.shape
    lc = ctx.shape[1]
    L = w_mod.shape[0]
    assert lc == TM and S % TM == 0 and S % GRID_W == 0 and B + 1 <= SUBLANES

    Xc, Xl = ctx, x
    cvec = jnp.zeros((SUBLANES, D), F32).at[0].set(c_ctx).at[1:B + 1].set(c)
    mods = _modulation(cvec, w_mod, b_mod).reshape(L, SUBLANES, 6, D)

    rope = _rope_tables(S, lc)
    w_in_r = _prep_w_in(w_in)
    sink = jnp.broadcast_to(a_sink[:, :, None], (L, A_HEADS, LANES)).astype(F32)
    wa, wb, wc, wo = w_br_a.astype(BF16), w_br_b.astype(BF16), w_br_c.astype(BF16), w_out.astype(BF16)
    qk = B_NOPE_DIM + B_ROPE_DIM
    wuq = jnp.pad(b_w_uq.reshape(L, B_Q_RANK, B_HEADS, qk),
                  ((0, 0), (0, 0), (0, 0), (0, B_HEAD_PAD - qk))).reshape(L, B_Q_RANK, -1).astype(BF16)
    ukv = b_w_ukv.reshape(L, B_KV_RANK, B_HEADS, B_NOPE_DIM + B_V_DIM)
    wkn = jnp.pad(ukv[..., :B_NOPE_DIM],
                  ((0, 0), (0, 0), (0, 0), (0, B_HEAD_PAD - B_NOPE_DIM))).reshape(L, B_KV_RANK, -1).astype(BF16)
    wvb = ukv[..., B_NOPE_DIM:].reshape(L, B_KV_RANK, -1).astype(BF16)
    nqk = C_HEADS * C_KEY_DIM
    wg = jnp.zeros((L, LANES, 2 * nqk), F32)
    wg = wg.at[:, 0:C_GATE_RANK, 0:nqk].set(c_w_gate[:, 0]).at[:, C_GATE_RANK:2 * C_GATE_RANK, nqk:].set(c_w_gate[:, 1])
    bg = c_b_gate.reshape(L, 1, 2 * nqk)
    wup, wdn = w_up.astype(BF16), w_down.astype(BF16)

    for l in range(L):
        mod = mods[l]
        (qa, ka, va, qb, kb, vb, cq, ck, cv, cr, g, gates) = _inproj(
            Xc, Xl, mod, norm_mix[l][None], l, w_in_r, rope, b_q_norm[l][None], b_kv_norm[l][None],
            wuq, wkn, wvb, wg, bg[l])
        ya = _attn_a(qa, ka, va, sink[l], lc)
        yb = _mla(qb, kb, vb, lc)
        of, orv = _gla(cq, ck, cv, g)
        X = _merge(Xc, Xl, mod, ya, yb, of, orv, cr, gates, c_head_norm[l][None], l, wa, wb, wc, wo)
        X = _ffn(X, mod, norm_ffn[l][None], l, wup, conv_w[l], conv_b[l][None], wdn, final_norm[None],
                 final=(l == L - 1))
        Xc = Xl = X
    return X
```

```python
import functools

import jax
import jax.numpy as jnp
from jax import lax
from jax.experimental import pallas as pl
from jax.experimental.pallas import tpu as pltpu

GRID_W = 64
EPS = 1e-6
ROPE_BASE = 10000.0
A_HEADS, A_KV_HEADS, A_HEAD_DIM, A_WINDOW = 8, 2, 64, 128
B_HEADS, B_Q_RANK, B_KV_RANK, B_NOPE_DIM, B_ROPE_DIM, B_V_DIM = 8, 256, 128, 64, 32, 64
C_HEADS, C_KEY_DIM, C_VAL_DIM, C_GATE_RANK, C_GATE_TAU, C_CHUNK = 4, 64, 128, 16, 16.0, 64
CONV_W = 3

LANES = 128
SUBLANES = 8
VMEM_BYTES = 64 * 1024 * 1024

TM = 256
GLA_SUB = 16
MLA_TK = 256
MOD_COLS = 3072
FFN_COLS = 256
F32 = jnp.float32
BF16 = jnp.bfloat16
_HI = lax.Precision.HIGHEST
_NEG_INF = float("-inf")


def _dot(a, b):
    return jnp.dot(a, b, preferred_element_type=F32)


def _dot_nt(a, b):
    return lax.dot_general(a, b, (((1,), (1,)), ((), ())), preferred_element_type=F32)


def _split_bf16(x, terms):
    parts = []
    for _ in range(terms):
        p = x.astype(BF16)
        parts.append(p)
        x = x - p.astype(F32)
    return parts


def _dot_split(a, b):
    a_hi, a_lo = _split_bf16(a, 2)
    b_hi, b_lo = _split_bf16(b, 2)
    return _dot(a_hi, b_hi) + (_dot(a_lo, b_hi) + _dot(a_hi, b_lo))


def _rms(x, w):
    return x * lax.rsqrt(jnp.mean(x * x, axis=-1, keepdims=True) + EPS) * w


def _swap_lane_groups(x, half):
    lane = lax.broadcasted_iota(jnp.int32, x.shape, 1)
    up = pltpu.roll(x, LANES - half, 1)
    down = pltpu.roll(x, half, 1)
    return jnp.where((lane & half) == 0, up, down)


def _vmem_limit(nbytes):
    return int(min(VMEM_BYTES - (4 << 20), max(nbytes, 32 << 20)))


def _layer_spec(stacked, layer):
    shape = stacked.shape[1:]
    nd = len(shape)
    return pl.BlockSpec((None,) + shape, lambda *_: (layer,) + (0,) * nd, pipeline_mode=pl.Buffered(1))


def _const_spec(shape):
    nd = len(shape)
    return pl.BlockSpec(shape, lambda *_: (0,) * nd, pipeline_mode=pl.Buffered(1))


def _mod_kernel(c_ref, w_ref, b_ref, o_ref):
    cv = c_ref[...]
    sc = cv * jax.nn.sigmoid(cv)
    o_ref[...] = jnp.dot(sc, w_ref[...], precision=_HI, preferred_element_type=F32) + b_ref[...]


def _modulation(cvec, w_mod, b_mod):
    L, D, N = w_mod.shape
    nb = N // MOD_COLS
    return pl.pallas_call(
        _mod_kernel,
        grid=(L, nb),
        in_specs=[
            pl.BlockSpec((SUBLANES, D), lambda l, j: (0, 0)),
            pl.BlockSpec((None, D, MOD_COLS), lambda l, j: (l, 0, j)),
            pl.BlockSpec((None, 1, MOD_COLS), lambda l, j: (l, 0, j)),
        ],
        out_specs=pl.BlockSpec((None, SUBLANES, MOD_COLS), lambda l, j: (l, 0, j)),
        out_shape=jax.ShapeDtypeStruct((L, SUBLANES, N), F32),
        compiler_params=pltpu.CompilerParams(
            dimension_semantics=("parallel", "parallel"),
            vmem_limit_bytes=_vmem_limit(3 * D * MOD_COLS * 4)),
    )(cvec, w_mod, b_mod.reshape(L, 1, N))


_C_AQ = 0
_C_AK = _C_AQ + A_HEADS * A_HEAD_DIM
_C_AV = _C_AK + A_KV_HEADS * A_HEAD_DIM
_C_BQ = _C_AV + A_KV_HEADS * A_HEAD_DIM
_C_BKV = _C_BQ + B_Q_RANK
_C_BKR = _C_BKV + B_KV_RANK
_C_CQ = _C_BKR + LANES
_C_CK = _C_CQ + C_HEADS * C_KEY_DIM
_C_CV = _C_CK + C_HEADS * C_KEY_DIM
_C_CR = _C_CV + C_HEADS * C_VAL_DIM
_C_CG = _C_CR + C_HEADS * C_VAL_DIM
_C_GATE = _C_CG + LANES
B_HEAD_PAD = LANES
B_V_ROWS = B_V_DIM + 2 * SUBLANES
A_V_ROWS = A_HEAD_DIM + 2 * SUBLANES
LOG2E = 1.4426950408889634


def _stream_specs(xc, xl):
    off = 0 if xc is xl else 1
    d = xl.shape[2]
    spec_c = pl.BlockSpec((None, TM, d), lambda b, t: (b, 0, 0))
    spec_l = pl.BlockSpec((None, TM, d), lambda b, t: (b, jnp.maximum(t - off, 0), 0))
    return spec_c, spec_l, xl.shape[1] + off * TM


def _stream_tile(xc_ref, xl_ref):
    return jnp.where(pl.program_id(1) == 0, xc_ref[...], xl_ref[...])


def _inproj_kernel(xc_ref, xl_ref, mod_ref, nw_ref, w1_ref, wkr_ref, w2_ref, wcg_ref, w3_ref, rope_ref,
                   bqn_ref, bkvn_ref, wuq_ref, wkn_ref,
                   wvb_ref, wg_ref, bg_ref,
                   qa_ref, ka_ref, va_ref, qb_ref, kb_ref, vb_ref, cq_ref, ck_ref, cv_ref, cr_ref,
                   g_ref, gate_ref, *, d_model):
    mod = mod_ref[...]
    h = _rms(_stream_tile(xc_ref, xl_ref), nw_ref[...]) * (1.0 + mod[1:2]) + mod[0:1]
    hb = h.astype(BF16)

    pieces = ((_C_AQ, w1_ref), (_C_BKR, wkr_ref), (_C_CQ, w2_ref), (_C_CG, wcg_ref), (_C_GATE, w3_ref))

    def proj(lo, hi):
        base, ref = [(b0, r) for b0, r in pieces if b0 <= lo][-1]
        assert hi - base <= ref.shape[1]
        return _dot(hb, ref[:, lo - base:hi - base])

    cos_a, sin_a, cos_b, sin_b = rope_ref[0], rope_ref[1], rope_ref[2], rope_ref[3]

    def rope_a(t):
        return t * cos_a + _swap_lane_groups(t, A_HEAD_DIM // 2) * sin_a

    def rope_b(t):
        return t * cos_b + _swap_lane_groups(t, B_ROPE_DIM // 2) * sin_b

    a_scale = A_HEAD_DIM ** -0.5 * LOG2E
    for c in range(A_HEADS * A_HEAD_DIM // LANES):
        t = proj(_C_AQ + c * LANES, _C_AQ + (c + 1) * LANES)
        qa_ref[c * LANES:(c + 1) * LANES, :] = (rope_a(t) * a_scale).T.astype(BF16)
    ka_ref[...] = rope_a(proj(_C_AK, _C_AV)).astype(BF16)
    va_t = proj(_C_AV, _C_BQ).T.astype(BF16)
    a_ones = jnp.where(lax.broadcasted_iota(jnp.int32, (A_V_ROWS - A_HEAD_DIM, va_t.shape[1]), 0) == 0,
                       1.0, 0.0).astype(BF16)
    for g in range(A_KV_HEADS):
        va_ref[g * A_V_ROWS:g * A_V_ROWS + A_HEAD_DIM, :] = va_t[g * A_HEAD_DIM:(g + 1) * A_HEAD_DIM]
        va_ref[g * A_V_ROWS + A_HEAD_DIM:(g + 1) * A_V_ROWS, :] = a_ones

    b_scale = (B_NOPE_DIM + B_ROPE_DIM) ** -0.5 * LOG2E
    cqn = _rms(proj(_C_BQ, _C_BKV), bqn_ref[...]).astype(BF16)
    ckvn = _rms(proj(_C_BKV, _C_BKR), bkvn_ref[...]).astype(BF16)
    kr = rope_b(proj(_C_BKR, _C_CQ))
    for hd in range(B_HEADS):
        sl = slice(hd * B_HEAD_PAD, (hd + 1) * B_HEAD_PAD)
        qb_ref[sl, :] = (rope_b(_dot(cqn, wuq_ref[:, sl])) * b_scale).T.astype(BF16)
        kb_ref[:, sl] = (_dot(ckvn, wkn_ref[:, sl]) + kr).astype(BF16)
    vb = _dot(ckvn, wvb_ref[...])
    tk = vb_ref.shape[2]
    per = LANES // B_V_DIM
    ones_rows = jnp.where(lax.broadcasted_iota(jnp.int32, (B_V_ROWS - B_V_DIM, tk), 0) == 0, 1.0, 0.0).astype(BF16)
    for cc in range(vb_ref.shape[0]):
        for c in range(vb.shape[1] // LANES):
            v_t = vb[cc * tk:(cc + 1) * tk, c * LANES:(c + 1) * LANES].T.astype(BF16)
            for i in range(per):
                r0 = (c * per + i) * B_V_ROWS
                vb_ref[cc, r0:r0 + B_V_DIM, :] = v_t[i * B_V_DIM:(i + 1) * B_V_DIM]
                vb_ref[cc, r0 + B_V_DIM:r0 + B_V_ROWS, :] = ones_rows

    cq_ref[...] = proj(_C_CQ, _C_CK) * (C_KEY_DIM ** -0.5)
    ck_ref[...] = proj(_C_CK, _C_CV)
    cv_ref[...] = proj(_C_CV, _C_CR).astype(BF16)
    cr_ref[...] = proj(_C_CR, _C_CG).astype(BF16)
    z = _dot_split(proj(_C_CG, _C_GATE), wg_ref[...]) + bg_ref[...]
    g_ref[...] = (jnp.minimum(z, 0.0) - jnp.log1p(jnp.exp(-jnp.abs(z)))) * (1.0 / C_GATE_TAU)

    gate_ref[...] = proj(_C_GATE, _C_GATE + 3 * d_model).astype(BF16)

def _mod_index(b, t):
    return (jnp.where(t == 0, 0, b + 1), 0, 0)


def _inproj(Xc, Xl, mod, nw, layer, w_pieces, rope, bqn, bkvn, wuq, wkn, wvb, wg, bg):
    spec_c, spec_l, T = _stream_specs(Xc, Xl)
    B, _, D = Xl.shape
    NT = T // TM
    n_in = sum(w.shape[2] for w in w_pieces)
    row = lambda n: pl.BlockSpec((None, TM, n), lambda b, t: (b, t, 0))
    widths = [(A_HEADS * A_HEAD_DIM, BF16), (A_KV_HEADS * A_HEAD_DIM, BF16), (A_KV_HEADS * A_HEAD_DIM, BF16),
              (B_HEADS * B_HEAD_PAD, BF16), (B_HEADS * B_HEAD_PAD, BF16), (B_HEADS * B_V_DIM, BF16),
              (C_HEADS * C_KEY_DIM, F32), (C_HEADS * C_KEY_DIM, F32), (C_HEADS * C_VAL_DIM, BF16),
              (C_HEADS * C_VAL_DIM, BF16), (2 * C_HEADS * C_KEY_DIM, F32), (3 * D, BF16)]
    out_bytes = sum(TM * n * jnp.dtype(dt).itemsize for n, dt in widths)
    vmem = D * n_in * 2 + 4 * out_bytes + 6 * TM * D * 4 + (8 << 20)
    out_specs = [row(n) for n, _ in widths]
    out_shape = [jax.ShapeDtypeStruct((B, T, n), dt) for n, dt in widths]
    qw, vw = B_HEADS * B_HEAD_PAD, B_HEADS * B_V_ROWS
    out_specs[3] = pl.BlockSpec((None, qw, TM), lambda b, t: (b, 0, t))
    out_shape[3] = jax.ShapeDtypeStruct((B, qw, T), BF16)
    out_specs[0] = pl.BlockSpec((None, widths[0][0], TM), lambda b, t: (b, 0, t))
    out_shape[0] = jax.ShapeDtypeStruct((B, widths[0][0], T), BF16)
    out_specs[2] = pl.BlockSpec((None, A_KV_HEADS * A_V_ROWS, TM), lambda b, t: (b, 0, t))
    out_shape[2] = jax.ShapeDtypeStruct((B, A_KV_HEADS * A_V_ROWS, T), BF16)
    out_specs[5] = pl.BlockSpec((None, TM // MLA_TK, vw, MLA_TK), lambda b, t: (b, t, 0, 0))
    out_shape[5] = jax.ShapeDtypeStruct((B, T // MLA_TK, vw, MLA_TK), BF16)
    return pl.pallas_call(
        functools.partial(_inproj_kernel, d_model=D),
        grid=(B, NT),
        in_specs=[
            spec_c, spec_l,
            pl.BlockSpec((None, 6, D), _mod_index),
            _const_spec((1, D)),
            *[_layer_spec(w, layer) for w in w_pieces],
            pl.BlockSpec((4, TM, LANES), lambda b, t: (0, t, 0)),
            _const_spec(bqn.shape), _const_spec(bkvn.shape), _layer_spec(wuq, layer),
            _layer_spec(wkn, layer), _layer_spec(wvb, layer), _layer_spec(wg, layer), _const_spec(bg.shape),
        ],
        out_specs=out_specs,
        out_shape=out_shape,
        compiler_params=pltpu.CompilerParams(
            dimension_semantics=("parallel", "parallel"), vmem_limit_bytes=_vmem_limit(vmem)),
    )(Xc, Xl, mod, nw, *w_pieces, rope, bqn, bkvn, wuq, wkn, wvb, wg, bg)


def _attn_a_kernel(q_ref, kp_ref, km_ref, kn_ref, kc_ref, vp_ref, vm_ref, vn_ref, vc_ref, sink_ref,
                   o_ref, *, seq):
    t = pl.program_id(1)
    kloc = jnp.concatenate([kp_ref[...], km_ref[...], kn_ref[...]], axis=0)
    vloc = jnp.concatenate([vp_ref[...], vm_ref[...], vn_ref[...]], axis=1)
    kctx, vctx = kc_ref[...], vc_ref[...]
    nloc = kloc.shape[0]
    r = lax.broadcasted_iota(jnp.int32, (nloc, TM), 0)
    c = lax.broadcasted_iota(jnp.int32, (nloc, TM), 1)
    rel = r - A_WINDOW - c
    kpos = (t - 1) * TM - A_WINDOW + r
    valid = (jnp.abs(rel) <= A_WINDOW) & (kpos >= 0) & (kpos < seq) & (t >= 1)
    group = A_HEADS // A_KV_HEADS
    per = LANES // A_HEAD_DIM
    q_zero = jnp.zeros((A_HEAD_DIM, TM), BF16)

    def scores(hd):
        q_t = q_ref[hd * A_HEAD_DIM:(hd + 1) * A_HEAD_DIM, :]
        g = hd // group
        qm = jnp.concatenate([q_t if i == g else q_zero for i in range(A_KV_HEADS)], axis=0)
        return _dot(kloc, qm), _dot(kctx, qm)

    cur = scores(0)
    parts = []
    for hd in range(A_HEADS):
        nxt = scores(hd + 1) if hd + 1 < A_HEADS else None
        s_loc = jnp.where(valid, cur[0], _NEG_INF)
        s_ctx = cur[1]
        sink = sink_ref[hd:hd + 1, 0:1] * LOG2E
        m = jnp.maximum(jnp.maximum(jnp.max(s_loc, axis=0, keepdims=True),
                                    jnp.max(s_ctx, axis=0, keepdims=True)), sink)
        p_loc = jnp.exp2((s_loc - m).astype(BF16))
        p_ctx = jnp.exp2((s_ctx - m).astype(BF16))
        g = hd // group
        vs = slice(g * A_V_ROWS, (g + 1) * A_V_ROWS)
        o_t = _dot(vloc[vs], p_loc) + _dot(vctx[vs], p_ctx)
        denom = o_t[A_HEAD_DIM:A_HEAD_DIM + 1] + jnp.exp2(sink - m)
        parts.append(o_t[0:A_HEAD_DIM] / denom)
        if len(parts) == per:
            c = hd // per
            o_ref[:, c * LANES:(c + 1) * LANES] = jnp.concatenate(parts, axis=0).T.astype(BF16)
            parts = []
        cur = nxt


def _attn_a(qa, ka, va, sink, lc):
    B, qw, T = qa.shape
    NT = T // TM
    seq = T - lc
    kvw = ka.shape[2]
    vrows = va.shape[1]
    per_tm = TM // A_WINDOW
    last = T // A_WINDOW - 1
    prev_i = lambda t: jnp.maximum(t * per_tm - 1, 0)
    next_i = lambda t: jnp.minimum((t + 1) * per_tm, last)
    kprev = pl.BlockSpec((None, A_WINDOW, kvw), lambda b, t: (b, prev_i(t), 0))
    kmain = pl.BlockSpec((None, TM, kvw), lambda b, t: (b, t, 0))
    knext = pl.BlockSpec((None, A_WINDOW, kvw), lambda b, t: (b, next_i(t), 0))
    kctx = pl.BlockSpec((None, lc, kvw), lambda b, t: (b, 0, 0))
    vprev = pl.BlockSpec((None, vrows, A_WINDOW), lambda b, t: (b, 0, prev_i(t)))
    vmain = pl.BlockSpec((None, vrows, TM), lambda b, t: (b, 0, t))
    vnext = pl.BlockSpec((None, vrows, A_WINDOW), lambda b, t: (b, 0, next_i(t)))
    vctx = pl.BlockSpec((None, vrows, lc), lambda b, t: (b, 0, 0))
    return pl.pallas_call(
        functools.partial(_attn_a_kernel, seq=seq),
        grid=(B, NT),
        in_specs=[pl.BlockSpec((None, qw, TM), lambda b, t: (b, 0, t)),
                  kprev, kmain, knext, kctx, vprev, vmain, vnext, vctx,
                  pl.BlockSpec(sink.shape, lambda b, t: (0, 0))],
        out_specs=pl.BlockSpec((None, TM, qw), lambda b, t: (b, t, 0)),
        out_shape=jax.ShapeDtypeStruct((B, T, qw), BF16),
        compiler_params=pltpu.CompilerParams(dimension_semantics=("parallel", "parallel")),
    )(qa, ka, ka, ka, ka, va, va, va, va, sink)


def _mla_kernel(q_ref, k_ref, v_ref, o_ref, m_ref, acc_ref, sa_ref, sb_ref, *, lc, tk):
    t = pl.program_id(1)
    total = k_ref.shape[0]
    nchunks = jnp.where(t == 0, lc // tk, total // tk)
    m_ref[...] = jnp.full(m_ref.shape, _NEG_INF, F32)
    acc_ref[...] = jnp.zeros(acc_ref.shape, F32)

    def stage(dst_ref, j, hd):
        qs = slice(hd * B_HEAD_PAD, (hd + 1) * B_HEAD_PAD)
        s = _dot(k_ref[pl.ds(pl.multiple_of(j * tk, tk), tk), qs], q_ref[qs, :])
        dst_ref[hd, 0:tk, :] = s
        dst_ref[hd, tk:tk + SUBLANES, :] = jnp.broadcast_to(jnp.max(s, axis=0, keepdims=True), (SUBLANES, TM))

    for hd in range(B_HEADS):
        stage(sa_ref, 0, hd)

    def step(j, cur_ref, nxt_ref, last=False):
        for hd in range(B_HEADS):
            vs = slice(hd * B_V_ROWS, (hd + 1) * B_V_ROWS)
            if not last:
                stage(nxt_ref, j + 1, hd)
            m_prev = m_ref[hd]
            m_new = jnp.maximum(m_prev, cur_ref[hd, tk:tk + SUBLANES, :])
            alpha = jnp.exp2(m_prev - m_new)
            p = jnp.exp2((cur_ref[hd, 0:tk, :] - m_new[0:1]).astype(BF16))
            m_ref[hd] = m_new
            acc_ref[vs, :] = acc_ref[vs, :] * alpha[0:1] + _dot(v_ref[j, vs, :], p)

    def body(j2, carry):
        step(2 * j2, sa_ref, sb_ref)
        step(2 * j2 + 1, sb_ref, sa_ref)
        return carry

    npairs = (nchunks - 1) // 2
    lax.fori_loop(0, npairs, body, 0)
    assert (lc // tk) % 2 == (total // tk) % 2
    if (total // tk) % 2 == 0:
        step(2 * npairs, sa_ref, sb_ref)
        step(2 * npairs + 1, sb_ref, sa_ref, last=True)
    else:
        step(2 * npairs, sa_ref, sb_ref, last=True)
    per = LANES // B_V_DIM
    for c in range(B_HEADS // per):
        o_t = jnp.concatenate(
            [acc_ref[hd * B_V_ROWS:hd * B_V_ROWS + B_V_DIM, :]
             / acc_ref[hd * B_V_ROWS + B_V_DIM:hd * B_V_ROWS + B_V_DIM + 1, :]
             for hd in range(c * per, (c + 1) * per)], axis=0)
        o_ref[:, c * LANES:(c + 1) * LANES] = o_t.T.astype(BF16)


def _mla(qb, kb, vb, lc):
    B, qw, T = qb.shape
    NT = T // TM
    _, nck, vw, tk = vb.shape
    ow = B_HEADS * B_V_DIM
    vmem = (T * (qw + vw) * 2 + 4 * TM * qw * 2 + (B_HEADS * SUBLANES * TM + 3 * TM * vw) * 4
            + 2 * B_HEADS * (tk + SUBLANES) * TM * 4 + (16 << 20))
    return pl.pallas_call(
        functools.partial(_mla_kernel, lc=lc, tk=tk),
        grid=(B, NT),
        in_specs=[pl.BlockSpec((None, qw, TM), lambda b, t: (b, 0, t)),
                  pl.BlockSpec((None, T, qw), lambda b, t: (b, 0, 0), pipeline_mode=pl.Buffered(1)),
                  pl.BlockSpec((None, nck, vw, tk), lambda b, t: (b, 0, 0, 0), pipeline_mode=pl.Buffered(1))],
        out_specs=pl.BlockSpec((None, TM, ow), lambda b, t: (b, t, 0)),
        out_shape=jax.ShapeDtypeStruct((B, T, ow), BF16),
        scratch_shapes=[pltpu.VMEM((B_HEADS, SUBLANES, TM), F32),
                        pltpu.VMEM((vw, TM), F32),
                        pltpu.VMEM((B_HEADS, tk + SUBLANES, TM), F32),
                        pltpu.VMEM((B_HEADS, tk + SUBLANES, TM), F32)],
        compiler_params=pltpu.CompilerParams(
            dimension_semantics=("parallel", "arbitrary"), vmem_limit_bytes=_vmem_limit(vmem)),
    )(qb, kb, vb)


def _gla_prep(q, k, b, reverse):
    C = q.shape[0]
    dk2 = q.shape[1]
    nsub = C // GLA_SUB
    b_end = b[0:1] if reverse else b[C - 1:C]
    row = lax.broadcasted_iota(jnp.int32, (C, dk2), 0)
    lane = lax.broadcasted_iota(jnp.int32, (C, dk2), 1)

    qe = (q * jnp.exp(b)).astype(BF16)
    kd_t = (k * jnp.exp(b_end - b)).T.astype(BF16)
    dec = jnp.exp(jnp.broadcast_to(b_end, (dk2, dk2)).T)
    dec2 = jnp.concatenate([dec] * (2 * C_VAL_DIM // dk2), axis=1)

    refs = [blk * GLA_SUB + (GLA_SUB - 1 if reverse else 0) for blk in range(nsub)]
    b_ref = jnp.concatenate([jnp.broadcast_to(b[n:n + 1], (GLA_SUB, dk2)) for n in refs], axis=0)
    q_sc = q * jnp.exp(b - b_ref)
    k_parts = []
    for blk, n in enumerate(refs):
        in_range = (row >= blk * GLA_SUB) if reverse else (row < (blk + 1) * GLA_SUB)
        k_parts.append(jnp.where(in_range, k * jnp.exp(b[n:n + 1] - b), 0.0))
    k_big = jnp.concatenate(k_parts, axis=1).astype(BF16)
    q_halves = []
    for hf in range(2):
        in_half = (lane >= hf * C_KEY_DIM) & (lane < (hf + 1) * C_KEY_DIM)
        q_halves.append(jnp.concatenate(
            [jnp.where(in_half & (row >= blk * GLA_SUB) & (row < (blk + 1) * GLA_SUB), q_sc, 0.0)
             for blk in range(nsub)], axis=1))
    q_big = jnp.concatenate(q_halves, axis=0).astype(BF16)
    return qe, kd_t, dec2, q_big, k_big


def _gla_kernel(qf_ref, kf_ref, vf_ref, gf_ref, qr_ref, kr_ref, vr_ref, gr_ref, of_ref, or_ref,
                sf_ref, sr_ref):
    t = pl.program_id(1)

    @pl.when(t == 0)
    def _():
        sf_ref[...] = jnp.zeros(sf_ref.shape, F32)
        sr_ref[...] = jnp.zeros(sr_ref.shape, F32)

    C = C_CHUNK
    nchunk = TM // C
    npair = C_HEADS // 2
    kw, vw = 2 * C_KEY_DIM, 2 * C_VAL_DIM
    dirs = ((qf_ref, kf_ref, vf_ref, gf_ref, of_ref, sf_ref, False),
            (qr_ref, kr_ref, vr_ref, gr_ref, or_ref, sr_ref, True))

    ii = lax.broadcasted_iota(jnp.int32, (TM, TM), 0)
    jj = lax.broadcasted_iota(jnp.int32, (TM, TM), 1)
    same_chunk = (ii // C) == (jj // C)
    b_all = []
    for (_, _, _, g_ref, _, _, reverse) in dirs:
        tri = jnp.where(same_chunk & ((jj >= ii) if reverse else (jj <= ii)), 1.0, 0.0).astype(BF16)
        g_hi, g_mid, g_lo = _split_bf16(g_ref[...], 3)
        b_all.append(_dot(tri, g_hi) + (_dot(tri, g_mid) + _dot(tri, g_lo)))

    units = []
    for d, (q_ref, k_ref, v_ref, _, o_ref, _, reverse) in enumerate(dirs):
        for p in range(npair):
            ks, vsl = slice(p * kw, (p + 1) * kw), slice(p * vw, (p + 1) * vw)
            for c in range(nchunk):
                rows = slice(c * C, (c + 1) * C)
                prep = _gla_prep(q_ref[rows, ks], k_ref[rows, ks], b_all[d][rows, ks], reverse)
                units.append((d, p, c, rows, vsl, v_ref, o_ref, reverse, prep))

    srow = lax.broadcasted_iota(jnp.int32, (kw, vw), 0)
    scol = lax.broadcasted_iota(jnp.int32, (kw, vw), 1)
    on_diag = (srow < C_KEY_DIM) == (scol < C_VAL_DIM)
    qi = lax.broadcasted_iota(jnp.int32, (2 * C, C), 0) % C
    kj = lax.broadcasted_iota(jnp.int32, (2 * C, C), 1)
    kvs, atts = [], []
    for (d, p, c, rows, vsl, v_ref, o_ref, reverse, prep) in units:
        _, kd_t, _, q_big, k_big = prep
        kvs.append(jnp.where(on_diag, _dot(kd_t, v_ref[rows, vsl]), 0.0))
        keep = (kj >= qi) if reverse else (kj <= qi)
        atts.append(jnp.where(keep, _dot_nt(q_big, k_big), 0.0).astype(BF16))

    intra = {}
    for u, (d, p, c, rows, vsl, v_ref, o_ref, reverse, prep) in enumerate(units):
        pv = _dot(atts[u], v_ref[rows, vsl])
        intra[(d, p, c)] = (u, jnp.concatenate([pv[0:C, 0:C_VAL_DIM], pv[C:2 * C, C_VAL_DIM:]], axis=1))

    states = {(d, p): dirs[d][5][p] for d in range(len(dirs)) for p in range(npair)}
    for step in range(nchunk):
        for d in range(len(dirs)):
            reverse = dirs[d][6]
            c = nchunk - 1 - step if reverse else step
            for p in range(npair):
                u, o_intra = intra[(d, p, c)]
                _, _, _, rows, vsl, _, o_ref, _, prep = units[u]
                qe, _, dec2, _, _ = prep
                o_ref[rows, vsl] = o_intra + _dot(qe, states[(d, p)].astype(BF16))
                states[(d, p)] = dec2 * states[(d, p)] + kvs[u]
    for d in range(len(dirs)):
        for p in range(npair):
            dirs[d][5][p] = states[(d, p)]


def _gla(cq, ck, cv, g):
    B, T, kw = cq.shape
    NT = T // TM
    vw = cv.shape[2]

    def fwd(n, col=0):
        return pl.BlockSpec((None, TM, n), lambda b, t: (b, t, col))

    def rev(n, col=0):
        return pl.BlockSpec((None, TM, n), lambda b, t: (b, jnp.where(t == 0, 0, NT - t), col))

    return pl.pallas_call(
        _gla_kernel,
        grid=(B, NT),
        in_specs=[fwd(kw), fwd(kw), fwd(vw), fwd(kw, 0), rev(kw), rev(kw), rev(vw), rev(kw, 1)],
        out_specs=[fwd(vw), rev(vw)],
        out_shape=[jax.ShapeDtypeStruct((B, T, vw), F32)] * 2,
        scratch_shapes=[pltpu.VMEM((C_HEADS // 2, 2 * C_KEY_DIM, 2 * C_VAL_DIM), F32)] * 2,
        compiler_params=pltpu.CompilerParams(dimension_semantics=("parallel", "arbitrary")),
    )(cq, ck, cv, g, cq, ck, cv, g)


def _merge_kernel(xc_ref, xl_ref, mod_ref, ya_ref, yb_ref, of_ref, or_ref, cr_ref, gate_ref, hn_ref,
                  wa_ref, wb_ref, wc_ref, wo_ref, o_ref):
    d = xl_ref.shape[1]
    o = of_ref[...] + or_ref[...]
    r = cr_ref[...].astype(F32)
    parts = []
    for hd in range(C_HEADS):
        sl = slice(hd * C_VAL_DIM, (hd + 1) * C_VAL_DIM)
        parts.append(_rms(o[:, sl], hn_ref[:, sl]))
    yc = (jnp.concatenate(parts, axis=1) * (r * jax.nn.sigmoid(r))).astype(BF16)

    def gate(i):
        return jax.nn.sigmoid(gate_ref[:, i * d:(i + 1) * d].astype(F32))

    m = (gate(0) * _dot(ya_ref[...], wa_ref[...]) + gate(1) * _dot(yb_ref[...], wb_ref[...])
         + gate(2) * _dot(yc, wc_ref[...]))
    y = _dot(m.astype(BF16), wo_ref[...])
    o_ref[...] = _stream_tile(xc_ref, xl_ref) + mod_ref[2:3, :] * y


def _merge(Xc, Xl, mod, ya, yb, of, orv, cr, gates, hn, layer, wa, wb, wc, wo):
    spec_c, spec_l, T = _stream_specs(Xc, Xl)
    B, _, D = Xl.shape
    NT = T // TM
    row = lambda n: pl.BlockSpec((None, TM, n), lambda b, t: (b, t, 0))
    wbytes = sum(w.size // w.shape[0] for w in (wa, wb, wc, wo)) * 2
    vmem = wbytes + 2 * TM * (2 * D + 3 * D + 4 * ya.shape[2]) * 4 + (16 << 20)
    return pl.pallas_call(
        _merge_kernel,
        grid=(B, NT),
        in_specs=[spec_c, spec_l, pl.BlockSpec((None, 6, D), _mod_index),
                  row(ya.shape[2]), row(yb.shape[2]), row(of.shape[2]), row(orv.shape[2]),
                  row(cr.shape[2]), row(gates.shape[2]), _const_spec(hn.shape),
                  _layer_spec(wa, layer), _layer_spec(wb, layer), _layer_spec(wc, layer), _layer_spec(wo, layer)],
        out_specs=row(D),
        out_shape=jax.ShapeDtypeStruct((B, T, D), F32),
        compiler_params=pltpu.CompilerParams(
            dimension_semantics=("parallel", "parallel"), vmem_limit_bytes=_vmem_limit(vmem)),
    )(Xc, Xl, mod, ya, yb, of, orv, cr, gates, hn, wa, wb, wc, wo)


def _ffn_kernel(x_ref, xp_ref, xn_ref, mod_ref, nw_ref, wup_ref, cw_ref, cb_ref, wdn_ref, fw_ref, o_ref,
                u_ref, act_ref, *, nt, ncb, t0, final):
    t = pl.program_id(1) + t0
    mod = mod_ref[...]
    nw = nw_ref[...]
    halo = xp_ref.shape[0]
    ffn = wdn_ref.shape[0]
    cb = ffn // ncb

    def norm(xv):
        return _rms(xv, nw) * (1.0 + mod[4:5]) + mod[3:4]

    has_prev = (t >= 2)
    has_next = (t >= 1) & (t < nt - 1)
    hb = jnp.concatenate([jnp.where(has_prev, norm(xp_ref[...]), 0.0), norm(x_ref[...]),
                          jnp.where(has_next, norm(xn_ref[...]), 0.0)], axis=0).astype(BF16)

    def up(j):
        u_ref[j % 2, 0] = _dot(hb, wup_ref[:, j * cb:(j + 1) * cb])
        u_ref[j % 2, 1] = _dot(hb, wup_ref[:, ffn + j * cb:ffn + (j + 1) * cb])

    def conv(slot, half, lo):
        w = cw_ref[:, lo:lo + cb]
        return (u_ref[slot, half, halo - 1:halo - 1 + TM, :] * w[0:1] + u_ref[slot, half, halo:halo + TM, :] * w[1:2]
                + u_ref[slot, half, halo + 1:halo + 1 + TM, :] * w[2:3] + cb_ref[:, lo:lo + cb])

    def act(j):
        gt = conv(j % 2, 0, j * cb)
        val = conv(j % 2, 1, ffn + j * cb)
        act_ref[:, j * cb:(j + 1) * cb] = (gt * jax.nn.sigmoid(gt) * val).astype(BF16)

    up(0)
    for j in range(1, ncb):
        up(j)
        act(j - 1)
    act(ncb - 1)
    y = x_ref[...] + mod[5:6] * _dot(act_ref[...], wdn_ref[...])
    o_ref[...] = _rms(y, fw_ref[...]) if final else y


def _ffn(X, mod, nw, layer, wup, cw, cb, wdn, fw, final):
    B, T, D = X.shape
    NT = T // TM
    t0 = 1 if final else 0
    halo = SUBLANES
    per = TM // halo
    last = T // halo - 1
    ffn = wdn.shape[1]
    cbw = FFN_COLS
    ncb = ffn // cbw
    vmem = 3 * D * ffn * 2 + 4 * (TM + 2 * halo) * cbw * 4 + TM * ffn * 2 + 8 * TM * D * 4 + (16 << 20)
    return pl.pallas_call(
        functools.partial(_ffn_kernel, nt=NT, ncb=ncb, t0=t0, final=final),
        grid=(B, NT - t0),
        in_specs=[pl.BlockSpec((None, TM, D), lambda b, t: (b, t + t0, 0)),
                  pl.BlockSpec((None, halo, D), lambda b, t: (b, jnp.maximum((t + t0) * per - 1, 0), 0)),
                  pl.BlockSpec((None, halo, D), lambda b, t: (b, jnp.minimum((t + t0 + 1) * per, last), 0)),
                  pl.BlockSpec((None, 6, D), lambda b, t: _mod_index(b, t + t0)),
                  _const_spec(nw.shape), _layer_spec(wup, layer), _const_spec(cw.shape), _const_spec(cb.shape),
                  _layer_spec(wdn, layer), _const_spec(fw.shape)],
        out_specs=pl.BlockSpec((None, TM, D), lambda b, t: (b, t, 0)),
        out_shape=jax.ShapeDtypeStruct((B, T - t0 * TM, D), F32),
        scratch_shapes=[pltpu.VMEM((2, 2, TM + 2 * halo, cbw), F32), pltpu.VMEM((TM, ffn), BF16)],
        compiler_params=pltpu.CompilerParams(
            dimension_semantics=("parallel", "parallel"), vmem_limit_bytes=_vmem_limit(vmem)),
    )(X, X, X, mod, nw, wup, cw, cb, wdn, fw)


def _rope_tables(seq, lc):
    rows = seq // GRID_W
    row = jnp.broadcast_to(jnp.arange(rows)[:, None], (rows, GRID_W)).reshape(-1).astype(F32)
    col = jnp.broadcast_to(jnp.arange(GRID_W)[None, :], (rows, GRID_W)).reshape(-1).astype(F32)

    def cs(rot_dim):
        n_freq = rot_dim // 4
        inv = ROPE_BASE ** (-jnp.arange(n_freq, dtype=F32) / n_freq)
        ang = jnp.concatenate([row[:, None] * inv, col[:, None] * inv], axis=-1)
        return jnp.cos(ang), jnp.sin(ang)

    ca, sa = cs(A_HEAD_DIM)
    cb, sb = cs(B_ROPE_DIM)
    reps = LANES // A_HEAD_DIM
    cos_a = jnp.tile(jnp.concatenate([ca, ca], axis=1), (1, reps))
    sin_a = jnp.tile(jnp.concatenate([-sa, sa], axis=1), (1, reps))
    one = jnp.ones((seq, B_NOPE_DIM), F32)
    pad = LANES - B_NOPE_DIM - B_ROPE_DIM
    cos_b = jnp.concatenate([one, cb, cb, jnp.ones((seq, pad), F32)], axis=1)
    sin_b = jnp.concatenate([0 * one, -sb, sb, jnp.zeros((seq, pad), F32)], axis=1)
    ident = jnp.stack([jnp.ones((lc, LANES), F32), jnp.zeros((lc, LANES), F32)] * 2)
    return jnp.concatenate([ident, jnp.stack([cos_a, sin_a, cos_b, sin_b])], axis=1)


def _prep_w_in(w_in):
    o_kr = _C_BKR
    o_cq = o_kr + B_ROPE_DIM
    o_cg = o_cq + (_C_CG - _C_CQ)
    o_gate = o_cg + 2 * C_GATE_RANK
    kr = jnp.pad(w_in[:, :, o_kr:o_cq], ((0, 0), (0, 0), (B_NOPE_DIM, LANES - B_NOPE_DIM - B_ROPE_DIM)))
    cg = jnp.pad(w_in[:, :, o_cg:o_gate], ((0, 0), (0, 0), (0, LANES - 2 * C_GATE_RANK)))
    return tuple(p.astype(BF16) for p in (w_in[:, :, :o_kr], kr, w_in[:, :, o_cq:o_cg], cg, w_in[:, :, o_gate:]))


def kernel(x, c, ctx, c_ctx, w_mod, b_mod, norm_mix, norm_ffn, w_in, a_sink, b_q_norm, b_kv_norm, b_w_uq, b_w_ukv, c_w_gate, c_b_gate, c_head_norm, w_br_a, w_br_b, w_br_c, w_out, w_up, conv_w, conv_b, w_down, final_norm):
    B, S, D = x.shape
    lc = ctx.shape[1]
    L = w_mod.shape[0]
    assert lc == TM and S % TM == 0 and S % GRID_W == 0 and B + 1 <= SUBLANES

    Xc, Xl = ctx, x
    cvec = jnp.zeros((SUBLANES, D), F32).at[0].set(c_ctx).at[1:B + 1].set(c)
    mods = _modulation(cvec, w_mod, b_mod).reshape(L, SUBLANES, 6, D)

    rope = _rope_tables(S, lc)
    w_in_r = _prep_w_in(w_in)
    sink = jnp.broadcast_to(a_sink[:, :, None], (L, A_HEADS, LANES)).astype(F32)
    wa, wb, wc, wo = w_br_a.astype(BF16), w_br_b.astype(BF16), w_br_c.astype(BF16), w_out.astype(BF16)
    qk = B_NOPE_DIM + B_ROPE_DIM
    wuq = jnp.pad(b_w_uq.reshape(L, B_Q_RANK, B_HEADS, qk),
                  ((0, 0), (0, 0), (0, 0), (0, B_HEAD_PAD - qk))).reshape(L, B_Q_RANK, -1).astype(BF16)
    ukv = b_w_ukv.reshape(L, B_KV_RANK, B_HEADS, B_NOPE_DIM + B_V_DIM)
    wkn = jnp.pad(ukv[..., :B_NOPE_DIM],
                  ((0, 0), (0, 0), (0, 0), (0, B_HEAD_PAD - B_NOPE_DIM))).reshape(L, B_KV_RANK, -1).astype(BF16)
    wvb = ukv[..., B_NOPE_DIM:].reshape(L, B_KV_RANK, -1).astype(BF16)
    nqk = C_HEADS * C_KEY_DIM
    wg = jnp.zeros((L, LANES, 2 * nqk), F32)
    wg = wg.at[:, 0:C_GATE_RANK, 0:nqk].set(c_w_gate[:, 0]).at[:, C_GATE_RANK:2 * C_GATE_RANK, nqk:].set(c_w_gate[:, 1])
    bg = c_b_gate.reshape(L, 1, 2 * nqk)
    wup, wdn = w_up.astype(BF16), w_down.astype(BF16)

    for l in range(L):
        mod = mods[l]
        (qa, ka, va, qb, kb, vb, cq, ck, cv, cr, g, gates) = _inproj(
            Xc, Xl, mod, norm_mix[l][None], l, w_in_r, rope, b_q_norm[l][None], b_kv_norm[l][None],
            wuq, wkn, wvb, wg, bg[l])
        ya = _attn_a(qa, ka, va, sink[l], lc)
        yb = _mla(qb, kb, vb, lc)
        of, orv = _gla(cq, ck, cv, g)
        X = _merge(Xc, Xl, mod, ya, yb, of, orv, cr, gates, c_head_norm[l][None], l, wa, wb, wc, wo)
        X = _ffn(X, mod, norm_ffn[l][None], l, wup, conv_w[l], conv_b[l][None], wdn, final_norm[None],
                 final=(l == L - 1))
        Xc = Xl = X
    return X
```

```python
import functools

import jax
import jax.numpy as jnp
from jax import lax
from jax.experimental import pallas as pl
from jax.experimental.pallas import tpu as pltpu

GRID_W = 64
EPS = 1e-6
ROPE_BASE = 10000.0
A_HEADS, A_KV_HEADS, A_HEAD_DIM, A_WINDOW = 8, 2, 64, 128
B_HEADS, B_Q_RANK, B_KV_RANK, B_NOPE_DIM, B_ROPE_DIM, B_V_DIM = 8, 256, 128, 64, 32, 64
C_HEADS, C_KEY_DIM, C_VAL_DIM, C_GATE_RANK, C_GATE_TAU, C_CHUNK = 4, 64, 128, 16, 16.0, 64
CONV_W = 3

LANES = 128
SUBLANES = 8
VMEM_BYTES = 64 * 1024 * 1024

TM = 256
GLA_SUB = 16
MLA_TK = 256
MLA_UNROLL = 8
MOD_COLS = 3072
FFN_COLS = 256
F32 = jnp.float32
BF16 = jnp.bfloat16
_HI = lax.Precision.HIGHEST
_NEG_INF = float("-inf")


def _dot(a, b):
    return jnp.dot(a, b, preferred_element_type=F32)


def _dot_nt(a, b):
    return lax.dot_general(a, b, (((1,), (1,)), ((), ())), preferred_element_type=F32)


def _split_bf16(x, terms):
    parts = []
    for _ in range(terms):
        p = x.astype(BF16)
        parts.append(p)
        x = x - p.astype(F32)
    return parts


def _dot_split(a, b):
    a_hi, a_lo = _split_bf16(a, 2)
    b_hi, b_lo = _split_bf16(b, 2)
    return _dot(a_hi, b_hi) + (_dot(a_lo, b_hi) + _dot(a_hi, b_lo))


def _rms(x, w):
    return x * lax.rsqrt(jnp.mean(x * x, axis=-1, keepdims=True) + EPS) * w


def _swap_lane_groups(x, half):
    lane = lax.broadcasted_iota(jnp.int32, x.shape, 1)
    up = pltpu.roll(x, LANES - half, 1)
    down = pltpu.roll(x, half, 1)
    return jnp.where((lane & half) == 0, up, down)


def _vmem_limit(nbytes):
    return int(min(VMEM_BYTES - (4 << 20), max(nbytes, 32 << 20)))


def _layer_spec(stacked, layer):
    shape = stacked.shape[1:]
    nd = len(shape)
    return pl.BlockSpec((None,) + shape, lambda *_: (layer,) + (0,) * nd, pipeline_mode=pl.Buffered(1))


def _const_spec(shape):
    nd = len(shape)
    return pl.BlockSpec(shape, lambda *_: (0,) * nd, pipeline_mode=pl.Buffered(1))


def _mod_kernel(c_ref, w_ref, b_ref, o_ref):
    cv = c_ref[...]
    sc = cv * jax.nn.sigmoid(cv)
    o_ref[...] = jnp.dot(sc, w_ref[...], precision=_HI, preferred_element_type=F32) + b_ref[...]


def _modulation(cvec, w_mod, b_mod):
    L, D, N = w_mod.shape
    nb = N // MOD_COLS
    return pl.pallas_call(
        _mod_kernel,
        grid=(L, nb),
        in_specs=[
            pl.BlockSpec((SUBLANES, D), lambda l, j: (0, 0)),
            pl.BlockSpec((None, D, MOD_COLS), lambda l, j: (l, 0, j)),
            pl.BlockSpec((None, 1, MOD_COLS), lambda l, j: (l, 0, j)),
        ],
        out_specs=pl.BlockSpec((None, SUBLANES, MOD_COLS), lambda l, j: (l, 0, j)),
        out_shape=jax.ShapeDtypeStruct((L, SUBLANES, N), F32),
        compiler_params=pltpu.CompilerParams(
            dimension_semantics=("parallel", "parallel"),
            vmem_limit_bytes=_vmem_limit(3 * D * MOD_COLS * 4)),
    )(cvec, w_mod, b_mod.reshape(L, 1, N))


_C_AQ = 0
_C_AK = _C_AQ + A_HEADS * A_HEAD_DIM
_C_AV = _C_AK + A_KV_HEADS * A_HEAD_DIM
_C_BQ = _C_AV + A_KV_HEADS * A_HEAD_DIM
_C_BKV = _C_BQ + B_Q_RANK
_C_BKR = _C_BKV + B_KV_RANK
_C_CQ = _C_BKR + LANES
_C_CK = _C_CQ + C_HEADS * C_KEY_DIM
_C_CV = _C_CK + C_HEADS * C_KEY_DIM
_C_CR = _C_CV + C_HEADS * C_VAL_DIM
_C_CG = _C_CR + C_HEADS * C_VAL_DIM
_C_GATE = _C_CG + LANES
B_HEAD_PAD = LANES
B_V_ROWS = B_V_DIM + 2 * SUBLANES
A_V_ROWS = A_HEAD_DIM + 2 * SUBLANES
LOG2E = 1.4426950408889634


def _stream_specs(xc, xl):
    off = 0 if xc is xl else 1
    d = xl.shape[2]
    spec_c = pl.BlockSpec((None, TM, d), lambda b, t: (b, 0, 0))
    spec_l = pl.BlockSpec((None, TM, d), lambda b, t: (b, jnp.maximum(t - off, 0), 0))
    return spec_c, spec_l, xl.shape[1] + off * TM


def _stream_tile(xc_ref, xl_ref):
    return jnp.where(pl.program_id(1) == 0, xc_ref[...], xl_ref[...])


def _inproj_kernel(xc_ref, xl_ref, mod_ref, nw_ref, w1_ref, wkr_ref, w2_ref, wcg_ref, w3_ref, rope_ref,
                   bqn_ref, bkvn_ref, wuq_ref, wkn_ref,
                   wvb_ref, wg_ref, bg_ref,
                   qa_ref, ka_ref, va_ref, qb_ref, kb_ref, vb_ref, cq_ref, ck_ref, cv_ref, cr_ref,
                   g_ref, gate_ref, *, d_model):
    mod = mod_ref[...]
    h = _rms(_stream_tile(xc_ref, xl_ref), nw_ref[...]) * (1.0 + mod[1:2]) + mod[0:1]
    hb = h.astype(BF16)

    pieces = ((_C_AQ, w1_ref), (_C_BKR, wkr_ref), (_C_CQ, w2_ref), (_C_CG, wcg_ref), (_C_GATE, w3_ref))

    def proj(lo, hi):
        base, ref = [(b0, r) for b0, r in pieces if b0 <= lo][-1]
        assert hi - base <= ref.shape[1]
        return _dot(hb, ref[:, lo - base:hi - base])

    cos_a, sin_a, cos_b, sin_b = rope_ref[0], rope_ref[1], rope_ref[2], rope_ref[3]

    def rope_a(t):
        return t * cos_a + _swap_lane_groups(t, A_HEAD_DIM // 2) * sin_a

    def rope_b(t):
        return t * cos_b + _swap_lane_groups(t, B_ROPE_DIM // 2) * sin_b

    a_scale = A_HEAD_DIM ** -0.5 * LOG2E
    for c in range(A_HEADS * A_HEAD_DIM // LANES):
        t = proj(_C_AQ + c * LANES, _C_AQ + (c + 1) * LANES)
        qa_ref[c * LANES:(c + 1) * LANES, :] = (rope_a(t) * a_scale).T.astype(BF16)
    ka_ref[...] = rope_a(proj(_C_AK, _C_AV)).astype(BF16)
    va_t = proj(_C_AV, _C_BQ).T.astype(BF16)
    a_ones = jnp.where(lax.broadcasted_iota(jnp.int32, (A_V_ROWS - A_HEAD_DIM, va_t.shape[1]), 0) == 0,
                       1.0, 0.0).astype(BF16)
    for g in range(A_KV_HEADS):
        va_ref[g * A_V_ROWS:g * A_V_ROWS + A_HEAD_DIM, :] = va_t[g * A_HEAD_DIM:(g + 1) * A_HEAD_DIM]
        va_ref[g * A_V_ROWS + A_HEAD_DIM:(g + 1) * A_V_ROWS, :] = a_ones

    b_scale = (B_NOPE_DIM + B_ROPE_DIM) ** -0.5 * LOG2E
    cqn = _rms(proj(_C_BQ, _C_BKV), bqn_ref[...]).astype(BF16)
    ckvn = _rms(proj(_C_BKV, _C_BKR), bkvn_ref[...]).astype(BF16)
    kr = rope_b(proj(_C_BKR, _C_CQ))
    for hd in range(B_HEADS):
        sl = slice(hd * B_HEAD_PAD, (hd + 1) * B_HEAD_PAD)
        qb_ref[sl, :] = (rope_b(_dot(cqn, wuq_ref[:, sl])) * b_scale).T.astype(BF16)
        kb_ref[:, sl] = (_dot(ckvn, wkn_ref[:, sl]) + kr).astype(BF16)
    vb = _dot(ckvn, wvb_ref[...])
    tk = vb_ref.shape[2]
    per = LANES // B_V_DIM
    ones_rows = jnp.where(lax.broadcasted_iota(jnp.int32, (B_V_ROWS - B_V_DIM, tk), 0) == 0, 1.0, 0.0).astype(BF16)
    for cc in range(vb_ref.shape[0]):
        for c in range(vb.shape[1] // LANES):
            v_t = vb[cc * tk:(cc + 1) * tk, c * LANES:(c + 1) * LANES].T.astype(BF16)
            for i in range(per):
                r0 = (c * per + i) * B_V_ROWS
                vb_ref[cc, r0:r0 + B_V_DIM, :] = v_t[i * B_V_DIM:(i + 1) * B_V_DIM]
                vb_ref[cc, r0 + B_V_DIM:r0 + B_V_ROWS, :] = ones_rows

    cq_ref[...] = proj(_C_CQ, _C_CK) * (C_KEY_DIM ** -0.5)
    ck_ref[...] = proj(_C_CK, _C_CV)
    cv_ref[...] = proj(_C_CV, _C_CR).astype(BF16)
    cr_ref[...] = proj(_C_CR, _C_CG).astype(BF16)
    z = _dot_split(proj(_C_CG, _C_GATE), wg_ref[...]) + bg_ref[...]
    g_ref[...] = (jnp.minimum(z, 0.0) - jnp.log1p(jnp.exp(-jnp.abs(z)))) * (1.0 / C_GATE_TAU)

    gate_ref[...] = proj(_C_GATE, _C_GATE + 3 * d_model).astype(BF16)

def _mod_index(b, t):
    return (jnp.where(t == 0, 0, b + 1), 0, 0)


def _inproj(Xc, Xl, mod, nw, layer, w_pieces, rope, bqn, bkvn, wuq, wkn, wvb, wg, bg):
    spec_c, spec_l, T = _stream_specs(Xc, Xl)
    B, _, D = Xl.shape
    NT = T // TM
    n_in = sum(w.shape[2] for w in w_pieces)
    row = lambda n: pl.BlockSpec((None, TM, n), lambda b, t: (b, t, 0))
    widths = [(A_HEADS * A_HEAD_DIM, BF16), (A_KV_HEADS * A_HEAD_DIM, BF16), (A_KV_HEADS * A_HEAD_DIM, BF16),
              (B_HEADS * B_HEAD_PAD, BF16), (B_HEADS * B_HEAD_PAD, BF16), (B_HEADS * B_V_DIM, BF16),
              (C_HEADS * C_KEY_DIM, F32), (C_HEADS * C_KEY_DIM, F32), (C_HEADS * C_VAL_DIM, BF16),
              (C_HEADS * C_VAL_DIM, BF16), (2 * C_HEADS * C_KEY_DIM, F32), (3 * D, BF16)]
    out_bytes = sum(TM * n * jnp.dtype(dt).itemsize for n, dt in widths)
    vmem = D * n_in * 2 + 4 * out_bytes + 6 * TM * D * 4 + (8 << 20)
    out_specs = [row(n) for n, _ in widths]
    out_shape = [jax.ShapeDtypeStruct((B, T, n), dt) for n, dt in widths]
    qw, vw = B_HEADS * B_HEAD_PAD, B_HEADS * B_V_ROWS
    out_specs[3] = pl.BlockSpec((None, qw, TM), lambda b, t: (b, 0, t))
    out_shape[3] = jax.ShapeDtypeStruct((B, qw, T), BF16)
    out_specs[0] = pl.BlockSpec((None, widths[0][0], TM), lambda b, t: (b, 0, t))
    out_shape[0] = jax.ShapeDtypeStruct((B, widths[0][0], T), BF16)
    out_specs[2] = pl.BlockSpec((None, A_KV_HEADS * A_V_ROWS, TM), lambda b, t: (b, 0, t))
    out_shape[2] = jax.ShapeDtypeStruct((B, A_KV_HEADS * A_V_ROWS, T), BF16)
    out_specs[5] = pl.BlockSpec((None, TM // MLA_TK, vw, MLA_TK), lambda b, t: (b, t, 0, 0))
    out_shape[5] = jax.ShapeDtypeStruct((B, T // MLA_TK, vw, MLA_TK), BF16)
    return pl.pallas_call(
        functools.partial(_inproj_kernel, d_model=D),
        grid=(B, NT),
        in_specs=[
            spec_c, spec_l,
            pl.BlockSpec((None, 6, D), _mod_index),
            _const_spec((1, D)),
            *[_layer_spec(w, layer) for w in w_pieces],
            pl.BlockSpec((4, TM, LANES), lambda b, t: (0, t, 0)),
            _const_spec(bqn.shape), _const_spec(bkvn.shape), _layer_spec(wuq, layer),
            _layer_spec(wkn, layer), _layer_spec(wvb, layer), _layer_spec(wg, layer), _const_spec(bg.shape),
        ],
        out_specs=out_specs,
        out_shape=out_shape,
        compiler_params=pltpu.CompilerParams(
            dimension_semantics=("parallel", "parallel"), vmem_limit_bytes=_vmem_limit(vmem)),
    )(Xc, Xl, mod, nw, *w_pieces, rope, bqn, bkvn, wuq, wkn, wvb, wg, bg)


def _attn_a_kernel(q_ref, kp_ref, km_ref, kn_ref, kc_ref, vp_ref, vm_ref, vn_ref, vc_ref, sink_ref,
                   o_ref, *, seq):
    t = pl.program_id(1)
    kloc = jnp.concatenate([kp_ref[...], km_ref[...], kn_ref[...]], axis=0)
    vloc = jnp.concatenate([vp_ref[...], vm_ref[...], vn_ref[...]], axis=1)
    kctx, vctx = kc_ref[...], vc_ref[...]
    nloc = kloc.shape[0]
    r = lax.broadcasted_iota(jnp.int32, (nloc, TM), 0)
    c = lax.broadcasted_iota(jnp.int32, (nloc, TM), 1)
    rel = r - A_WINDOW - c
    kpos = (t - 1) * TM - A_WINDOW + r
    valid = (jnp.abs(rel) <= A_WINDOW) & (kpos >= 0) & (kpos < seq) & (t >= 1)
    group = A_HEADS // A_KV_HEADS
    per = LANES // A_HEAD_DIM
    q_zero = jnp.zeros((A_HEAD_DIM, TM), BF16)

    def scores(hd):
        q_t = q_ref[hd * A_HEAD_DIM:(hd + 1) * A_HEAD_DIM, :]
        g = hd // group
        qm = jnp.concatenate([q_t if i == g else q_zero for i in range(A_KV_HEADS)], axis=0)
        return _dot(kloc, qm), _dot(kctx, qm)

    cur = scores(0)
    parts = []
    for hd in range(A_HEADS):
        nxt = scores(hd + 1) if hd + 1 < A_HEADS else None
        s_loc = jnp.where(valid, cur[0], _NEG_INF)
        s_ctx = cur[1]
        sink = sink_ref[hd:hd + 1, 0:1] * LOG2E
        m = jnp.maximum(jnp.maximum(jnp.max(s_loc, axis=0, keepdims=True),
                                    jnp.max(s_ctx, axis=0, keepdims=True)), sink)
        p_loc = jnp.exp2((s_loc - m).astype(BF16))
        p_ctx = jnp.exp2((s_ctx - m).astype(BF16))
        g = hd // group
        vs = slice(g * A_V_ROWS, (g + 1) * A_V_ROWS)
        o_t = _dot(vloc[vs], p_loc) + _dot(vctx[vs], p_ctx)
        denom = o_t[A_HEAD_DIM:A_HEAD_DIM + 1] + jnp.exp2(sink - m)
        parts.append(o_t[0:A_HEAD_DIM] / denom)
        if len(parts) == per:
            c = hd // per
            o_ref[:, c * LANES:(c + 1) * LANES] = jnp.concatenate(parts, axis=0).T.astype(BF16)
            parts = []
        cur = nxt


def _attn_a(qa, ka, va, sink, lc):
    B, qw, T = qa.shape
    NT = T // TM
    seq = T - lc
    kvw = ka.shape[2]
    vrows = va.shape[1]
    per_tm = TM // A_WINDOW
    last = T // A_WINDOW - 1
    prev_i = lambda t: jnp.maximum(t * per_tm - 1, 0)
    next_i = lambda t: jnp.minimum((t + 1) * per_tm, last)
    kprev = pl.BlockSpec((None, A_WINDOW, kvw), lambda b, t: (b, prev_i(t), 0))
    kmain = pl.BlockSpec((None, TM, kvw), lambda b, t: (b, t, 0))
    knext = pl.BlockSpec((None, A_WINDOW, kvw), lambda b, t: (b, next_i(t), 0))
    kctx = pl.BlockSpec((None, lc, kvw), lambda b, t: (b, 0, 0))
    vprev = pl.BlockSpec((None, vrows, A_WINDOW), lambda b, t: (b, 0, prev_i(t)))
    vmain = pl.BlockSpec((None, vrows, TM), lambda b, t: (b, 0, t))
    vnext = pl.BlockSpec((None, vrows, A_WINDOW), lambda b, t: (b, 0, next_i(t)))
    vctx = pl.BlockSpec((None, vrows, lc), lambda b, t: (b, 0, 0))
    return pl.pallas_call(
        functools.partial(_attn_a_kernel, seq=seq),
        grid=(B, NT),
        in_specs=[pl.BlockSpec((None, qw, TM), lambda b, t: (b, 0, t)),
                  kprev, kmain, knext, kctx, vprev, vmain, vnext, vctx,
                  pl.BlockSpec(sink.shape, lambda b, t: (0, 0))],
        out_specs=pl.BlockSpec((None, TM, qw), lambda b, t: (b, t, 0)),
        out_shape=jax.ShapeDtypeStruct((B, T, qw), BF16),
        compiler_params=pltpu.CompilerParams(dimension_semantics=("parallel", "parallel")),
    )(qa, ka, ka, ka, ka, va, va, va, va, sink)


def _mla_kernel(q_ref, k_ref, v_ref, o_ref, m_ref, acc_ref, sa_ref, sb_ref, *, lc, tk):
    t = pl.program_id(1)
    total = k_ref.shape[0]
    nchunks = jnp.where(t == 0, lc // tk, total // tk)
    m_ref[...] = jnp.full(m_ref.shape, _NEG_INF, F32)
    acc_ref[...] = jnp.zeros(acc_ref.shape, F32)

    def stage(dst_ref, j, hd):
        qs = slice(hd * B_HEAD_PAD, (hd + 1) * B_HEAD_PAD)
        s = _dot(k_ref[pl.ds(pl.multiple_of(j * tk, tk), tk), qs], q_ref[qs, :])
        dst_ref[hd, 0:tk, :] = s
        dst_ref[hd, tk:tk + SUBLANES, :] = jnp.broadcast_to(jnp.max(s, axis=0, keepdims=True), (SUBLANES, TM))

    for hd in range(B_HEADS):
        stage(sa_ref, 0, hd)

    def step(j, cur_ref, nxt_ref, last=False):
        for hd in range(B_HEADS):
            vs = slice(hd * B_V_ROWS, (hd + 1) * B_V_ROWS)
            if not last:
                stage(nxt_ref, j + 1, hd)
            m_prev = m_ref[hd]
            m_new = jnp.maximum(m_prev, cur_ref[hd, tk:tk + SUBLANES, :])
            alpha = jnp.exp2(m_prev - m_new)
            p = jnp.exp2((cur_ref[hd, 0:tk, :] - m_new[0:1]).astype(BF16))
            m_ref[hd] = m_new
            acc_ref[vs, :] = acc_ref[vs, :] * alpha[0:1] + _dot(v_ref[j, vs, :], p)

    bufs = (sa_ref, sb_ref)

    def body(g, carry):
        for i in range(MLA_UNROLL):
            step(MLA_UNROLL * g + i, bufs[i % 2], bufs[(i + 1) % 2])
        return carry

    ngroups = (nchunks - 1) // MLA_UNROLL
    lax.fori_loop(0, ngroups, body, 0)
    tail = (total // tk - 1) % MLA_UNROLL + 1
    assert tail == (lc // tk - 1) % MLA_UNROLL + 1
    for i in range(tail):
        step(MLA_UNROLL * ngroups + i, bufs[i % 2], bufs[(i + 1) % 2], last=(i == tail - 1))
    per = LANES // B_V_DIM
    for c in range(B_HEADS // per):
        o_t = jnp.concatenate(
            [acc_ref[hd * B_V_ROWS:hd * B_V_ROWS + B_V_DIM, :]
             / acc_ref[hd * B_V_ROWS + B_V_DIM:hd * B_V_ROWS + B_V_DIM + 1, :]
             for hd in range(c * per, (c + 1) * per)], axis=0)
        o_ref[:, c * LANES:(c + 1) * LANES] = o_t.T.astype(BF16)


def _mla(qb, kb, vb, lc):
    B, qw, T = qb.shape
    NT = T // TM
    _, nck, vw, tk = vb.shape
    ow = B_HEADS * B_V_DIM
    vmem = (T * (qw + vw) * 2 + 4 * TM * qw * 2 + (B_HEADS * SUBLANES * TM + 3 * TM * vw) * 4
            + 2 * B_HEADS * (tk + SUBLANES) * TM * 4 + (16 << 20))
    return pl.pallas_call(
        functools.partial(_mla_kernel, lc=lc, tk=tk),
        grid=(B, NT),
        in_specs=[pl.BlockSpec((None, qw, TM), lambda b, t: (b, 0, t)),
                  pl.BlockSpec((None, T, qw), lambda b, t: (b, 0, 0), pipeline_mode=pl.Buffered(1)),
                  pl.BlockSpec((None, nck, vw, tk), lambda b, t: (b, 0, 0, 0), pipeline_mode=pl.Buffered(1))],
        out_specs=pl.BlockSpec((None, TM, ow), lambda b, t: (b, t, 0)),
        out_shape=jax.ShapeDtypeStruct((B, T, ow), BF16),
        scratch_shapes=[pltpu.VMEM((B_HEADS, SUBLANES, TM), F32),
                        pltpu.VMEM((vw, TM), F32),
                        pltpu.VMEM((B_HEADS, tk + SUBLANES, TM), F32),
                        pltpu.VMEM((B_HEADS, tk + SUBLANES, TM), F32)],
        compiler_params=pltpu.CompilerParams(
            dimension_semantics=("parallel", "arbitrary"), vmem_limit_bytes=_vmem_limit(vmem)),
    )(qb, kb, vb)


def _gla_prep(q, k, b, reverse):
    C = q.shape[0]
    dk2 = q.shape[1]
    nsub = C // GLA_SUB
    b_end = b[0:1] if reverse else b[C - 1:C]
    row = lax.broadcasted_iota(jnp.int32, (C, dk2), 0)
    lane = lax.broadcasted_iota(jnp.int32, (C, dk2), 1)

    qe = (q * jnp.exp(b)).astype(BF16)
    kd_t = (k * jnp.exp(b_end - b)).T.astype(BF16)
    dec = jnp.exp(jnp.broadcast_to(b_end, (dk2, dk2)).T)
    dec2 = jnp.concatenate([dec] * (2 * C_VAL_DIM // dk2), axis=1)

    refs = [blk * GLA_SUB + (GLA_SUB - 1 if reverse else 0) for blk in range(nsub)]
    b_ref = jnp.concatenate([jnp.broadcast_to(b[n:n + 1], (GLA_SUB, dk2)) for n in refs], axis=0)
    q_sc = q * jnp.exp(b - b_ref)
    k_parts = []
    for blk, n in enumerate(refs):
        in_range = (row >= blk * GLA_SUB) if reverse else (row < (blk + 1) * GLA_SUB)
        k_parts.append(jnp.where(in_range, k * jnp.exp(b[n:n + 1] - b), 0.0))
    k_big = jnp.concatenate(k_parts, axis=1).astype(BF16)
    q_halves = []
    for hf in range(2):
        in_half = (lane >= hf * C_KEY_DIM) & (lane < (hf + 1) * C_KEY_DIM)
        q_halves.append(jnp.concatenate(
            [jnp.where(in_half & (row >= blk * GLA_SUB) & (row < (blk + 1) * GLA_SUB), q_sc, 0.0)
             for blk in range(nsub)], axis=1))
    q_big = jnp.concatenate(q_halves, axis=0).astype(BF16)
    return qe, kd_t, dec2, q_big, k_big


def _gla_kernel(qf_ref, kf_ref, vf_ref, gf_ref, qr_ref, kr_ref, vr_ref, gr_ref, of_ref, or_ref,
                sf_ref, sr_ref):
    t = pl.program_id(1)

    @pl.when(t == 0)
    def _():
        sf_ref[...] = jnp.zeros(sf_ref.shape, F32)
        sr_ref[...] = jnp.zeros(sr_ref.shape, F32)

    C = C_CHUNK
    nchunk = TM // C
    npair = C_HEADS // 2
    kw, vw = 2 * C_KEY_DIM, 2 * C_VAL_DIM
    dirs = ((qf_ref, kf_ref, vf_ref, gf_ref, of_ref, sf_ref, False),
            (qr_ref, kr_ref, vr_ref, gr_ref, or_ref, sr_ref, True))

    ii = lax.broadcasted_iota(jnp.int32, (TM, TM), 0)
    jj = lax.broadcasted_iota(jnp.int32, (TM, TM), 1)
    same_chunk = (ii // C) == (jj // C)
    b_all = []
    for (_, _, _, g_ref, _, _, reverse) in dirs:
        tri = jnp.where(same_chunk & ((jj >= ii) if reverse else (jj <= ii)), 1.0, 0.0).astype(BF16)
        g_hi, g_mid, g_lo = _split_bf16(g_ref[...], 3)
        b_all.append(_dot(tri, g_hi) + (_dot(tri, g_mid) + _dot(tri, g_lo)))

    units = []
    for d, (q_ref, k_ref, v_ref, _, o_ref, _, reverse) in enumerate(dirs):
        for p in range(npair):
            ks, vsl = slice(p * kw, (p + 1) * kw), slice(p * vw, (p + 1) * vw)
            for c in range(nchunk):
                rows = slice(c * C, (c + 1) * C)
                prep = _gla_prep(q_ref[rows, ks], k_ref[rows, ks], b_all[d][rows, ks], reverse)
                units.append((d, p, c, rows, vsl, v_ref, o_ref, reverse, prep))

    srow = lax.broadcasted_iota(jnp.int32, (kw, vw), 0)
    scol = lax.broadcasted_iota(jnp.int32, (kw, vw), 1)
    on_diag = (srow < C_KEY_DIM) == (scol < C_VAL_DIM)
    qi = lax.broadcasted_iota(jnp.int32, (2 * C, C), 0) % C
    kj = lax.broadcasted_iota(jnp.int32, (2 * C, C), 1)
    kvs, atts = [], []
    for (d, p, c, rows, vsl, v_ref, o_ref, reverse, prep) in units:
        _, kd_t, _, q_big, k_big = prep
        kvs.append(jnp.where(on_diag, _dot(kd_t, v_ref[rows, vsl]), 0.0))
        keep = (kj >= qi) if reverse else (kj <= qi)
        atts.append(jnp.where(keep, _dot_nt(q_big, k_big), 0.0).astype(BF16))

    intra = {}
    for u, (d, p, c, rows, vsl, v_ref, o_ref, reverse, prep) in enumerate(units):
        pv = _dot(atts[u], v_ref[rows, vsl])
        intra[(d, p, c)] = (u, jnp.concatenate([pv[0:C, 0:C_VAL_DIM], pv[C:2 * C, C_VAL_DIM:]], axis=1))

    states = {(d, p): dirs[d][5][p] for d in range(len(dirs)) for p in range(npair)}
    for step in range(nchunk):
        for d in range(len(dirs)):
            reverse = dirs[d][6]
            c = nchunk - 1 - step if reverse else step
            for p in range(npair):
                u, o_intra = intra[(d, p, c)]
                _, _, _, rows, vsl, _, o_ref, _, prep = units[u]
                qe, _, dec2, _, _ = prep
                o_ref[rows, vsl] = o_intra + _dot(qe, states[(d, p)].astype(BF16))
                states[(d, p)] = dec2 * states[(d, p)] + kvs[u]
    for d in range(len(dirs)):
        for p in range(npair):
            dirs[d][5][p] = states[(d, p)]


def _gla(cq, ck, cv, g):
    B, T, kw = cq.shape
    NT = T // TM
    vw = cv.shape[2]

    def fwd(n, col=0):
        return pl.BlockSpec((None, TM, n), lambda b, t: (b, t, col))

    def rev(n, col=0):
        return pl.BlockSpec((None, TM, n), lambda b, t: (b, jnp.where(t == 0, 0, NT - t), col))

    return pl.pallas_call(
        _gla_kernel,
        grid=(B, NT),
        in_specs=[fwd(kw), fwd(kw), fwd(vw), fwd(kw, 0), rev(kw), rev(kw), rev(vw), rev(kw, 1)],
        out_specs=[fwd(vw), rev(vw)],
        out_shape=[jax.ShapeDtypeStruct((B, T, vw), F32)] * 2,
        scratch_shapes=[pltpu.VMEM((C_HEADS // 2, 2 * C_KEY_DIM, 2 * C_VAL_DIM), F32)] * 2,
        compiler_params=pltpu.CompilerParams(dimension_semantics=("parallel", "arbitrary")),
    )(cq, ck, cv, g, cq, ck, cv, g)


def _merge_kernel(xc_ref, xl_ref, mod_ref, ya_ref, yb_ref, of_ref, or_ref, cr_ref, gate_ref, hn_ref,
                  wa_ref, wb_ref, wc_ref, wo_ref, o_ref):
    d = xl_ref.shape[1]
    o = of_ref[...] + or_ref[...]
    r = cr_ref[...].astype(F32)
    parts = []
    for hd in range(C_HEADS):
        sl = slice(hd * C_VAL_DIM, (hd + 1) * C_VAL_DIM)
        parts.append(_rms(o[:, sl], hn_ref[:, sl]))
    yc = (jnp.concatenate(parts, axis=1) * (r * jax.nn.sigmoid(r))).astype(BF16)

    def gate(i):
        return jax.nn.sigmoid(gate_ref[:, i * d:(i + 1) * d].astype(F32))

    m = (gate(0) * _dot(ya_ref[...], wa_ref[...]) + gate(1) * _dot(yb_ref[...], wb_ref[...])
         + gate(2) * _dot(yc, wc_ref[...]))
    y = _dot(m.astype(BF16), wo_ref[...])
    o_ref[...] = _stream_tile(xc_ref, xl_ref) + mod_ref[2:3, :] * y


def _merge(Xc, Xl, mod, ya, yb, of, orv, cr, gates, hn, layer, wa, wb, wc, wo):
    spec_c, spec_l, T = _stream_specs(Xc, Xl)
    B, _, D = Xl.shape
    NT = T // TM
    row = lambda n: pl.BlockSpec((None, TM, n), lambda b, t: (b, t, 0))
    wbytes = sum(w.size // w.shape[0] for w in (wa, wb, wc, wo)) * 2
    vmem = wbytes + 2 * TM * (2 * D + 3 * D + 4 * ya.shape[2]) * 4 + (16 << 20)
    return pl.pallas_call(
        _merge_kernel,
        grid=(B, NT),
        in_specs=[spec_c, spec_l, pl.BlockSpec((None, 6, D), _mod_index),
                  row(ya.shape[2]), row(yb.shape[2]), row(of.shape[2]), row(orv.shape[2]),
                  row(cr.shape[2]), row(gates.shape[2]), _const_spec(hn.shape),
                  _layer_spec(wa, layer), _layer_spec(wb, layer), _layer_spec(wc, layer), _layer_spec(wo, layer)],
        out_specs=row(D),
        out_shape=jax.ShapeDtypeStruct((B, T, D), F32),
        compiler_params=pltpu.CompilerParams(
            dimension_semantics=("parallel", "parallel"), vmem_limit_bytes=_vmem_limit(vmem)),
    )(Xc, Xl, mod, ya, yb, of, orv, cr, gates, hn, wa, wb, wc, wo)


def _ffn_kernel(x_ref, xp_ref, xn_ref, mod_ref, nw_ref, wup_ref, cw_ref, cb_ref, wdn_ref, fw_ref, o_ref,
                u_ref, act_ref, *, nt, ncb, t0, final):
    t = pl.program_id(1) + t0
    mod = mod_ref[...]
    nw = nw_ref[...]
    halo = xp_ref.shape[0]
    ffn = wdn_ref.shape[0]
    cb = ffn // ncb

    def norm(xv):
        return _rms(xv, nw) * (1.0 + mod[4:5]) + mod[3:4]

    has_prev = (t >= 2)
    has_next = (t >= 1) & (t < nt - 1)
    hb = jnp.concatenate([jnp.where(has_prev, norm(xp_ref[...]), 0.0), norm(x_ref[...]),
                          jnp.where(has_next, norm(xn_ref[...]), 0.0)], axis=0).astype(BF16)

    def up(j):
        u_ref[j % 2, 0] = _dot(hb, wup_ref[:, j * cb:(j + 1) * cb])
        u_ref[j % 2, 1] = _dot(hb, wup_ref[:, ffn + j * cb:ffn + (j + 1) * cb])

    def conv(slot, half, lo):
        w = cw_ref[:, lo:lo + cb]
        return (u_ref[slot, half, halo - 1:halo - 1 + TM, :] * w[0:1] + u_ref[slot, half, halo:halo + TM, :] * w[1:2]
                + u_ref[slot, half, halo + 1:halo + 1 + TM, :] * w[2:3] + cb_ref[:, lo:lo + cb])

    def act(j):
        gt = conv(j % 2, 0, j * cb)
        val = conv(j % 2, 1, ffn + j * cb)
        act_ref[:, j * cb:(j + 1) * cb] = (gt * jax.nn.sigmoid(gt) * val).astype(BF16)

    up(0)
    for j in range(1, ncb):
        up(j)
        act(j - 1)
    act(ncb - 1)
    y = x_ref[...] + mod[5:6] * _dot(act_ref[...], wdn_ref[...])
    o_ref[...] = _rms(y, fw_ref[...]) if final else y


def _ffn(X, mod, nw, layer, wup, cw, cb, wdn, fw, final):
    B, T, D = X.shape
    NT = T // TM
    t0 = 1 if final else 0
    halo = SUBLANES
    per = TM // halo
    last = T // halo - 1
    ffn = wdn.shape[1]
    cbw = FFN_COLS
    ncb = ffn // cbw
    vmem = 3 * D * ffn * 2 + 4 * (TM + 2 * halo) * cbw * 4 + TM * ffn * 2 + 8 * TM * D * 4 + (16 << 20)
    return pl.pallas_call(
        functools.partial(_ffn_kernel, nt=NT, ncb=ncb, t0=t0, final=final),
        grid=(B, NT - t0),
        in_specs=[pl.BlockSpec((None, TM, D), lambda b, t: (b, t + t0, 0)),
                  pl.BlockSpec((None, halo, D), lambda b, t: (b, jnp.maximum((t + t0) * per - 1, 0), 0)),
                  pl.BlockSpec((None, halo, D), lambda b, t: (b, jnp.minimum((t + t0 + 1) * per, last), 0)),
                  pl.BlockSpec((None, 6, D), lambda b, t: _mod_index(b, t + t0)),
                  _const_spec(nw.shape), _layer_spec(wup, layer), _const_spec(cw.shape), _const_spec(cb.shape),
                  _layer_spec(wdn, layer), _const_spec(fw.shape)],
        out_specs=pl.BlockSpec((None, TM, D), lambda b, t: (b, t, 0)),
        out_shape=jax.ShapeDtypeStruct((B, T - t0 * TM, D), F32),
        scratch_shapes=[pltpu.VMEM((2, 2, TM + 2 * halo, cbw), F32), pltpu.VMEM((TM, ffn), BF16)],
        compiler_params=pltpu.CompilerParams(
            dimension_semantics=("parallel", "parallel"), vmem_limit_bytes=_vmem_limit(vmem)),
    )(X, X, X, mod, nw, wup, cw, cb, wdn, fw)


def _rope_tables(seq, lc):
    rows = seq // GRID_W
    row = jnp.broadcast_to(jnp.arange(rows)[:, None], (rows, GRID_W)).reshape(-1).astype(F32)
    col = jnp.broadcast_to(jnp.arange(GRID_W)[None, :], (rows, GRID_W)).reshape(-1).astype(F32)

    def cs(rot_dim):
        n_freq = rot_dim // 4
        inv = ROPE_BASE ** (-jnp.arange(n_freq, dtype=F32) / n_freq)
        ang = jnp.concatenate([row[:, None] * inv, col[:, None] * inv], axis=-1)
        return jnp.cos(ang), jnp.sin(ang)

    ca, sa = cs(A_HEAD_DIM)
    cb, sb = cs(B_ROPE_DIM)
    reps = LANES // A_HEAD_DIM
    cos_a = jnp.tile(jnp.concatenate([ca, ca], axis=1), (1, reps))
    sin_a = jnp.tile(jnp.concatenate([-sa, sa], axis=1), (1, reps))
    one = jnp.ones((seq, B_NOPE_DIM), F32)
    pad = LANES - B_NOPE_DIM - B_ROPE_DIM
    cos_b = jnp.concatenate([one, cb, cb, jnp.ones((seq, pad), F32)], axis=1)
    sin_b = jnp.concatenate([0 * one, -sb, sb, jnp.zeros((seq, pad), F32)], axis=1)
    ident = jnp.stack([jnp.ones((lc, LANES), F32), jnp.zeros((lc, LANES), F32)] * 2)
    return jnp.concatenate([ident, jnp.stack([cos_a, sin_a, cos_b, sin_b])], axis=1)


def _prep_w_in(w_in):
    o_kr = _C_BKR
    o_cq = o_kr + B_ROPE_DIM
    o_cg = o_cq + (_C_CG - _C_CQ)
    o_gate = o_cg + 2 * C_GATE_RANK
    kr = jnp.pad(w_in[:, :, o_kr:o_cq], ((0, 0), (0, 0), (B_NOPE_DIM, LANES - B_NOPE_DIM - B_ROPE_DIM)))
    cg = jnp.pad(w_in[:, :, o_cg:o_gate], ((0, 0), (0, 0), (0, LANES - 2 * C_GATE_RANK)))
    return tuple(p.astype(BF16) for p in (w_in[:, :, :o_kr], kr, w_in[:, :, o_cq:o_cg], cg, w_in[:, :, o_gate:]))


def kernel(x, c, ctx, c_ctx, w_mod, b_mod, norm_mix, norm_ffn, w_in, a_sink, b_q_norm, b_kv_norm, b_w_uq, b_w_ukv, c_w_gate, c_b_gate, c_head_norm, w_br_a, w_br_b, w_br_c, w_out, w_up, conv_w, conv_b, w_down, final_norm):
    B, S, D = x.shape
    lc = ctx.shape[1]
    L = w_mod.shape[0]
    assert lc == TM and S % TM == 0 and S % GRID_W == 0 and B + 1 <= SUBLANES

    Xc, Xl = ctx, x
    cvec = jnp.zeros((SUBLANES, D), F32).at[0].set(c_ctx).at[1:B + 1].set(c)
    mods = _modulation(cvec, w_mod, b_mod).reshape(L, SUBLANES, 6, D)

    rope = _rope_tables(S, lc)
    w_in_r = _prep_w_in(w_in)
    sink = jnp.broadcast_to(a_sink[:, :, None], (L, A_HEADS, LANES)).astype(F32)
    wa, wb, wc, wo = w_br_a.astype(BF16), w_br_b.astype(BF16), w_br_c.astype(BF16), w_out.astype(BF16)
    qk = B_NOPE_DIM + B_ROPE_DIM
    wuq = jnp.pad(b_w_uq.reshape(L, B_Q_RANK, B_HEADS, qk),
                  ((0, 0), (0, 0), (0, 0), (0, B_HEAD_PAD - qk))).reshape(L, B_Q_RANK, -1).astype(BF16)
    ukv = b_w_ukv.reshape(L, B_KV_RANK, B_HEADS, B_NOPE_DIM + B_V_DIM)
    wkn = jnp.pad(ukv[..., :B_NOPE_DIM],
                  ((0, 0), (0, 0), (0, 0), (0, B_HEAD_PAD - B_NOPE_DIM))).reshape(L, B_KV_RANK, -1).astype(BF16)
    wvb = ukv[..., B_NOPE_DIM:].reshape(L, B_KV_RANK, -1).astype(BF16)
    nqk = C_HEADS * C_KEY_DIM
    wg = jnp.zeros((L, LANES, 2 * nqk), F32)
    wg = wg.at[:, 0:C_GATE_RANK, 0:nqk].set(c_w_gate[:, 0]).at[:, C_GATE_RANK:2 * C_GATE_RANK, nqk:].set(c_w_gate[:, 1])
    bg = c_b_gate.reshape(L, 1, 2 * nqk)
    wup, wdn = w_up.astype(BF16), w_down.astype(BF16)

    for l in range(L):
        mod = mods[l]
        (qa, ka, va, qb, kb, vb, cq, ck, cv, cr, g, gates) = _inproj(
            Xc, Xl, mod, norm_mix[l][None], l, w_in_r, rope, b_q_norm[l][None], b_kv_norm[l][None],
            wuq, wkn, wvb, wg, bg[l])
        ya = _attn_a(qa, ka, va, sink[l], lc)
        yb = _mla(qb, kb, vb, lc)
        of, orv = _gla(cq, ck, cv, g)
        X = _merge(Xc, Xl, mod, ya, yb, of, orv, cr, gates, c_head_norm[l][None], l, wa, wb, wc, wo)
        X = _ffn(X, mod, norm_ffn[l][None], l, wup, conv_w[l], conv_b[l][None], wdn, final_norm[None],
                 final=(l == L - 1))
        Xc = Xl = X
    return X
```

```python
import functools

import jax
import jax.numpy as jnp
from jax import lax
from jax.experimental import pallas as pl
from jax.experimental.pallas import tpu as pltpu

GRID_W = 64
EPS = 1e-6
ROPE_BASE = 10000.0
A_HEADS, A_KV_HEADS, A_HEAD_DIM, A_WINDOW = 8, 2, 64, 128
B_HEADS, B_Q_RANK, B_KV_RANK, B_NOPE_DIM, B_ROPE_DIM, B_V_DIM = 8, 256, 128, 64, 32, 64
C_HEADS, C_KEY_DIM, C_VAL_DIM, C_GATE_RANK, C_GATE_TAU, C_CHUNK = 4, 64, 128, 16, 16.0, 64
CONV_W = 3

LANES = 128
SUBLANES = 8
VMEM_BYTES = 64 * 1024 * 1024

TM = 256
GLA_SUB = 16
MLA_TK = 256
MLA_UNROLL = 16
MOD_COLS = 3072
FFN_COLS = 256
F32 = jnp.float32
BF16 = jnp.bfloat16
_HI = lax.Precision.HIGHEST
_NEG_INF = float("-inf")


def _dot(a, b):
    return jnp.dot(a, b, preferred_element_type=F32)


def _dot_nt(a, b):
    return lax.dot_general(a, b, (((1,), (1,)), ((), ())), preferred_element_type=F32)


def _split_bf16(x, terms):
    parts = []
    for _ in range(terms):
        p = x.astype(BF16)
        parts.append(p)
        x = x - p.astype(F32)
    return parts


def _dot_split(a, b):
    a_hi, a_lo = _split_bf16(a, 2)
    b_hi, b_lo = _split_bf16(b, 2)
    return _dot(a_hi, b_hi) + (_dot(a_lo, b_hi) + _dot(a_hi, b_lo))


def _rms(x, w):
    return x * lax.rsqrt(jnp.mean(x * x, axis=-1, keepdims=True) + EPS) * w


def _swap_lane_groups(x, half):
    lane = lax.broadcasted_iota(jnp.int32, x.shape, 1)
    up = pltpu.roll(x, LANES - half, 1)
    down = pltpu.roll(x, half, 1)
    return jnp.where((lane & half) == 0, up, down)


def _vmem_limit(nbytes):
    return int(min(VMEM_BYTES - (4 << 20), max(nbytes, 32 << 20)))


def _layer_spec(stacked, layer):
    shape = stacked.shape[1:]
    nd = len(shape)
    return pl.BlockSpec((None,) + shape, lambda *_: (layer,) + (0,) * nd, pipeline_mode=pl.Buffered(1))


def _const_spec(shape):
    nd = len(shape)
    return pl.BlockSpec(shape, lambda *_: (0,) * nd, pipeline_mode=pl.Buffered(1))


def _mod_kernel(c_ref, w_ref, b_ref, o_ref):
    cv = c_ref[...]
    sc = cv * jax.nn.sigmoid(cv)
    o_ref[...] = jnp.dot(sc, w_ref[...], precision=_HI, preferred_element_type=F32) + b_ref[...]


def _modulation(cvec, w_mod, b_mod):
    L, D, N = w_mod.shape
    nb = N // MOD_COLS
    return pl.pallas_call(
        _mod_kernel,
        grid=(L, nb),
        in_specs=[
            pl.BlockSpec((SUBLANES, D), lambda l, j: (0, 0)),
            pl.BlockSpec((None, D, MOD_COLS), lambda l, j: (l, 0, j)),
            pl.BlockSpec((None, 1, MOD_COLS), lambda l, j: (l, 0, j)),
        ],
        out_specs=pl.BlockSpec((None, SUBLANES, MOD_COLS), lambda l, j: (l, 0, j)),
        out_shape=jax.ShapeDtypeStruct((L, SUBLANES, N), F32),
        compiler_params=pltpu.CompilerParams(
            dimension_semantics=("parallel", "parallel"),
            vmem_limit_bytes=_vmem_limit(3 * D * MOD_COLS * 4)),
    )(cvec, w_mod, b_mod.reshape(L, 1, N))


_C_AQ = 0
_C_AK = _C_AQ + A_HEADS * A_HEAD_DIM
_C_AV = _C_AK + A_KV_HEADS * A_HEAD_DIM
_C_BQ = _C_AV + A_KV_HEADS * A_HEAD_DIM
_C_BKV = _C_BQ + B_Q_RANK
_C_BKR = _C_BKV + B_KV_RANK
_C_CQ = _C_BKR + LANES
_C_CK = _C_CQ + C_HEADS * C_KEY_DIM
_C_CV = _C_CK + C_HEADS * C_KEY_DIM
_C_CR = _C_CV + C_HEADS * C_VAL_DIM
_C_CG = _C_CR + C_HEADS * C_VAL_DIM
_C_GATE = _C_CG + LANES
B_HEAD_PAD = LANES
B_V_ROWS = B_V_DIM + 2 * SUBLANES
A_V_ROWS = A_HEAD_DIM + 2 * SUBLANES
LOG2E = 1.4426950408889634


def _stream_specs(xc, xl):
    off = 0 if xc is xl else 1
    d = xl.shape[2]
    spec_c = pl.BlockSpec((None, TM, d), lambda b, t: (b, 0, 0))
    spec_l = pl.BlockSpec((None, TM, d), lambda b, t: (b, jnp.maximum(t - off, 0), 0))
    return spec_c, spec_l, xl.shape[1] + off * TM


def _stream_tile(xc_ref, xl_ref):
    return jnp.where(pl.program_id(1) == 0, xc_ref[...], xl_ref[...])


def _inproj_kernel(xc_ref, xl_ref, mod_ref, nw_ref, w1_ref, wkr_ref, w2_ref, wcg_ref, w3_ref, rope_ref,
                   bqn_ref, bkvn_ref, wuq_ref, wkn_ref,
                   wvb_ref, wg_ref, bg_ref,
                   qa_ref, ka_ref, va_ref, qb_ref, kb_ref, vb_ref, cq_ref, ck_ref, cv_ref, cr_ref,
                   g_ref, gate_ref, *, d_model):
    mod = mod_ref[...]
    h = _rms(_stream_tile(xc_ref, xl_ref), nw_ref[...]) * (1.0 + mod[1:2]) + mod[0:1]
    hb = h.astype(BF16)

    pieces = ((_C_AQ, w1_ref), (_C_BKR, wkr_ref), (_C_CQ, w2_ref), (_C_CG, wcg_ref), (_C_GATE, w3_ref))

    def proj(lo, hi):
        base, ref = [(b0, r) for b0, r in pieces if b0 <= lo][-1]
        assert hi - base <= ref.shape[1]
        return _dot(hb, ref[:, lo - base:hi - base])

    cos_a, sin_a, cos_b, sin_b = rope_ref[0], rope_ref[1], rope_ref[2], rope_ref[3]

    def rope_a(t):
        return t * cos_a + _swap_lane_groups(t, A_HEAD_DIM // 2) * sin_a

    def rope_b(t):
        return t * cos_b + _swap_lane_groups(t, B_ROPE_DIM // 2) * sin_b

    a_scale = A_HEAD_DIM ** -0.5 * LOG2E
    for c in range(A_HEADS * A_HEAD_DIM // LANES):
        t = proj(_C_AQ + c * LANES, _C_AQ + (c + 1) * LANES)
        qa_ref[c * LANES:(c + 1) * LANES, :] = (rope_a(t) * a_scale).T.astype(BF16)
    ka_ref[...] = rope_a(proj(_C_AK, _C_AV)).astype(BF16)
    va_t = proj(_C_AV, _C_BQ).T.astype(BF16)
    a_ones = jnp.where(lax.broadcasted_iota(jnp.int32, (A_V_ROWS - A_HEAD_DIM, va_t.shape[1]), 0) == 0,
                       1.0, 0.0).astype(BF16)
    for g in range(A_KV_HEADS):
        va_ref[g * A_V_ROWS:g * A_V_ROWS + A_HEAD_DIM, :] = va_t[g * A_HEAD_DIM:(g + 1) * A_HEAD_DIM]
        va_ref[g * A_V_ROWS + A_HEAD_DIM:(g + 1) * A_V_ROWS, :] = a_ones

    b_scale = (B_NOPE_DIM + B_ROPE_DIM) ** -0.5 * LOG2E
    cqn = _rms(proj(_C_BQ, _C_BKV), bqn_ref[...]).astype(BF16)
    ckvn = _rms(proj(_C_BKV, _C_BKR), bkvn_ref[...]).astype(BF16)
    kr = rope_b(proj(_C_BKR, _C_CQ))
    for hd in range(B_HEADS):
        sl = slice(hd * B_HEAD_PAD, (hd + 1) * B_HEAD_PAD)
        qb_ref[sl, :] = (rope_b(_dot(cqn, wuq_ref[:, sl])) * b_scale).T.astype(BF16)
        kb_ref[:, sl] = (_dot(ckvn, wkn_ref[:, sl]) + kr).astype(BF16)
    vb = _dot(ckvn, wvb_ref[...])
    tk = vb_ref.shape[2]
    per = LANES // B_V_DIM
    ones_rows = jnp.where(lax.broadcasted_iota(jnp.int32, (B_V_ROWS - B_V_DIM, tk), 0) == 0, 1.0, 0.0).astype(BF16)
    for cc in range(vb_ref.shape[0]):
        for c in range(vb.shape[1] // LANES):
            v_t = vb[cc * tk:(cc + 1) * tk, c * LANES:(c + 1) * LANES].T.astype(BF16)
            for i in range(per):
                r0 = (c * per + i) * B_V_ROWS
                vb_ref[cc, r0:r0 + B_V_DIM, :] = v_t[i * B_V_DIM:(i + 1) * B_V_DIM]
                vb_ref[cc, r0 + B_V_DIM:r0 + B_V_ROWS, :] = ones_rows

    cq_ref[...] = proj(_C_CQ, _C_CK) * (C_KEY_DIM ** -0.5)
    ck_ref[...] = proj(_C_CK, _C_CV)
    cv_ref[...] = proj(_C_CV, _C_CR).astype(BF16)
    cr_ref[...] = proj(_C_CR, _C_CG).astype(BF16)
    z = _dot_split(proj(_C_CG, _C_GATE), wg_ref[...]) + bg_ref[...]
    g_ref[...] = (jnp.minimum(z, 0.0) - jnp.log1p(jnp.exp(-jnp.abs(z)))) * (1.0 / C_GATE_TAU)

    gate_ref[...] = proj(_C_GATE, _C_GATE + 3 * d_model).astype(BF16)

def _mod_index(b, t):
    return (jnp.where(t == 0, 0, b + 1), 0, 0)


def _inproj(Xc, Xl, mod, nw, layer, w_pieces, rope, bqn, bkvn, wuq, wkn, wvb, wg, bg):
    spec_c, spec_l, T = _stream_specs(Xc, Xl)
    B, _, D = Xl.shape
    NT = T // TM
    n_in = sum(w.shape[2] for w in w_pieces)
    row = lambda n: pl.BlockSpec((None, TM, n), lambda b, t: (b, t, 0))
    widths = [(A_HEADS * A_HEAD_DIM, BF16), (A_KV_HEADS * A_HEAD_DIM, BF16), (A_KV_HEADS * A_HEAD_DIM, BF16),
              (B_HEADS * B_HEAD_PAD, BF16), (B_HEADS * B_HEAD_PAD, BF16), (B_HEADS * B_V_DIM, BF16),
              (C_HEADS * C_KEY_DIM, F32), (C_HEADS * C_KEY_DIM, F32), (C_HEADS * C_VAL_DIM, BF16),
              (C_HEADS * C_VAL_DIM, BF16), (2 * C_HEADS * C_KEY_DIM, F32), (3 * D, BF16)]
    out_bytes = sum(TM * n * jnp.dtype(dt).itemsize for n, dt in widths)
    vmem = D * n_in * 2 + 4 * out_bytes + 6 * TM * D * 4 + (8 << 20)
    out_specs = [row(n) for n, _ in widths]
    out_shape = [jax.ShapeDtypeStruct((B, T, n), dt) for n, dt in widths]
    qw, vw = B_HEADS * B_HEAD_PAD, B_HEADS * B_V_ROWS
    out_specs[3] = pl.BlockSpec((None, qw, TM), lambda b, t: (b, 0, t))
    out_shape[3] = jax.ShapeDtypeStruct((B, qw, T), BF16)
    out_specs[0] = pl.BlockSpec((None, widths[0][0], TM), lambda b, t: (b, 0, t))
    out_shape[0] = jax.ShapeDtypeStruct((B, widths[0][0], T), BF16)
    out_specs[2] = pl.BlockSpec((None, A_KV_HEADS * A_V_ROWS, TM), lambda b, t: (b, 0, t))
    out_shape[2] = jax.ShapeDtypeStruct((B, A_KV_HEADS * A_V_ROWS, T), BF16)
    out_specs[5] = pl.BlockSpec((None, TM // MLA_TK, vw, MLA_TK), lambda b, t: (b, t, 0, 0))
    out_shape[5] = jax.ShapeDtypeStruct((B, T // MLA_TK, vw, MLA_TK), BF16)
    return pl.pallas_call(
        functools.partial(_inproj_kernel, d_model=D),
        grid=(B, NT),
        in_specs=[
            spec_c, spec_l,
            pl.BlockSpec((None, 6, D), _mod_index),
            _const_spec((1, D)),
            *[_layer_spec(w, layer) for w in w_pieces],
            pl.BlockSpec((4, TM, LANES), lambda b, t: (0, t, 0)),
            _const_spec(bqn.shape), _const_spec(bkvn.shape), _layer_spec(wuq, layer),
            _layer_spec(wkn, layer), _layer_spec(wvb, layer), _layer_spec(wg, layer), _const_spec(bg.shape),
        ],
        out_specs=out_specs,
        out_shape=out_shape,
        compiler_params=pltpu.CompilerParams(
            dimension_semantics=("parallel", "parallel"), vmem_limit_bytes=_vmem_limit(vmem)),
    )(Xc, Xl, mod, nw, *w_pieces, rope, bqn, bkvn, wuq, wkn, wvb, wg, bg)


def _attn_a_kernel(q_ref, kp_ref, km_ref, kn_ref, kc_ref, vp_ref, vm_ref, vn_ref, vc_ref, sink_ref,
                   o_ref, *, seq):
    t = pl.program_id(1)
    kloc = jnp.concatenate([kp_ref[...], km_ref[...], kn_ref[...]], axis=0)
    vloc = jnp.concatenate([vp_ref[...], vm_ref[...], vn_ref[...]], axis=1)
    kctx, vctx = kc_ref[...], vc_ref[...]
    nloc = kloc.shape[0]
    r = lax.broadcasted_iota(jnp.int32, (nloc, TM), 0)
    c = lax.broadcasted_iota(jnp.int32, (nloc, TM), 1)
    rel = r - A_WINDOW - c
    kpos = (t - 1) * TM - A_WINDOW + r
    valid = (jnp.abs(rel) <= A_WINDOW) & (kpos >= 0) & (kpos < seq) & (t >= 1)
    group = A_HEADS // A_KV_HEADS
    per = LANES // A_HEAD_DIM
    q_zero = jnp.zeros((A_HEAD_DIM, TM), BF16)

    def scores(hd):
        q_t = q_ref[hd * A_HEAD_DIM:(hd + 1) * A_HEAD_DIM, :]
        g = hd // group
        qm = jnp.concatenate([q_t if i == g else q_zero for i in range(A_KV_HEADS)], axis=0)
        return _dot(kloc, qm), _dot(kctx, qm)

    cur = scores(0)
    parts = []
    for hd in range(A_HEADS):
        nxt = scores(hd + 1) if hd + 1 < A_HEADS else None
        s_loc = jnp.where(valid, cur[0], _NEG_INF)
        s_ctx = cur[1]
        sink = sink_ref[hd:hd + 1, 0:1] * LOG2E
        m = jnp.maximum(jnp.maximum(jnp.max(s_loc, axis=0, keepdims=True),
                                    jnp.max(s_ctx, axis=0, keepdims=True)), sink)
        p_loc = jnp.exp2((s_loc - m).astype(BF16))
        p_ctx = jnp.exp2((s_ctx - m).astype(BF16))
        g = hd // group
        vs = slice(g * A_V_ROWS, (g + 1) * A_V_ROWS)
        o_t = _dot(vloc[vs], p_loc) + _dot(vctx[vs], p_ctx)
        denom = o_t[A_HEAD_DIM:A_HEAD_DIM + 1] + jnp.exp2(sink - m)
        parts.append(o_t[0:A_HEAD_DIM] / denom)
        if len(parts) == per:
            c = hd // per
            o_ref[:, c * LANES:(c + 1) * LANES] = jnp.concatenate(parts, axis=0).T.astype(BF16)
            parts = []
        cur = nxt


def _attn_a(qa, ka, va, sink, lc):
    B, qw, T = qa.shape
    NT = T // TM
    seq = T - lc
    kvw = ka.shape[2]
    vrows = va.shape[1]
    per_tm = TM // A_WINDOW
    last = T // A_WINDOW - 1
    prev_i = lambda t: jnp.maximum(t * per_tm - 1, 0)
    next_i = lambda t: jnp.minimum((t + 1) * per_tm, last)
    kprev = pl.BlockSpec((None, A_WINDOW, kvw), lambda b, t: (b, prev_i(t), 0))
    kmain = pl.BlockSpec((None, TM, kvw), lambda b, t: (b, t, 0))
    knext = pl.BlockSpec((None, A_WINDOW, kvw), lambda b, t: (b, next_i(t), 0))
    kctx = pl.BlockSpec((None, lc, kvw), lambda b, t: (b, 0, 0))
    vprev = pl.BlockSpec((None, vrows, A_WINDOW), lambda b, t: (b, 0, prev_i(t)))
    vmain = pl.BlockSpec((None, vrows, TM), lambda b, t: (b, 0, t))
    vnext = pl.BlockSpec((None, vrows, A_WINDOW), lambda b, t: (b, 0, next_i(t)))
    vctx = pl.BlockSpec((None, vrows, lc), lambda b, t: (b, 0, 0))
    return pl.pallas_call(
        functools.partial(_attn_a_kernel, seq=seq),
        grid=(B, NT),
        in_specs=[pl.BlockSpec((None, qw, TM), lambda b, t: (b, 0, t)),
                  kprev, kmain, knext, kctx, vprev, vmain, vnext, vctx,
                  pl.BlockSpec(sink.shape, lambda b, t: (0, 0))],
        out_specs=pl.BlockSpec((None, TM, qw), lambda b, t: (b, t, 0)),
        out_shape=jax.ShapeDtypeStruct((B, T, qw), BF16),
        compiler_params=pltpu.CompilerParams(dimension_semantics=("parallel", "parallel")),
    )(qa, ka, ka, ka, ka, va, va, va, va, sink)


def _mla_kernel(q_ref, k_ref, v_ref, o_ref, m_ref, acc_ref, sa_ref, sb_ref, *, lc, tk):
    t = pl.program_id(1)
    total = k_ref.shape[0]
    nchunks = jnp.where(t == 0, lc // tk, total // tk)
    m_ref[...] = jnp.full(m_ref.shape, _NEG_INF, F32)
    acc_ref[...] = jnp.zeros(acc_ref.shape, F32)

    def stage(dst_ref, j, hd):
        qs = slice(hd * B_HEAD_PAD, (hd + 1) * B_HEAD_PAD)
        s = _dot(k_ref[pl.ds(pl.multiple_of(j * tk, tk), tk), qs], q_ref[qs, :])
        dst_ref[hd, 0:tk, :] = s
        dst_ref[hd, tk:tk + SUBLANES, :] = jnp.broadcast_to(jnp.max(s, axis=0, keepdims=True), (SUBLANES, TM))

    for hd in range(B_HEADS):
        stage(sa_ref, 0, hd)

    def step(j, cur_ref, nxt_ref, last=False):
        for hd in range(B_HEADS):
            vs = slice(hd * B_V_ROWS, (hd + 1) * B_V_ROWS)
            if not last:
                stage(nxt_ref, j + 1, hd)
            m_prev = m_ref[hd]
            m_new = jnp.maximum(m_prev, cur_ref[hd, tk:tk + SUBLANES, :])
            alpha = jnp.exp2(m_prev - m_new)
            p = jnp.exp2((cur_ref[hd, 0:tk, :] - m_new[0:1]).astype(BF16))
            m_ref[hd] = m_new
            acc_ref[vs, :] = acc_ref[vs, :] * alpha[0:1] + _dot(v_ref[j, vs, :], p)

    bufs = (sa_ref, sb_ref)

    def body(g, carry):
        for i in range(MLA_UNROLL):
            step(MLA_UNROLL * g + i, bufs[i % 2], bufs[(i + 1) % 2])
        return carry

    ngroups = (nchunks - 1) // MLA_UNROLL
    lax.fori_loop(0, ngroups, body, 0)
    tail = (total // tk - 1) % MLA_UNROLL + 1
    assert tail == (lc // tk - 1) % MLA_UNROLL + 1
    for i in range(tail):
        step(MLA_UNROLL * ngroups + i, bufs[i % 2], bufs[(i + 1) % 2], last=(i == tail - 1))
    per = LANES // B_V_DIM
    for c in range(B_HEADS // per):
        o_t = jnp.concatenate(
            [acc_ref[hd * B_V_ROWS:hd * B_V_ROWS + B_V_DIM, :]
             / acc_ref[hd * B_V_ROWS + B_V_DIM:hd * B_V_ROWS + B_V_DIM + 1, :]
             for hd in range(c * per, (c + 1) * per)], axis=0)
        o_ref[:, c * LANES:(c + 1) * LANES] = o_t.T.astype(BF16)


def _mla(qb, kb, vb, lc):
    B, qw, T = qb.shape
    NT = T // TM
    _, nck, vw, tk = vb.shape
    ow = B_HEADS * B_V_DIM
    vmem = (T * (qw + vw) * 2 + 4 * TM * qw * 2 + (B_HEADS * SUBLANES * TM + 3 * TM * vw) * 4
            + 2 * B_HEADS * (tk + SUBLANES) * TM * 4 + (16 << 20))
    return pl.pallas_call(
        functools.partial(_mla_kernel, lc=lc, tk=tk),
        grid=(B, NT),
        in_specs=[pl.BlockSpec((None, qw, TM), lambda b, t: (b, 0, t)),
                  pl.BlockSpec((None, T, qw), lambda b, t: (b, 0, 0), pipeline_mode=pl.Buffered(1)),
                  pl.BlockSpec((None, nck, vw, tk), lambda b, t: (b, 0, 0, 0), pipeline_mode=pl.Buffered(1))],
        out_specs=pl.BlockSpec((None, TM, ow), lambda b, t: (b, t, 0)),
        out_shape=jax.ShapeDtypeStruct((B, T, ow), BF16),
        scratch_shapes=[pltpu.VMEM((B_HEADS, SUBLANES, TM), F32),
                        pltpu.VMEM((vw, TM), F32),
                        pltpu.VMEM((B_HEADS, tk + SUBLANES, TM), F32),
                        pltpu.VMEM((B_HEADS, tk + SUBLANES, TM), F32)],
        compiler_params=pltpu.CompilerParams(
            dimension_semantics=("parallel", "arbitrary"), vmem_limit_bytes=_vmem_limit(vmem)),
    )(qb, kb, vb)


def _gla_prep(q, k, b, reverse):
    C = q.shape[0]
    dk2 = q.shape[1]
    nsub = C // GLA_SUB
    b_end = b[0:1] if reverse else b[C - 1:C]
    row = lax.broadcasted_iota(jnp.int32, (C, dk2), 0)
    lane = lax.broadcasted_iota(jnp.int32, (C, dk2), 1)

    qe = (q * jnp.exp(b)).astype(BF16)
    kd_t = (k * jnp.exp(b_end - b)).T.astype(BF16)
    dec = jnp.exp(jnp.broadcast_to(b_end, (dk2, dk2)).T)
    dec2 = jnp.concatenate([dec] * (2 * C_VAL_DIM // dk2), axis=1)

    refs = [blk * GLA_SUB + (GLA_SUB - 1 if reverse else 0) for blk in range(nsub)]
    b_ref = jnp.concatenate([jnp.broadcast_to(b[n:n + 1], (GLA_SUB, dk2)) for n in refs], axis=0)
    q_sc = q * jnp.exp(b - b_ref)
    k_parts = []
    for blk, n in enumerate(refs):
        in_range = (row >= blk * GLA_SUB) if reverse else (row < (blk + 1) * GLA_SUB)
        k_parts.append(jnp.where(in_range, k * jnp.exp(b[n:n + 1] - b), 0.0))
    k_big = jnp.concatenate(k_parts, axis=1).astype(BF16)
    q_halves = []
    for hf in range(2):
        in_half = (lane >= hf * C_KEY_DIM) & (lane < (hf + 1) * C_KEY_DIM)
        q_halves.append(jnp.concatenate(
            [jnp.where(in_half & (row >= blk * GLA_SUB) & (row < (blk + 1) * GLA_SUB), q_sc, 0.0)
             for blk in range(nsub)], axis=1))
    q_big = jnp.concatenate(q_halves, axis=0).astype(BF16)
    return qe, kd_t, dec2, q_big, k_big


def _gla_kernel(qf_ref, kf_ref, vf_ref, gf_ref, qr_ref, kr_ref, vr_ref, gr_ref, of_ref, or_ref,
                sf_ref, sr_ref):
    t = pl.program_id(1)

    @pl.when(t == 0)
    def _():
        sf_ref[...] = jnp.zeros(sf_ref.shape, F32)
        sr_ref[...] = jnp.zeros(sr_ref.shape, F32)

    C = C_CHUNK
    nchunk = TM // C
    npair = C_HEADS // 2
    kw, vw = 2 * C_KEY_DIM, 2 * C_VAL_DIM
    dirs = ((qf_ref, kf_ref, vf_ref, gf_ref, of_ref, sf_ref, False),
            (qr_ref, kr_ref, vr_ref, gr_ref, or_ref, sr_ref, True))

    ii = lax.broadcasted_iota(jnp.int32, (TM, TM), 0)
    jj = lax.broadcasted_iota(jnp.int32, (TM, TM), 1)
    same_chunk = (ii // C) == (jj // C)
    b_all = []
    for (_, _, _, g_ref, _, _, reverse) in dirs:
        tri = jnp.where(same_chunk & ((jj >= ii) if reverse else (jj <= ii)), 1.0, 0.0).astype(BF16)
        g_hi, g_mid, g_lo = _split_bf16(g_ref[...], 3)
        b_all.append(_dot(tri, g_hi) + (_dot(tri, g_mid) + _dot(tri, g_lo)))

    units = []
    for d, (q_ref, k_ref, v_ref, _, o_ref, _, reverse) in enumerate(dirs):
        for p in range(npair):
            ks, vsl = slice(p * kw, (p + 1) * kw), slice(p * vw, (p + 1) * vw)
            for c in range(nchunk):
                rows = slice(c * C, (c + 1) * C)
                prep = _gla_prep(q_ref[rows, ks], k_ref[rows, ks], b_all[d][rows, ks], reverse)
                units.append((d, p, c, rows, vsl, v_ref, o_ref, reverse, prep))

    srow = lax.broadcasted_iota(jnp.int32, (kw, vw), 0)
    scol = lax.broadcasted_iota(jnp.int32, (kw, vw), 1)
    on_diag = (srow < C_KEY_DIM) == (scol < C_VAL_DIM)
    qi = lax.broadcasted_iota(jnp.int32, (2 * C, C), 0) % C
    kj = lax.broadcasted_iota(jnp.int32, (2 * C, C), 1)
    kvs, atts = [], []
    for (d, p, c, rows, vsl, v_ref, o_ref, reverse, prep) in units:
        _, kd_t, _, q_big, k_big = prep
        kvs.append(jnp.where(on_diag, _dot(kd_t, v_ref[rows, vsl]), 0.0))
        keep = (kj >= qi) if reverse else (kj <= qi)
        atts.append(jnp.where(keep, _dot_nt(q_big, k_big), 0.0).astype(BF16))

    intra = {}
    for u, (d, p, c, rows, vsl, v_ref, o_ref, reverse, prep) in enumerate(units):
        pv = _dot(atts[u], v_ref[rows, vsl])
        intra[(d, p, c)] = (u, jnp.concatenate([pv[0:C, 0:C_VAL_DIM], pv[C:2 * C, C_VAL_DIM:]], axis=1))

    states = {(d, p): dirs[d][5][p] for d in range(len(dirs)) for p in range(npair)}
    for step in range(nchunk):
        for d in range(len(dirs)):
            reverse = dirs[d][6]
            c = nchunk - 1 - step if reverse else step
            for p in range(npair):
                u, o_intra = intra[(d, p, c)]
                _, _, _, rows, vsl, _, o_ref, _, prep = units[u]
                qe, _, dec2, _, _ = prep
                o_ref[rows, vsl] = o_intra + _dot(qe, states[(d, p)].astype(BF16))
                states[(d, p)] = dec2 * states[(d, p)] + kvs[u]
    for d in range(len(dirs)):
        for p in range(npair):
            dirs[d][5][p] = states[(d, p)]


def _gla(cq, ck, cv, g):
    B, T, kw = cq.shape
    NT = T // TM
    vw = cv.shape[2]

    def fwd(n, col=0):
        return pl.BlockSpec((None, TM, n), lambda b, t: (b, t, col))

    def rev(n, col=0):
        return pl.BlockSpec((None, TM, n), lambda b, t: (b, jnp.where(t == 0, 0, NT - t), col))

    return pl.pallas_call(
        _gla_kernel,
        grid=(B, NT),
        in_specs=[fwd(kw), fwd(kw), fwd(vw), fwd(kw, 0), rev(kw), rev(kw), rev(vw), rev(kw, 1)],
        out_specs=[fwd(vw), rev(vw)],
        out_shape=[jax.ShapeDtypeStruct((B, T, vw), F32)] * 2,
        scratch_shapes=[pltpu.VMEM((C_HEADS // 2, 2 * C_KEY_DIM, 2 * C_VAL_DIM), F32)] * 2,
        compiler_params=pltpu.CompilerParams(dimension_semantics=("parallel", "arbitrary")),
    )(cq, ck, cv, g, cq, ck, cv, g)


def _merge_kernel(xc_ref, xl_ref, mod_ref, ya_ref, yb_ref, of_ref, or_ref, cr_ref, gate_ref, hn_ref,
                  wa_ref, wb_ref, wc_ref, wo_ref, o_ref):
    d = xl_ref.shape[1]
    o = of_ref[...] + or_ref[...]
    r = cr_ref[...].astype(F32)
    parts = []
    for hd in range(C_HEADS):
        sl = slice(hd * C_VAL_DIM, (hd + 1) * C_VAL_DIM)
        parts.append(_rms(o[:, sl], hn_ref[:, sl]))
    yc = (jnp.concatenate(parts, axis=1) * (r * jax.nn.sigmoid(r))).astype(BF16)

    def gate(i):
        return jax.nn.sigmoid(gate_ref[:, i * d:(i + 1) * d].astype(F32))

    m = (gate(0) * _dot(ya_ref[...], wa_ref[...]) + gate(1) * _dot(yb_ref[...], wb_ref[...])
         + gate(2) * _dot(yc, wc_ref[...]))
    y = _dot(m.astype(BF16), wo_ref[...])
    o_ref[...] = _stream_tile(xc_ref, xl_ref) + mod_ref[2:3, :] * y


def _merge(Xc, Xl, mod, ya, yb, of, orv, cr, gates, hn, layer, wa, wb, wc, wo):
    spec_c, spec_l, T = _stream_specs(Xc, Xl)
    B, _, D = Xl.shape
    NT = T // TM
    row = lambda n: pl.BlockSpec((None, TM, n), lambda b, t: (b, t, 0))
    wbytes = sum(w.size // w.shape[0] for w in (wa, wb, wc, wo)) * 2
    vmem = wbytes + 2 * TM * (2 * D + 3 * D + 4 * ya.shape[2]) * 4 + (16 << 20)
    return pl.pallas_call(
        _merge_kernel,
        grid=(B, NT),
        in_specs=[spec_c, spec_l, pl.BlockSpec((None, 6, D), _mod_index),
                  row(ya.shape[2]), row(yb.shape[2]), row(of.shape[2]), row(orv.shape[2]),
                  row(cr.shape[2]), row(gates.shape[2]), _const_spec(hn.shape),
                  _layer_spec(wa, layer), _layer_spec(wb, layer), _layer_spec(wc, layer), _layer_spec(wo, layer)],
        out_specs=row(D),
        out_shape=jax.ShapeDtypeStruct((B, T, D), F32),
        compiler_params=pltpu.CompilerParams(
            dimension_semantics=("parallel", "parallel"), vmem_limit_bytes=_vmem_limit(vmem)),
    )(Xc, Xl, mod, ya, yb, of, orv, cr, gates, hn, wa, wb, wc, wo)


def _ffn_kernel(x_ref, xp_ref, xn_ref, mod_ref, nw_ref, wup_ref, cw_ref, cb_ref, wdn_ref, fw_ref, o_ref,
                u_ref, act_ref, *, nt, ncb, t0, final):
    t = pl.program_id(1) + t0
    mod = mod_ref[...]
    nw = nw_ref[...]
    halo = xp_ref.shape[0]
    ffn = wdn_ref.shape[0]
    cb = ffn // ncb

    def norm(xv):
        return _rms(xv, nw) * (1.0 + mod[4:5]) + mod[3:4]

    has_prev = (t >= 2)
    has_next = (t >= 1) & (t < nt - 1)
    hb = jnp.concatenate([jnp.where(has_prev, norm(xp_ref[...]), 0.0), norm(x_ref[...]),
                          jnp.where(has_next, norm(xn_ref[...]), 0.0)], axis=0).astype(BF16)

    def up(j):
        u_ref[j % 2, 0] = _dot(hb, wup_ref[:, j * cb:(j + 1) * cb])
        u_ref[j % 2, 1] = _dot(hb, wup_ref[:, ffn + j * cb:ffn + (j + 1) * cb])

    def conv(slot, half, lo):
        w = cw_ref[:, lo:lo + cb]
        return (u_ref[slot, half, halo - 1:halo - 1 + TM, :] * w[0:1] + u_ref[slot, half, halo:halo + TM, :] * w[1:2]
                + u_ref[slot, half, halo + 1:halo + 1 + TM, :] * w[2:3] + cb_ref[:, lo:lo + cb])

    def act(j):
        gt = conv(j % 2, 0, j * cb)
        val = conv(j % 2, 1, ffn + j * cb)
        act_ref[:, j * cb:(j + 1) * cb] = (gt * jax.nn.sigmoid(gt) * val).astype(BF16)

    up(0)
    for j in range(1, ncb):
        up(j)
        act(j - 1)
    act(ncb - 1)
    y = x_ref[...] + mod[5:6] * _dot(act_ref[...], wdn_ref[...])
    o_ref[...] = _rms(y, fw_ref[...]) if final else y


def _ffn(X, mod, nw, layer, wup, cw, cb, wdn, fw, final):
    B, T, D = X.shape
    NT = T // TM
    t0 = 1 if final else 0
    halo = SUBLANES
    per = TM // halo
    last = T // halo - 1
    ffn = wdn.shape[1]
    cbw = FFN_COLS
    ncb = ffn // cbw
    vmem = 3 * D * ffn * 2 + 4 * (TM + 2 * halo) * cbw * 4 + TM * ffn * 2 + 8 * TM * D * 4 + (16 << 20)
    return pl.pallas_call(
        functools.partial(_ffn_kernel, nt=NT, ncb=ncb, t0=t0, final=final),
        grid=(B, NT - t0),
        in_specs=[pl.BlockSpec((None, TM, D), lambda b, t: (b, t + t0, 0)),
                  pl.BlockSpec((None, halo, D), lambda b, t: (b, jnp.maximum((t + t0) * per - 1, 0), 0)),
                  pl.BlockSpec((None, halo, D), lambda b, t: (b, jnp.minimum((t + t0 + 1) * per, last), 0)),
                  pl.BlockSpec((None, 6, D), lambda b, t: _mod_index(b, t + t0)),
                  _const_spec(nw.shape), _layer_spec(wup, layer), _const_spec(cw.shape), _const_spec(cb.shape),
                  _layer_spec(wdn, layer), _const_spec(fw.shape)],
        out_specs=pl.BlockSpec((None, TM, D), lambda b, t: (b, t, 0)),
        out_shape=jax.ShapeDtypeStruct((B, T - t0 * TM, D), F32),
        scratch_shapes=[pltpu.VMEM((2, 2, TM + 2 * halo, cbw), F32), pltpu.VMEM((TM, ffn), BF16)],
        compiler_params=pltpu.CompilerParams(
            dimension_semantics=("parallel", "parallel"), vmem_limit_bytes=_vmem_limit(vmem)),
    )(X, X, X, mod, nw, wup, cw, cb, wdn, fw)


def _rope_tables(seq, lc):
    rows = seq // GRID_W
    row = jnp.broadcast_to(jnp.arange(rows)[:, None], (rows, GRID_W)).reshape(-1).astype(F32)
    col = jnp.broadcast_to(jnp.arange(GRID_W)[None, :], (rows, GRID_W)).reshape(-1).astype(F32)

    def cs(rot_dim):
        n_freq = rot_dim // 4
        inv = ROPE_BASE ** (-jnp.arange(n_freq, dtype=F32) / n_freq)
        ang = jnp.concatenate([row[:, None] * inv, col[:, None] * inv], axis=-1)
        return jnp.cos(ang), jnp.sin(ang)

    ca, sa = cs(A_HEAD_DIM)
    cb, sb = cs(B_ROPE_DIM)
    reps = LANES // A_HEAD_DIM
    cos_a = jnp.tile(jnp.concatenate([ca, ca], axis=1), (1, reps))
    sin_a = jnp.tile(jnp.concatenate([-sa, sa], axis=1), (1, reps))
    one = jnp.ones((seq, B_NOPE_DIM), F32)
    pad = LANES - B_NOPE_DIM - B_ROPE_DIM
    cos_b = jnp.concatenate([one, cb, cb, jnp.ones((seq, pad), F32)], axis=1)
    sin_b = jnp.concatenate([0 * one, -sb, sb, jnp.zeros((seq, pad), F32)], axis=1)
    ident = jnp.stack([jnp.ones((lc, LANES), F32), jnp.zeros((lc, LANES), F32)] * 2)
    return jnp.concatenate([ident, jnp.stack([cos_a, sin_a, cos_b, sin_b])], axis=1)


def _prep_w_in(w_in):
    o_kr = _C_BKR
    o_cq = o_kr + B_ROPE_DIM
    o_cg = o_cq + (_C_CG - _C_CQ)
    o_gate = o_cg + 2 * C_GATE_RANK
    kr = jnp.pad(w_in[:, :, o_kr:o_cq], ((0, 0), (0, 0), (B_NOPE_DIM, LANES - B_NOPE_DIM - B_ROPE_DIM)))
    cg = jnp.pad(w_in[:, :, o_cg:o_gate], ((0, 0), (0, 0), (0, LANES - 2 * C_GATE_RANK)))
    return tuple(p.astype(BF16) for p in (w_in[:, :, :o_kr], kr, w_in[:, :, o_cq:o_cg], cg, w_in[:, :, o_gate:]))


def kernel(x, c, ctx, c_ctx, w_mod, b_mod, norm_mix, norm_ffn, w_in, a_sink, b_q_norm, b_kv_norm, b_w_uq, b_w_ukv, c_w_gate, c_b_gate, c_head_norm, w_br_a, w_br_b, w_br_c, w_out, w_up, conv_w, conv_b, w_down, final_norm):
    B, S, D = x.shape
    lc = ctx.shape[1]
    L = w_mod.shape[0]
    assert lc == TM and S % TM == 0 and S % GRID_W == 0 and B + 1 <= SUBLANES

    Xc, Xl = ctx, x
    cvec = jnp.zeros((SUBLANES, D), F32).at[0].set(c_ctx).at[1:B + 1].set(c)
    mods = _modulation(cvec, w_mod, b_mod).reshape(L, SUBLANES, 6, D)

    rope = _rope_tables(S, lc)
    w_in_r = _prep_w_in(w_in)
    sink = jnp.broadcast_to(a_sink[:, :, None], (L, A_HEADS, LANES)).astype(F32)
    wa, wb, wc, wo = w_br_a.astype(BF16), w_br_b.astype(BF16), w_br_c.astype(BF16), w_out.astype(BF16)
    qk = B_NOPE_DIM + B_ROPE_DIM
    wuq = jnp.pad(b_w_uq.reshape(L, B_Q_RANK, B_HEADS, qk),
                  ((0, 0), (0, 0), (0, 0), (0, B_HEAD_PAD - qk))).reshape(L, B_Q_RANK, -1).astype(BF16)
    ukv = b_w_ukv.reshape(L, B_KV_RANK, B_HEADS, B_NOPE_DIM + B_V_DIM)
    wkn = jnp.pad(ukv[..., :B_NOPE_DIM],
                  ((0, 0), (0, 0), (0, 0), (0, B_HEAD_PAD - B_NOPE_DIM))).reshape(L, B_KV_RANK, -1).astype(BF16)
    wvb = ukv[..., B_NOPE_DIM:].reshape(L, B_KV_RANK, -1).astype(BF16)
    nqk = C_HEADS * C_KEY_DIM
    wg = jnp.zeros((L, LANES, 2 * nqk), F32)
    wg = wg.at[:, 0:C_GATE_RANK, 0:nqk].set(c_w_gate[:, 0]).at[:, C_GATE_RANK:2 * C_GATE_RANK, nqk:].set(c_w_gate[:, 1])
    bg = c_b_gate.reshape(L, 1, 2 * nqk)
    wup, wdn = w_up.astype(BF16), w_down.astype(BF16)

    for l in range(L):
        mod = mods[l]
        (qa, ka, va, qb, kb, vb, cq, ck, cv, cr, g, gates) = _inproj(
            Xc, Xl, mod, norm_mix[l][None], l, w_in_r, rope, b_q_norm[l][None], b_kv_norm[l][None],
            wuq, wkn, wvb, wg, bg[l])
        ya = _attn_a(qa, ka, va, sink[l], lc)
        yb = _mla(qb, kb, vb, lc)
        of, orv = _gla(cq, ck, cv, g)
        X = _merge(Xc, Xl, mod, ya, yb, of, orv, cr, gates, c_head_norm[l][None], l, wa, wb, wc, wo)
        X = _ffn(X, mod, norm_ffn[l][None], l, wup, conv_w[l], conv_b[l][None], wdn, final_norm[None],
                 final=(l == L - 1))
        Xc = Xl = X
    return X
```

```python
import functools

import jax
import jax.numpy as jnp
from jax import lax
from jax.experimental import pallas as pl
from jax.experimental.pallas import tpu as pltpu

GRID_W = 64
EPS = 1e-6
ROPE_BASE = 10000.0
A_HEADS, A_KV_HEADS, A_HEAD_DIM, A_WINDOW = 8, 2, 64, 128
B_HEADS, B_Q_RANK, B_KV_RANK, B_NOPE_DIM, B_ROPE_DIM, B_V_DIM = 8, 256, 128, 64, 32, 64
C_HEADS, C_KEY_DIM, C_VAL_DIM, C_GATE_RANK, C_GATE_TAU, C_CHUNK = 4, 64, 128, 16, 16.0, 64
CONV_W = 3

LANES = 128
SUBLANES = 8
VMEM_BYTES = 64 * 1024 * 1024

TM = 256
GLA_SUB = 16
MLA_TK = 256
MLA_UNROLL = 16
MOD_COLS = 3072
FFN_COLS = 256
F32 = jnp.float32
BF16 = jnp.bfloat16
_HI = lax.Precision.HIGHEST
_NEG_INF = float("-inf")


def _dot(a, b):
    return jnp.dot(a, b, preferred_element_type=F32)


def _dot_nt(a, b):
    return lax.dot_general(a, b, (((1,), (1,)), ((), ())), preferred_element_type=F32)


def _split_bf16(x, terms):
    parts = []
    for _ in range(terms):
        p = x.astype(BF16)
        parts.append(p)
        x = x - p.astype(F32)
    return parts


def _dot_split(a, b):
    a_hi, a_lo = _split_bf16(a, 2)
    b_hi, b_lo = _split_bf16(b, 2)
    return _dot(a_hi, b_hi) + (_dot(a_lo, b_hi) + _dot(a_hi, b_lo))


def _rms(x, w):
    return x * lax.rsqrt(jnp.mean(x * x, axis=-1, keepdims=True) + EPS) * w


def _swap_lane_groups(x, half):
    lane = lax.broadcasted_iota(jnp.int32, x.shape, 1)
    up = pltpu.roll(x, LANES - half, 1)
    down = pltpu.roll(x, half, 1)
    return jnp.where((lane & half) == 0, up, down)


def _vmem_limit(nbytes):
    return int(min(VMEM_BYTES - (4 << 20), max(nbytes, 32 << 20)))


def _layer_spec(stacked, layer):
    shape = stacked.shape[1:]
    nd = len(shape)
    return pl.BlockSpec((None,) + shape, lambda *_: (layer,) + (0,) * nd, pipeline_mode=pl.Buffered(1))


def _const_spec(shape):
    nd = len(shape)
    return pl.BlockSpec(shape, lambda *_: (0,) * nd, pipeline_mode=pl.Buffered(1))


def _mod_kernel(c_ref, w_ref, b_ref, o_ref):
    cv = c_ref[...]
    sc = cv * jax.nn.sigmoid(cv)
    o_ref[...] = jnp.dot(sc, w_ref[...], precision=_HI, preferred_element_type=F32) + b_ref[...]


def _modulation(cvec, w_mod, b_mod):
    L, D, N = w_mod.shape
    nb = N // MOD_COLS
    return pl.pallas_call(
        _mod_kernel,
        grid=(L, nb),
        in_specs=[
            pl.BlockSpec((SUBLANES, D), lambda l, j: (0, 0)),
            pl.BlockSpec((None, D, MOD_COLS), lambda l, j: (l, 0, j)),
            pl.BlockSpec((None, 1, MOD_COLS), lambda l, j: (l, 0, j)),
        ],
        out_specs=pl.BlockSpec((None, SUBLANES, MOD_COLS), lambda l, j: (l, 0, j)),
        out_shape=jax.ShapeDtypeStruct((L, SUBLANES, N), F32),
        compiler_params=pltpu.CompilerParams(
            dimension_semantics=("parallel", "parallel"),
            vmem_limit_bytes=_vmem_limit(3 * D * MOD_COLS * 4)),
    )(cvec, w_mod, b_mod.reshape(L, 1, N))


_C_AQ = 0
_C_AK = _C_AQ + A_HEADS * A_HEAD_DIM
_C_AV = _C_AK + A_KV_HEADS * A_HEAD_DIM
_C_BQ = _C_AV + A_KV_HEADS * A_HEAD_DIM
_C_BKV = _C_BQ + B_Q_RANK
_C_BKR = _C_BKV + B_KV_RANK
_C_CQ = _C_BKR + LANES
_C_CK = _C_CQ + C_HEADS * C_KEY_DIM
_C_CV = _C_CK + C_HEADS * C_KEY_DIM
_C_CR = _C_CV + C_HEADS * C_VAL_DIM
_C_CG = _C_CR + C_HEADS * C_VAL_DIM
_C_GATE = _C_CG + LANES
B_HEAD_PAD = LANES
B_V_ROWS = B_V_DIM + 2 * SUBLANES
A_V_ROWS = A_HEAD_DIM + 2 * SUBLANES
LOG2E = 1.4426950408889634


def _stream_specs(xc, xl):
    off = 0 if xc is xl else 1
    d = xl.shape[2]
    spec_c = pl.BlockSpec((None, TM, d), lambda b, t: (b, 0, 0))
    spec_l = pl.BlockSpec((None, TM, d), lambda b, t: (b, jnp.maximum(t - off, 0), 0))
    return spec_c, spec_l, xl.shape[1] + off * TM


def _stream_tile(xc_ref, xl_ref):
    return jnp.where(pl.program_id(1) == 0, xc_ref[...], xl_ref[...])


def _inproj_kernel(xc_ref, xl_ref, mod_ref, nw_ref, w1_ref, wkr_ref, w2_ref, wcg_ref, w3_ref, rope_ref,
                   bqn_ref, bkvn_ref, wuq_ref, wkn_ref,
                   wvb_ref, wg_ref, bg_ref,
                   qa_ref, ka_ref, va_ref, qb_ref, kb_ref, vb_ref, cq_ref, ck_ref, cv_ref, cr_ref,
                   g_ref, gate_ref, *, d_model):
    mod = mod_ref[...]
    h = _rms(_stream_tile(xc_ref, xl_ref), nw_ref[...]) * (1.0 + mod[1:2]) + mod[0:1]
    hb = h.astype(BF16)

    pieces = ((_C_AQ, w1_ref), (_C_BKR, wkr_ref), (_C_CQ, w2_ref), (_C_CG, wcg_ref), (_C_GATE, w3_ref))

    def proj(lo, hi):
        base, ref = [(b0, r) for b0, r in pieces if b0 <= lo][-1]
        assert hi - base <= ref.shape[1]
        return _dot(hb, ref[:, lo - base:hi - base])

    cos_a, sin_a, cos_b, sin_b = rope_ref[0], rope_ref[1], rope_ref[2], rope_ref[3]

    def rope_a(t):
        return t * cos_a + _swap_lane_groups(t, A_HEAD_DIM // 2) * sin_a

    def rope_b(t):
        return t * cos_b + _swap_lane_groups(t, B_ROPE_DIM // 2) * sin_b

    a_scale = A_HEAD_DIM ** -0.5 * LOG2E
    for c in range(A_HEADS * A_HEAD_DIM // LANES):
        t = proj(_C_AQ + c * LANES, _C_AQ + (c + 1) * LANES)
        qa_ref[c * LANES:(c + 1) * LANES, :] = (rope_a(t) * a_scale).T.astype(BF16)
    ka_ref[...] = rope_a(proj(_C_AK, _C_AV)).astype(BF16)
    va_t = proj(_C_AV, _C_BQ).T.astype(BF16)
    a_ones = jnp.where(lax.broadcasted_iota(jnp.int32, (A_V_ROWS - A_HEAD_DIM, va_t.shape[1]), 0) == 0,
                       1.0, 0.0).astype(BF16)
    for g in range(A_KV_HEADS):
        va_ref[g * A_V_ROWS:g * A_V_ROWS + A_HEAD_DIM, :] = va_t[g * A_HEAD_DIM:(g + 1) * A_HEAD_DIM]
        va_ref[g * A_V_ROWS + A_HEAD_DIM:(g + 1) * A_V_ROWS, :] = a_ones

    b_scale = (B_NOPE_DIM + B_ROPE_DIM) ** -0.5 * LOG2E
    cqn = _rms(proj(_C_BQ, _C_BKV), bqn_ref[...]).astype(BF16)
    ckvn = _rms(proj(_C_BKV, _C_BKR), bkvn_ref[...]).astype(BF16)
    kr = rope_b(proj(_C_BKR, _C_CQ))
    for hd in range(B_HEADS):
        sl = slice(hd * B_HEAD_PAD, (hd + 1) * B_HEAD_PAD)
        qb_ref[sl, :] = (rope_b(_dot(cqn, wuq_ref[:, sl])) * b_scale).T.astype(BF16)
        kb_ref[:, sl] = (_dot(ckvn, wkn_ref[:, sl]) + kr).astype(BF16)
    vb = _dot(ckvn, wvb_ref[...])
    tk = vb_ref.shape[2]
    per = LANES // B_V_DIM
    ones_rows = jnp.where(lax.broadcasted_iota(jnp.int32, (B_V_ROWS - B_V_DIM, tk), 0) == 0, 1.0, 0.0).astype(BF16)
    for cc in range(vb_ref.shape[0]):
        for c in range(vb.shape[1] // LANES):
            v_t = vb[cc * tk:(cc + 1) * tk, c * LANES:(c + 1) * LANES].T.astype(BF16)
            for i in range(per):
                r0 = (c * per + i) * B_V_ROWS
                vb_ref[cc, r0:r0 + B_V_DIM, :] = v_t[i * B_V_DIM:(i + 1) * B_V_DIM]
                vb_ref[cc, r0 + B_V_DIM:r0 + B_V_ROWS, :] = ones_rows

    cq_ref[...] = proj(_C_CQ, _C_CK) * (C_KEY_DIM ** -0.5)
    ck_ref[...] = proj(_C_CK, _C_CV)
    cv_ref[...] = proj(_C_CV, _C_CR).astype(BF16)
    cr_ref[...] = proj(_C_CR, _C_CG).astype(BF16)
    z = _dot_split(proj(_C_CG, _C_GATE), wg_ref[...]) + bg_ref[...]
    g_ref[...] = (jnp.minimum(z, 0.0) - jnp.log1p(jnp.exp(-jnp.abs(z)))) * (1.0 / C_GATE_TAU)

    gate_ref[...] = proj(_C_GATE, _C_GATE + 3 * d_model).astype(BF16)

def _mod_index(b, t):
    return (jnp.where(t == 0, 0, b + 1), 0, 0)


def _inproj(Xc, Xl, mod, nw, layer, w_pieces, rope, bqn, bkvn, wuq, wkn, wvb, wg, bg):
    spec_c, spec_l, T = _stream_specs(Xc, Xl)
    B, _, D = Xl.shape
    NT = T // TM
    n_in = sum(w.shape[2] for w in w_pieces)
    row = lambda n: pl.BlockSpec((None, TM, n), lambda b, t: (b, t, 0))
    widths = [(A_HEADS * A_HEAD_DIM, BF16), (A_KV_HEADS * A_HEAD_DIM, BF16), (A_KV_HEADS * A_HEAD_DIM, BF16),
              (B_HEADS * B_HEAD_PAD, BF16), (B_HEADS * B_HEAD_PAD, BF16), (B_HEADS * B_V_DIM, BF16),
              (C_HEADS * C_KEY_DIM, F32), (C_HEADS * C_KEY_DIM, F32), (C_HEADS * C_VAL_DIM, BF16),
              (C_HEADS * C_VAL_DIM, BF16), (2 * C_HEADS * C_KEY_DIM, F32), (3 * D, BF16)]
    out_bytes = sum(TM * n * jnp.dtype(dt).itemsize for n, dt in widths)
    vmem = D * n_in * 2 + 4 * out_bytes + 6 * TM * D * 4 + (8 << 20)
    out_specs = [row(n) for n, _ in widths]
    out_shape = [jax.ShapeDtypeStruct((B, T, n), dt) for n, dt in widths]
    qw, vw = B_HEADS * B_HEAD_PAD, B_HEADS * B_V_ROWS
    out_specs[3] = pl.BlockSpec((None, qw, TM), lambda b, t: (b, 0, t))
    out_shape[3] = jax.ShapeDtypeStruct((B, qw, T), BF16)
    out_specs[0] = pl.BlockSpec((None, widths[0][0], TM), lambda b, t: (b, 0, t))
    out_shape[0] = jax.ShapeDtypeStruct((B, widths[0][0], T), BF16)
    out_specs[2] = pl.BlockSpec((None, A_KV_HEADS * A_V_ROWS, TM), lambda b, t: (b, 0, t))
    out_shape[2] = jax.ShapeDtypeStruct((B, A_KV_HEADS * A_V_ROWS, T), BF16)
    out_specs[5] = pl.BlockSpec((None, TM // MLA_TK, vw, MLA_TK), lambda b, t: (b, t, 0, 0))
    out_shape[5] = jax.ShapeDtypeStruct((B, T // MLA_TK, vw, MLA_TK), BF16)
    return pl.pallas_call(
        functools.partial(_inproj_kernel, d_model=D),
        grid=(B, NT),
        in_specs=[
            spec_c, spec_l,
            pl.BlockSpec((None, 6, D), _mod_index),
            _const_spec((1, D)),
            *[_layer_spec(w, layer) for w in w_pieces],
            pl.BlockSpec((4, TM, LANES), lambda b, t: (0, t, 0)),
            _const_spec(bqn.shape), _const_spec(bkvn.shape), _layer_spec(wuq, layer),
            _layer_spec(wkn, layer), _layer_spec(wvb, layer), _layer_spec(wg, layer), _const_spec(bg.shape),
        ],
        out_specs=out_specs,
        out_shape=out_shape,
        compiler_params=pltpu.CompilerParams(
            dimension_semantics=("parallel", "parallel"), vmem_limit_bytes=_vmem_limit(vmem)),
    )(Xc, Xl, mod, nw, *w_pieces, rope, bqn, bkvn, wuq, wkn, wvb, wg, bg)


def _attn_a_kernel(q_ref, kp_ref, km_ref, kn_ref, kc_ref, vp_ref, vm_ref, vn_ref, vc_ref, sink_ref,
                   o_ref, *, seq):
    t = pl.program_id(1)
    kloc = jnp.concatenate([kp_ref[...], km_ref[...], kn_ref[...]], axis=0)
    vloc = jnp.concatenate([vp_ref[...], vm_ref[...], vn_ref[...]], axis=1)
    kctx, vctx = kc_ref[...], vc_ref[...]
    nloc = kloc.shape[0]
    r = lax.broadcasted_iota(jnp.int32, (nloc, TM), 0)
    c = lax.broadcasted_iota(jnp.int32, (nloc, TM), 1)
    rel = r - A_WINDOW - c
    kpos = (t - 1) * TM - A_WINDOW + r
    valid = (jnp.abs(rel) <= A_WINDOW) & (kpos >= 0) & (kpos < seq) & (t >= 1)
    group = A_HEADS // A_KV_HEADS
    per = LANES // A_HEAD_DIM
    q_zero = jnp.zeros((A_HEAD_DIM, TM), BF16)

    def scores(hd):
        q_t = q_ref[hd * A_HEAD_DIM:(hd + 1) * A_HEAD_DIM, :]
        g = hd // group
        qm = jnp.concatenate([q_t if i == g else q_zero for i in range(A_KV_HEADS)], axis=0)
        return _dot(kloc, qm), _dot(kctx, qm)

    cur = scores(0)
    parts = []
    for hd in range(A_HEADS):
        nxt = scores(hd + 1) if hd + 1 < A_HEADS else None
        s_loc = jnp.where(valid, cur[0], _NEG_INF)
        s_ctx = cur[1]
        sink = sink_ref[hd:hd + 1, 0:1] * LOG2E
        m = jnp.maximum(jnp.maximum(jnp.max(s_loc, axis=0, keepdims=True),
                                    jnp.max(s_ctx, axis=0, keepdims=True)), sink)
        p_loc = jnp.exp2((s_loc - m).astype(BF16))
        p_ctx = jnp.exp2((s_ctx - m).astype(BF16))
        g = hd // group
        vs = slice(g * A_V_ROWS, (g + 1) * A_V_ROWS)
        o_t = _dot(vloc[vs], p_loc) + _dot(vctx[vs], p_ctx)
        denom = o_t[A_HEAD_DIM:A_HEAD_DIM + 1] + jnp.exp2(sink - m)
        parts.append(o_t[0:A_HEAD_DIM] / denom)
        if len(parts) == per:
            c = hd // per
            o_ref[:, c * LANES:(c + 1) * LANES] = jnp.concatenate(parts, axis=0).T.astype(BF16)
            parts = []
        cur = nxt


def _attn_a(qa, ka, va, sink, lc):
    B, qw, T = qa.shape
    NT = T // TM
    seq = T - lc
    kvw = ka.shape[2]
    vrows = va.shape[1]
    per_tm = TM // A_WINDOW
    last = T // A_WINDOW - 1
    prev_i = lambda t: jnp.maximum(t * per_tm - 1, 0)
    next_i = lambda t: jnp.minimum((t + 1) * per_tm, last)
    kprev = pl.BlockSpec((None, A_WINDOW, kvw), lambda b, t: (b, prev_i(t), 0))
    kmain = pl.BlockSpec((None, TM, kvw), lambda b, t: (b, t, 0))
    knext = pl.BlockSpec((None, A_WINDOW, kvw), lambda b, t: (b, next_i(t), 0))
    kctx = pl.BlockSpec((None, lc, kvw), lambda b, t: (b, 0, 0))
    vprev = pl.BlockSpec((None, vrows, A_WINDOW), lambda b, t: (b, 0, prev_i(t)))
    vmain = pl.BlockSpec((None, vrows, TM), lambda b, t: (b, 0, t))
    vnext = pl.BlockSpec((None, vrows, A_WINDOW), lambda b, t: (b, 0, next_i(t)))
    vctx = pl.BlockSpec((None, vrows, lc), lambda b, t: (b, 0, 0))
    return pl.pallas_call(
        functools.partial(_attn_a_kernel, seq=seq),
        grid=(B, NT),
        in_specs=[pl.BlockSpec((None, qw, TM), lambda b, t: (b, 0, t)),
                  kprev, kmain, knext, kctx, vprev, vmain, vnext, vctx,
                  pl.BlockSpec(sink.shape, lambda b, t: (0, 0))],
        out_specs=pl.BlockSpec((None, TM, qw), lambda b, t: (b, t, 0)),
        out_shape=jax.ShapeDtypeStruct((B, T, qw), BF16),
        compiler_params=pltpu.CompilerParams(dimension_semantics=("parallel", "parallel")),
    )(qa, ka, ka, ka, ka, va, va, va, va, sink)


def _mla_kernel(q_ref, k_ref, v_ref, o_ref, m_ref, acc_ref, sa_ref, sb_ref, *, lc, tk):
    t = pl.program_id(1)
    total = k_ref.shape[0]
    nchunks = jnp.where(t == 0, lc // tk, total // tk)
    m_ref[...] = jnp.full(m_ref.shape, _NEG_INF, F32)
    acc_ref[...] = jnp.zeros(acc_ref.shape, F32)

    def stage(dst_ref, j, hd):
        qs = slice(hd * B_HEAD_PAD, (hd + 1) * B_HEAD_PAD)
        s = _dot(k_ref[pl.ds(pl.multiple_of(j * tk, tk), tk), qs], q_ref[qs, :])
        dst_ref[hd, 0:tk, :] = s
        dst_ref[hd, tk:tk + SUBLANES, :] = jnp.broadcast_to(jnp.max(s, axis=0, keepdims=True), (SUBLANES, TM))

    for hd in range(B_HEADS):
        stage(sa_ref, 0, hd)

    def step(j, cur_ref, nxt_ref, last=False):
        for hd in range(B_HEADS):
            vs = slice(hd * B_V_ROWS, (hd + 1) * B_V_ROWS)
            if not last:
                stage(nxt_ref, j + 1, hd)
            m_prev = m_ref[hd]
            m_new = jnp.maximum(m_prev, cur_ref[hd, tk:tk + SUBLANES, :])
            alpha = jnp.exp2(m_prev - m_new)
            p = jnp.exp2((cur_ref[hd, 0:tk, :] - m_new[0:1]).astype(BF16))
            m_ref[hd] = m_new
            acc_ref[vs, :] = acc_ref[vs, :] * alpha[0:1] + _dot(v_ref[j, vs, :], p)

    bufs = (sa_ref, sb_ref)

    def body(g, carry):
        for i in range(MLA_UNROLL):
            step(MLA_UNROLL * g + i, bufs[i % 2], bufs[(i + 1) % 2])
        return carry

    ngroups = (nchunks - 1) // MLA_UNROLL
    lax.fori_loop(0, ngroups, body, 0)
    tail = (total // tk - 1) % MLA_UNROLL + 1
    assert tail == (lc // tk - 1) % MLA_UNROLL + 1
    for i in range(tail):
        step(MLA_UNROLL * ngroups + i, bufs[i % 2], bufs[(i + 1) % 2], last=(i == tail - 1))
    per = LANES // B_V_DIM
    for c in range(B_HEADS // per):
        o_t = jnp.concatenate(
            [acc_ref[hd * B_V_ROWS:hd * B_V_ROWS + B_V_DIM, :]
             / acc_ref[hd * B_V_ROWS + B_V_DIM:hd * B_V_ROWS + B_V_DIM + 1, :]
             for hd in range(c * per, (c + 1) * per)], axis=0)
        o_ref[:, c * LANES:(c + 1) * LANES] = o_t.T.astype(BF16)


def _mla(qb, kb, vb, lc):
    B, qw, T = qb.shape
    NT = T // TM
    _, nck, vw, tk = vb.shape
    ow = B_HEADS * B_V_DIM
    vmem = (T * (qw + vw) * 2 + 4 * TM * qw * 2 + (B_HEADS * SUBLANES * TM + 3 * TM * vw) * 4
            + 2 * B_HEADS * (tk + SUBLANES) * TM * 4 + (16 << 20))
    return pl.pallas_call(
        functools.partial(_mla_kernel, lc=lc, tk=tk),
        grid=(B, NT),
        in_specs=[pl.BlockSpec((None, qw, TM), lambda b, t: (b, 0, t)),
                  pl.BlockSpec((None, T, qw), lambda b, t: (b, 0, 0), pipeline_mode=pl.Buffered(1)),
                  pl.BlockSpec((None, nck, vw, tk), lambda b, t: (b, 0, 0, 0), pipeline_mode=pl.Buffered(1))],
        out_specs=pl.BlockSpec((None, TM, ow), lambda b, t: (b, t, 0)),
        out_shape=jax.ShapeDtypeStruct((B, T, ow), BF16),
        scratch_shapes=[pltpu.VMEM((B_HEADS, SUBLANES, TM), F32),
                        pltpu.VMEM((vw, TM), F32),
                        pltpu.VMEM((B_HEADS, tk + SUBLANES, TM), F32),
                        pltpu.VMEM((B_HEADS, tk + SUBLANES, TM), F32)],
        compiler_params=pltpu.CompilerParams(
            dimension_semantics=("parallel", "arbitrary"), vmem_limit_bytes=_vmem_limit(vmem)),
    )(qb, kb, vb)


def _gla_prep(q, k, b, reverse):
    C = q.shape[0]
    dk2 = q.shape[1]
    nsub = C // GLA_SUB
    b_end = b[0:1] if reverse else b[C - 1:C]
    row = lax.broadcasted_iota(jnp.int32, (C, dk2), 0)
    lane = lax.broadcasted_iota(jnp.int32, (C, dk2), 1)

    qe = (q * jnp.exp(b)).astype(BF16)
    kd_t = (k * jnp.exp(b_end - b)).T.astype(BF16)
    dec = jnp.exp(jnp.broadcast_to(b_end, (dk2, dk2)).T)
    dec2 = jnp.concatenate([dec] * (2 * C_VAL_DIM // dk2), axis=1)

    refs = [blk * GLA_SUB + (GLA_SUB - 1 if reverse else 0) for blk in range(nsub)]
    b_ref = jnp.concatenate([jnp.broadcast_to(b[n:n + 1], (GLA_SUB, dk2)) for n in refs], axis=0)
    q_sc = q * jnp.exp(b - b_ref)
    k_parts = []
    for blk, n in enumerate(refs):
        in_range = (row >= blk * GLA_SUB) if reverse else (row < (blk + 1) * GLA_SUB)
        k_parts.append(jnp.where(in_range, k * jnp.exp(b[n:n + 1] - b), 0.0))
    k_big = jnp.concatenate(k_parts, axis=1).astype(BF16)
    q_halves = []
    for hf in range(2):
        in_half = (lane >= hf * C_KEY_DIM) & (lane < (hf + 1) * C_KEY_DIM)
        q_halves.append(jnp.concatenate(
            [jnp.where(in_half & (row >= blk * GLA_SUB) & (row < (blk + 1) * GLA_SUB), q_sc, 0.0)
             for blk in range(nsub)], axis=1))
    q_big = jnp.concatenate(q_halves, axis=0).astype(BF16)
    return qe, kd_t, dec2, q_big, k_big


def _gla_kernel(qf_ref, kf_ref, vf_ref, gf_ref, qr_ref, kr_ref, vr_ref, gr_ref, of_ref, or_ref,
                sf_ref, sr_ref):
    t = pl.program_id(1)

    @pl.when(t == 0)
    def _():
        sf_ref[...] = jnp.zeros(sf_ref.shape, F32)
        sr_ref[...] = jnp.zeros(sr_ref.shape, F32)

    C = C_CHUNK
    nchunk = TM // C
    npair = C_HEADS // 2
    kw, vw = 2 * C_KEY_DIM, 2 * C_VAL_DIM
    dirs = ((qf_ref, kf_ref, vf_ref, gf_ref, of_ref, sf_ref, False),
            (qr_ref, kr_ref, vr_ref, gr_ref, or_ref, sr_ref, True))

    ii = lax.broadcasted_iota(jnp.int32, (TM, TM), 0)
    jj = lax.broadcasted_iota(jnp.int32, (TM, TM), 1)
    same_chunk = (ii // C) == (jj // C)
    b_all = []
    for (_, _, _, g_ref, _, _, reverse) in dirs:
        tri = jnp.where(same_chunk & ((jj >= ii) if reverse else (jj <= ii)), 1.0, 0.0).astype(BF16)
        g_hi, g_mid, g_lo = _split_bf16(g_ref[...], 3)
        b_all.append(_dot(tri, g_hi) + (_dot(tri, g_mid) + _dot(tri, g_lo)))

    units = []
    for d, (q_ref, k_ref, v_ref, _, o_ref, _, reverse) in enumerate(dirs):
        for p in range(npair):
            ks, vsl = slice(p * kw, (p + 1) * kw), slice(p * vw, (p + 1) * vw)
            for c in range(nchunk):
                rows = slice(c * C, (c + 1) * C)
                prep = _gla_prep(q_ref[rows, ks], k_ref[rows, ks], b_all[d][rows, ks], reverse)
                units.append((d, p, c, rows, vsl, v_ref, o_ref, reverse, prep))

    srow = lax.broadcasted_iota(jnp.int32, (kw, vw), 0)
    scol = lax.broadcasted_iota(jnp.int32, (kw, vw), 1)
    on_diag = (srow < C_KEY_DIM) == (scol < C_VAL_DIM)
    qi = lax.broadcasted_iota(jnp.int32, (2 * C, C), 0) % C
    kj = lax.broadcasted_iota(jnp.int32, (2 * C, C), 1)
    kvs, atts = [], []
    for (d, p, c, rows, vsl, v_ref, o_ref, reverse, prep) in units:
        _, kd_t, _, q_big, k_big = prep
        kvs.append(jnp.where(on_diag, _dot(kd_t, v_ref[rows, vsl]), 0.0))
        keep = (kj >= qi) if reverse else (kj <= qi)
        atts.append(jnp.where(keep, _dot_nt(q_big, k_big), 0.0).astype(BF16))

    intra = {}
    for u, (d, p, c, rows, vsl, v_ref, o_ref, reverse, prep) in enumerate(units):
        pv = _dot(atts[u], v_ref[rows, vsl])
        intra[(d, p, c)] = (u, jnp.concatenate([pv[0:C, 0:C_VAL_DIM], pv[C:2 * C, C_VAL_DIM:]], axis=1))

    states = {(d, p): dirs[d][5][p] for d in range(len(dirs)) for p in range(npair)}
    for step in range(nchunk):
        for d in range(len(dirs)):
            reverse = dirs[d][6]
            c = nchunk - 1 - step if reverse else step
            for p in range(npair):
                u, o_intra = intra[(d, p, c)]
                _, _, _, rows, vsl, _, o_ref, _, prep = units[u]
                qe, _, dec2, _, _ = prep
                o_ref[rows, vsl] = o_intra + _dot(qe, states[(d, p)].astype(BF16))
                states[(d, p)] = dec2 * states[(d, p)] + kvs[u]
    for d in range(len(dirs)):
        for p in range(npair):
            dirs[d][5][p] = states[(d, p)]


def _gla(cq, ck, cv, g):
    B, T, kw = cq.shape
    NT = T // TM
    vw = cv.shape[2]

    def fwd(n, col=0):
        return pl.BlockSpec((None, TM, n), lambda b, t: (b, t, col))

    def rev(n, col=0):
        return pl.BlockSpec((None, TM, n), lambda b, t: (b, jnp.where(t == 0, 0, NT - t), col))

    return pl.pallas_call(
        _gla_kernel,
        grid=(B, NT),
        in_specs=[fwd(kw), fwd(kw), fwd(vw), fwd(kw, 0), rev(kw), rev(kw), rev(vw), rev(kw, 1)],
        out_specs=[fwd(vw), rev(vw)],
        out_shape=[jax.ShapeDtypeStruct((B, T, vw), F32)] * 2,
        scratch_shapes=[pltpu.VMEM((C_HEADS // 2, 2 * C_KEY_DIM, 2 * C_VAL_DIM), F32)] * 2,
        compiler_params=pltpu.CompilerParams(dimension_semantics=("parallel", "arbitrary")),
    )(cq, ck, cv, g, cq, ck, cv, g)


def _merge_kernel(xc_ref, xl_ref, mod_ref, ya_ref, yb_ref, of_ref, or_ref, cr_ref, gate_ref, hn_ref,
                  wa_ref, wb_ref, wc_ref, wo_ref, o_ref):
    d = xl_ref.shape[1]
    o = of_ref[...] + or_ref[...]
    r = cr_ref[...].astype(F32)
    parts = []
    for hd in range(C_HEADS):
        sl = slice(hd * C_VAL_DIM, (hd + 1) * C_VAL_DIM)
        parts.append(_rms(o[:, sl], hn_ref[:, sl]))
    yc = (jnp.concatenate(parts, axis=1) * (r * jax.nn.sigmoid(r))).astype(BF16)

    def gate(i):
        return jax.nn.sigmoid(gate_ref[:, i * d:(i + 1) * d].astype(F32))

    m = (gate(0) * _dot(ya_ref[...], wa_ref[...]) + gate(1) * _dot(yb_ref[...], wb_ref[...])
         + gate(2) * _dot(yc, wc_ref[...]))
    y = _dot(m.astype(BF16), wo_ref[...])
    o_ref[...] = _stream_tile(xc_ref, xl_ref) + mod_ref[2:3, :] * y


def _merge(Xc, Xl, mod, ya, yb, of, orv, cr, gates, hn, layer, wa, wb, wc, wo):
    spec_c, spec_l, T = _stream_specs(Xc, Xl)
    B, _, D = Xl.shape
    NT = T // TM
    row = lambda n: pl.BlockSpec((None, TM, n), lambda b, t: (b, t, 0))
    wbytes = sum(w.size // w.shape[0] for w in (wa, wb, wc, wo)) * 2
    vmem = wbytes + 2 * TM * (2 * D + 3 * D + 4 * ya.shape[2]) * 4 + (16 << 20)
    return pl.pallas_call(
        _merge_kernel,
        grid=(B, NT),
        in_specs=[spec_c, spec_l, pl.BlockSpec((None, 6, D), _mod_index),
                  row(ya.shape[2]), row(yb.shape[2]), row(of.shape[2]), row(orv.shape[2]),
                  row(cr.shape[2]), row(gates.shape[2]), _const_spec(hn.shape),
                  _layer_spec(wa, layer), _layer_spec(wb, layer), _layer_spec(wc, layer), _layer_spec(wo, layer)],
        out_specs=row(D),
        out_shape=jax.ShapeDtypeStruct((B, T, D), F32),
        compiler_params=pltpu.CompilerParams(
            dimension_semantics=("parallel", "parallel"), vmem_limit_bytes=_vmem_limit(vmem)),
    )(Xc, Xl, mod, ya, yb, of, orv, cr, gates, hn, wa, wb, wc, wo)


def _ffn_kernel(x_ref, xp_ref, xn_ref, mod_ref, nw_ref, wup_ref, cw_ref, cb_ref, wdn_ref, fw_ref, o_ref,
                u_ref, act_ref, *, nt, ncb, t0, final):
    t = pl.program_id(1) + t0
    mod = mod_ref[...]
    nw = nw_ref[...]
    halo = xp_ref.shape[0]
    ffn = wdn_ref.shape[0]
    cb = ffn // ncb

    def norm(xv):
        return _rms(xv, nw) * (1.0 + mod[4:5]) + mod[3:4]

    has_prev = (t >= 2)
    has_next = (t >= 1) & (t < nt - 1)
    hb = jnp.concatenate([norm(x_ref[...]).astype(BF16),
                          jnp.concatenate([jnp.where(has_prev, norm(xp_ref[...]), 0.0),
                                           jnp.where(has_next, norm(xn_ref[...]), 0.0)], axis=0).astype(BF16)], axis=0)
    row8 = lax.broadcasted_iota(jnp.int32, (SUBLANES, cb), 0)

    def up(j):
        u_ref[j % 2, 0] = _dot(hb, wup_ref[:, j * cb:(j + 1) * cb])
        u_ref[j % 2, 1] = _dot(hb, wup_ref[:, ffn + j * cb:ffn + (j + 1) * cb])

    def conv(slot, half, lo):
        w = cw_ref[:, lo:lo + cb]
        um = u_ref[slot, half, 0:TM, :]
        prev_row = u_ref[slot, half, TM + halo - 1:TM + halo, :]
        next_row = u_ref[slot, half, TM + halo:TM + halo + 1, :]
        rd = pltpu.roll(um, 1, 0)
        ru = pltpu.roll(um, TM - 1, 0)
        sd = jnp.concatenate([jnp.where(row8 == 0, prev_row, rd[0:SUBLANES]), rd[SUBLANES:]], axis=0)
        su = jnp.concatenate([ru[:TM - SUBLANES], jnp.where(row8 == SUBLANES - 1, next_row, ru[TM - SUBLANES:])],
                             axis=0)
        return sd * w[0:1] + um * w[1:2] + su * w[2:3] + cb_ref[:, lo:lo + cb]

    def act(j):
        gt = conv(j % 2, 0, j * cb)
        val = conv(j % 2, 1, ffn + j * cb)
        act_ref[:, j * cb:(j + 1) * cb] = (gt * jax.nn.sigmoid(gt) * val).astype(BF16)

    up(0)
    for j in range(1, ncb):
        up(j)
        act(j - 1)
    act(ncb - 1)
    y = x_ref[...] + mod[5:6] * _dot(act_ref[...], wdn_ref[...])
    o_ref[...] = _rms(y, fw_ref[...]) if final else y


def _ffn(X, mod, nw, layer, wup, cw, cb, wdn, fw, final):
    B, T, D = X.shape
    NT = T // TM
    t0 = 1 if final else 0
    halo = SUBLANES
    per = TM // halo
    last = T // halo - 1
    ffn = wdn.shape[1]
    cbw = FFN_COLS
    ncb = ffn // cbw
    vmem = 3 * D * ffn * 2 + 4 * (TM + 2 * halo) * cbw * 4 + TM * ffn * 2 + 8 * TM * D * 4 + (16 << 20)
    return pl.pallas_call(
        functools.partial(_ffn_kernel, nt=NT, ncb=ncb, t0=t0, final=final),
        grid=(B, NT - t0),
        in_specs=[pl.BlockSpec((None, TM, D), lambda b, t: (b, t + t0, 0)),
                  pl.BlockSpec((None, halo, D), lambda b, t: (b, jnp.maximum((t + t0) * per - 1, 0), 0)),
                  pl.BlockSpec((None, halo, D), lambda b, t: (b, jnp.minimum((t + t0 + 1) * per, last), 0)),
                  pl.BlockSpec((None, 6, D), lambda b, t: _mod_index(b, t + t0)),
                  _const_spec(nw.shape), _layer_spec(wup, layer), _const_spec(cw.shape), _const_spec(cb.shape),
                  _layer_spec(wdn, layer), _const_spec(fw.shape)],
        out_specs=pl.BlockSpec((None, TM, D), lambda b, t: (b, t, 0)),
        out_shape=jax.ShapeDtypeStruct((B, T - t0 * TM, D), F32),
        scratch_shapes=[pltpu.VMEM((2, 2, TM + 2 * halo, cbw), F32), pltpu.VMEM((TM, ffn), BF16)],
        compiler_params=pltpu.CompilerParams(
            dimension_semantics=("parallel", "parallel"), vmem_limit_bytes=_vmem_limit(vmem)),
    )(X, X, X, mod, nw, wup, cw, cb, wdn, fw)


def _rope_tables(seq, lc):
    rows = seq // GRID_W
    row = jnp.broadcast_to(jnp.arange(rows)[:, None], (rows, GRID_W)).reshape(-1).astype(F32)
    col = jnp.broadcast_to(jnp.arange(GRID_W)[None, :], (rows, GRID_W)).reshape(-1).astype(F32)

    def cs(rot_dim):
        n_freq = rot_dim // 4
        inv = ROPE_BASE ** (-jnp.arange(n_freq, dtype=F32) / n_freq)
        ang = jnp.concatenate([row[:, None] * inv, col[:, None] * inv], axis=-1)
        return jnp.cos(ang), jnp.sin(ang)

    ca, sa = cs(A_HEAD_DIM)
    cb, sb = cs(B_ROPE_DIM)
    reps = LANES // A_HEAD_DIM
    cos_a = jnp.tile(jnp.concatenate([ca, ca], axis=1), (1, reps))
    sin_a = jnp.tile(jnp.concatenate([-sa, sa], axis=1), (1, reps))
    one = jnp.ones((seq, B_NOPE_DIM), F32)
    pad = LANES - B_NOPE_DIM - B_ROPE_DIM
    cos_b = jnp.concatenate([one, cb, cb, jnp.ones((seq, pad), F32)], axis=1)
    sin_b = jnp.concatenate([0 * one, -sb, sb, jnp.zeros((seq, pad), F32)], axis=1)
    ident = jnp.stack([jnp.ones((lc, LANES), F32), jnp.zeros((lc, LANES), F32)] * 2)
    return jnp.concatenate([ident, jnp.stack([cos_a, sin_a, cos_b, sin_b])], axis=1)


def _prep_w_in(w_in):
    o_kr = _C_BKR
    o_cq = o_kr + B_ROPE_DIM
    o_cg = o_cq + (_C_CG - _C_CQ)
    o_gate = o_cg + 2 * C_GATE_RANK
    kr = jnp.pad(w_in[:, :, o_kr:o_cq], ((0, 0), (0, 0), (B_NOPE_DIM, LANES - B_NOPE_DIM - B_ROPE_DIM)))
    cg = jnp.pad(w_in[:, :, o_cg:o_gate], ((0, 0), (0, 0), (0, LANES - 2 * C_GATE_RANK)))
    return tuple(p.astype(BF16) for p in (w_in[:, :, :o_kr], kr, w_in[:, :, o_cq:o_cg], cg, w_in[:, :, o_gate:]))


def kernel(x, c, ctx, c_ctx, w_mod, b_mod, norm_mix, norm_ffn, w_in, a_sink, b_q_norm, b_kv_norm, b_w_uq, b_w_ukv, c_w_gate, c_b_gate, c_head_norm, w_br_a, w_br_b, w_br_c, w_out, w_up, conv_w, conv_b, w_down, final_norm):
    B, S, D = x.shape
    lc = ctx.shape[1]
    L = w_mod.shape[0]
    assert lc == TM and S % TM == 0 and S % GRID_W == 0 and B + 1 <= SUBLANES

    Xc, Xl = ctx, x
    cvec = jnp.zeros((SUBLANES, D), F32).at[0].set(c_ctx).at[1:B + 1].set(c)
    mods = _modulation(cvec, w_mod, b_mod).reshape(L, SUBLANES, 6, D)

    rope = _rope_tables(S, lc)
    w_in_r = _prep_w_in(w_in)
    sink = jnp.broadcast_to(a_sink[:, :, None], (L, A_HEADS, LANES)).astype(F32)
    wa, wb, wc, wo = w_br_a.astype(BF16), w_br_b.astype(BF16), w_br_c.astype(BF16), w_out.astype(BF16)
    qk = B_NOPE_DIM + B_ROPE_DIM
    wuq = jnp.pad(b_w_uq.reshape(L, B_Q_RANK, B_HEADS, qk),
                  ((0, 0), (0, 0), (0, 0), (0, B_HEAD_PAD - qk))).reshape(L, B_Q_RANK, -1).astype(BF16)
    ukv = b_w_ukv.reshape(L, B_KV_RANK, B_HEADS, B_NOPE_DIM + B_V_DIM)
    wkn = jnp.pad(ukv[..., :B_NOPE_DIM],
                  ((0, 0), (0, 0), (0, 0), (0, B_HEAD_PAD - B_NOPE_DIM))).reshape(L, B_KV_RANK, -1).astype(BF16)
    wvb = ukv[..., B_NOPE_DIM:].reshape(L, B_KV_RANK, -1).astype(BF16)
    nqk = C_HEADS * C_KEY_DIM
    wg = jnp.zeros((L, LANES, 2 * nqk), F32)
    wg = wg.at[:, 0:C_GATE_RANK, 0:nqk].set(c_w_gate[:, 0]).at[:, C_GATE_RANK:2 * C_GATE_RANK, nqk:].set(c_w_gate[:, 1])
    bg = c_b_gate.reshape(L, 1, 2 * nqk)
    wup, wdn = w_up.astype(BF16), w_down.astype(BF16)

    for l in range(L):
        mod = mods[l]
        (qa, ka, va, qb, kb, vb, cq, ck, cv, cr, g, gates) = _inproj(
            Xc, Xl, mod, norm_mix[l][None], l, w_in_r, rope, b_q_norm[l][None], b_kv_norm[l][None],
            wuq, wkn, wvb, wg, bg[l])
        ya = _attn_a(qa, ka, va, sink[l], lc)
        yb = _mla(qb, kb, vb, lc)
        of, orv = _gla(cq, ck, cv, g)
        X = _merge(Xc, Xl, mod, ya, yb, of, orv, cr, gates, c_head_norm[l][None], l, wa, wb, wc, wo)
        X = _ffn(X, mod, norm_ffn[l][None], l, wup, conv_w[l], conv_b[l][None], wdn, final_norm[None],
                 final=(l == L - 1))
        Xc = Xl = X
    return X
```

```python
import functools

import jax
import jax.numpy as jnp
from jax import lax
from jax.experimental import pallas as pl
from jax.experimental.pallas import tpu as pltpu

GRID_W = 64
EPS = 1e-6
ROPE_BASE = 10000.0
A_HEADS, A_KV_HEADS, A_HEAD_DIM, A_WINDOW = 8, 2, 64, 128
B_HEADS, B_Q_RANK, B_KV_RANK, B_NOPE_DIM, B_ROPE_DIM, B_V_DIM = 8, 256, 128, 64, 32, 64
C_HEADS, C_KEY_DIM, C_VAL_DIM, C_GATE_RANK, C_GATE_TAU, C_CHUNK = 4, 64, 128, 16, 16.0, 64
CONV_W = 3

LANES = 128
SUBLANES = 8
VMEM_BYTES = 64 * 1024 * 1024

TM = 256
GLA_SUB = 16
MLA_TK = 256
MLA_UNROLL = 16
MOD_COLS = 3072
FFN_COLS = 256
F32 = jnp.float32
BF16 = jnp.bfloat16
_HI = lax.Precision.HIGHEST
_NEG_INF = float("-inf")


def _dot(a, b):
    return jnp.dot(a, b, preferred_element_type=F32)


def _dot_nt(a, b):
    return lax.dot_general(a, b, (((1,), (1,)), ((), ())), preferred_element_type=F32)


def _split_bf16(x, terms):
    parts = []
    for _ in range(terms):
        p = x.astype(BF16)
        parts.append(p)
        x = x - p.astype(F32)
    return parts


def _dot_split(a, b):
    a_hi, a_lo = _split_bf16(a, 2)
    b_hi, b_lo = _split_bf16(b, 2)
    return _dot(a_hi, b_hi) + (_dot(a_lo, b_hi) + _dot(a_hi, b_lo))


def _rms(x, w):
    return x * lax.rsqrt(jnp.mean(x * x, axis=-1, keepdims=True) + EPS) * w


def _swap_lane_groups(x, half):
    lane = lax.broadcasted_iota(jnp.int32, x.shape, 1)
    up = pltpu.roll(x, LANES - half, 1)
    down = pltpu.roll(x, half, 1)
    return jnp.where((lane & half) == 0, up, down)


def _vmem_limit(nbytes):
    return int(min(VMEM_BYTES - (4 << 20), max(nbytes, 32 << 20)))


def _layer_spec(stacked, layer):
    shape = stacked.shape[1:]
    nd = len(shape)
    return pl.BlockSpec((None,) + shape, lambda *_: (layer,) + (0,) * nd, pipeline_mode=pl.Buffered(1))


def _const_spec(shape):
    nd = len(shape)
    return pl.BlockSpec(shape, lambda *_: (0,) * nd, pipeline_mode=pl.Buffered(1))


def _mod_kernel(c_ref, w_ref, b_ref, o_ref):
    cv = c_ref[...]
    sc = cv * jax.nn.sigmoid(cv)
    o_ref[...] = jnp.dot(sc, w_ref[...], precision=_HI, preferred_element_type=F32) + b_ref[...]


def _modulation(cvec, w_mod, b_mod):
    L, D, N = w_mod.shape
    nb = N // MOD_COLS
    return pl.pallas_call(
        _mod_kernel,
        grid=(L, nb),
        in_specs=[
            pl.BlockSpec((SUBLANES, D), lambda l, j: (0, 0)),
            pl.BlockSpec((None, D, MOD_COLS), lambda l, j: (l, 0, j)),
            pl.BlockSpec((None, 1, MOD_COLS), lambda l, j: (l, 0, j)),
        ],
        out_specs=pl.BlockSpec((None, SUBLANES, MOD_COLS), lambda l, j: (l, 0, j)),
        out_shape=jax.ShapeDtypeStruct((L, SUBLANES, N), F32),
        compiler_params=pltpu.CompilerParams(
            dimension_semantics=("parallel", "parallel"),
            vmem_limit_bytes=_vmem_limit(3 * D * MOD_COLS * 4)),
    )(cvec, w_mod, b_mod.reshape(L, 1, N))


_C_AQ = 0
_C_AK = _C_AQ + A_HEADS * A_HEAD_DIM
_C_AV = _C_AK + A_KV_HEADS * A_HEAD_DIM
_C_BQ = _C_AV + A_KV_HEADS * A_HEAD_DIM
_C_BKV = _C_BQ + B_Q_RANK
_C_BKR = _C_BKV + B_KV_RANK
_C_CQ = _C_BKR + LANES
_C_CK = _C_CQ + C_HEADS * C_KEY_DIM
_C_CV = _C_CK + C_HEADS * C_KEY_DIM
_C_CR = _C_CV + C_HEADS * C_VAL_DIM
_C_GATE = _C_CR + C_HEADS * C_VAL_DIM
B_HEAD_PAD = LANES
B_V_ROWS = B_V_DIM + 2 * SUBLANES
A_V_ROWS = A_HEAD_DIM + 2 * SUBLANES
LOG2E = 1.4426950408889634


def _stream_specs(xc, xl):
    off = 0 if xc is xl else 1
    d = xl.shape[2]
    spec_c = pl.BlockSpec((None, TM, d), lambda b, t: (b, 0, 0))
    spec_l = pl.BlockSpec((None, TM, d), lambda b, t: (b, jnp.maximum(t - off, 0), 0))
    return spec_c, spec_l, xl.shape[1] + off * TM


def _stream_tile(xc_ref, xl_ref):
    return jnp.where(pl.program_id(1) == 0, xc_ref[...], xl_ref[...])


def _inproj_kernel(xc_ref, xl_ref, mod_ref, nw_ref, w1_ref, wkv_ref, w2_ref, w3_ref, rope_ref,
                   bqn_ref, bkvn_ref, wuq_ref, wkn_ref,
                   wvb_ref, wg_ref, bg_ref,
                   qa_ref, ka_ref, va_ref, qb_ref, kb_ref, vb_ref, cq_ref, ck_ref, cv_ref, cr_ref,
                   g_ref, gate_ref, *, d_model):
    mod = mod_ref[...]
    h = _rms(_stream_tile(xc_ref, xl_ref), nw_ref[...]) * (1.0 + mod[1:2]) + mod[0:1]
    hb = h.astype(BF16)

    pieces = ((_C_AQ, w1_ref), (_C_BKV, wkv_ref), (_C_CQ, w2_ref), (_C_GATE, w3_ref))

    def proj(lo, hi):
        base, ref = [(b0, r) for b0, r in pieces if b0 <= lo][-1]
        assert hi - base <= ref.shape[1]
        return _dot(hb, ref[:, lo - base:hi - base])

    lane = lax.broadcasted_iota(jnp.int32, (TM, LANES), 1)

    cos_a, sin_a, cos_b, sin_b = rope_ref[0], rope_ref[1], rope_ref[2], rope_ref[3]

    def rope_a(t):
        return t * cos_a + _swap_lane_groups(t, A_HEAD_DIM // 2) * sin_a

    def rope_b(t):
        return t * cos_b + _swap_lane_groups(t, B_ROPE_DIM // 2) * sin_b

    a_scale = A_HEAD_DIM ** -0.5 * LOG2E
    aq = proj(_C_AQ, _C_AK)
    for c in range(A_HEADS * A_HEAD_DIM // LANES):
        qa_ref[c * LANES:(c + 1) * LANES, :] = (rope_a(aq[:, c * LANES:(c + 1) * LANES]) * a_scale).T.astype(BF16)
    akv = proj(_C_AK, _C_BQ)
    kvw = A_KV_HEADS * A_HEAD_DIM
    ka_ref[...] = rope_a(akv[:, 0:kvw]).astype(BF16)
    va_t = akv[:, kvw:2 * kvw].T.astype(BF16)
    a_ones = jnp.where(lax.broadcasted_iota(jnp.int32, (A_V_ROWS - A_HEAD_DIM, va_t.shape[1]), 0) == 0,
                       1.0, 0.0).astype(BF16)
    for g in range(A_KV_HEADS):
        va_ref[g * A_V_ROWS:g * A_V_ROWS + A_HEAD_DIM, :] = va_t[g * A_HEAD_DIM:(g + 1) * A_HEAD_DIM]
        va_ref[g * A_V_ROWS + A_HEAD_DIM:(g + 1) * A_V_ROWS, :] = a_ones

    b_scale = (B_NOPE_DIM + B_ROPE_DIM) ** -0.5 * LOG2E
    cqn = _rms(proj(_C_BQ, _C_BKV), bqn_ref[...]).astype(BF16)
    bkv = proj(_C_BKV, _C_CQ)
    ckvn = _rms(bkv[:, 0:B_KV_RANK], bkvn_ref[...]).astype(BF16)
    mixed = bkv[:, B_KV_RANK:B_KV_RANK + LANES]
    g_low = mixed
    kr = rope_b(jnp.where(lane >= B_NOPE_DIM, mixed, 0.0))
    for pair in range(B_HEADS // 2):
        sl2 = slice(2 * pair * B_HEAD_PAD, 2 * (pair + 1) * B_HEAD_PAD)
        q2 = _dot(cqn, wuq_ref[:, sl2])
        k2 = _dot(ckvn, wkn_ref[:, sl2])
        for i in range(2):
            sl = slice((2 * pair + i) * B_HEAD_PAD, (2 * pair + i + 1) * B_HEAD_PAD)
            part = slice(i * B_HEAD_PAD, (i + 1) * B_HEAD_PAD)
            qb_ref[sl, :] = (rope_b(q2[:, part]) * b_scale).T.astype(BF16)
            kb_ref[:, sl] = (k2[:, part] + kr).astype(BF16)
    vb = _dot(ckvn, wvb_ref[...])
    tk = vb_ref.shape[2]
    per = LANES // B_V_DIM
    ones_rows = jnp.where(lax.broadcasted_iota(jnp.int32, (B_V_ROWS - B_V_DIM, tk), 0) == 0, 1.0, 0.0).astype(BF16)
    for cc in range(vb_ref.shape[0]):
        for c in range(vb.shape[1] // LANES):
            v_t = vb[cc * tk:(cc + 1) * tk, c * LANES:(c + 1) * LANES].T.astype(BF16)
            for i in range(per):
                r0 = (c * per + i) * B_V_ROWS
                vb_ref[cc, r0:r0 + B_V_DIM, :] = v_t[i * B_V_DIM:(i + 1) * B_V_DIM]
                vb_ref[cc, r0 + B_V_DIM:r0 + B_V_ROWS, :] = ones_rows

    cq_ref[...] = proj(_C_CQ, _C_CK) * (C_KEY_DIM ** -0.5)
    ck_ref[...] = proj(_C_CK, _C_CV)
    cv_ref[...] = proj(_C_CV, _C_CR).astype(BF16)
    cr_ref[...] = proj(_C_CR, _C_GATE).astype(BF16)
    z = _dot_split(g_low, wg_ref[...]) + bg_ref[...]
    g_ref[...] = (jnp.minimum(z, 0.0) - jnp.log1p(jnp.exp(-jnp.abs(z)))) * (1.0 / C_GATE_TAU)

    gate_ref[...] = proj(_C_GATE, _C_GATE + 3 * d_model).astype(BF16)

def _mod_index(b, t):
    return (jnp.where(t == 0, 0, b + 1), 0, 0)


def _inproj(Xc, Xl, mod, nw, layer, w_pieces, rope, bqn, bkvn, wuq, wkn, wvb, wg, bg):
    spec_c, spec_l, T = _stream_specs(Xc, Xl)
    B, _, D = Xl.shape
    NT = T // TM
    n_in = sum(w.shape[2] for w in w_pieces)
    row = lambda n: pl.BlockSpec((None, TM, n), lambda b, t: (b, t, 0))
    widths = [(A_HEADS * A_HEAD_DIM, BF16), (A_KV_HEADS * A_HEAD_DIM, BF16), (A_KV_HEADS * A_HEAD_DIM, BF16),
              (B_HEADS * B_HEAD_PAD, BF16), (B_HEADS * B_HEAD_PAD, BF16), (B_HEADS * B_V_DIM, BF16),
              (C_HEADS * C_KEY_DIM, F32), (C_HEADS * C_KEY_DIM, F32), (C_HEADS * C_VAL_DIM, BF16),
              (C_HEADS * C_VAL_DIM, BF16), (2 * C_HEADS * C_KEY_DIM, F32), (3 * D, BF16)]
    out_bytes = sum(TM * n * jnp.dtype(dt).itemsize for n, dt in widths)
    vmem = D * n_in * 2 + 4 * out_bytes + 6 * TM * D * 4 + (8 << 20)
    out_specs = [row(n) for n, _ in widths]
    out_shape = [jax.ShapeDtypeStruct((B, T, n), dt) for n, dt in widths]
    qw, vw = B_HEADS * B_HEAD_PAD, B_HEADS * B_V_ROWS
    out_specs[3] = pl.BlockSpec((None, qw, TM), lambda b, t: (b, 0, t))
    out_shape[3] = jax.ShapeDtypeStruct((B, qw, T), BF16)
    out_specs[0] = pl.BlockSpec((None, widths[0][0], TM), lambda b, t: (b, 0, t))
    out_shape[0] = jax.ShapeDtypeStruct((B, widths[0][0], T), BF16)
    out_specs[2] = pl.BlockSpec((None, A_KV_HEADS * A_V_ROWS, TM), lambda b, t: (b, 0, t))
    out_shape[2] = jax.ShapeDtypeStruct((B, A_KV_HEADS * A_V_ROWS, T), BF16)
    out_specs[5] = pl.BlockSpec((None, TM // MLA_TK, vw, MLA_TK), lambda b, t: (b, t, 0, 0))
    out_shape[5] = jax.ShapeDtypeStruct((B, T // MLA_TK, vw, MLA_TK), BF16)
    return pl.pallas_call(
        functools.partial(_inproj_kernel, d_model=D),
        grid=(B, NT),
        in_specs=[
            spec_c, spec_l,
            pl.BlockSpec((None, 6, D), _mod_index),
            _const_spec((1, D)),
            *[_layer_spec(w, layer) for w in w_pieces],
            pl.BlockSpec((4, TM, LANES), lambda b, t: (0, t, 0)),
            _const_spec(bqn.shape), _const_spec(bkvn.shape), _layer_spec(wuq, layer),
            _layer_spec(wkn, layer), _layer_spec(wvb, layer), _layer_spec(wg, layer), _const_spec(bg.shape),
        ],
        out_specs=out_specs,
        out_shape=out_shape,
        compiler_params=pltpu.CompilerParams(
            dimension_semantics=("parallel", "parallel"), vmem_limit_bytes=_vmem_limit(vmem)),
    )(Xc, Xl, mod, nw, *w_pieces, rope, bqn, bkvn, wuq, wkn, wvb, wg, bg)


def _attn_a_kernel(q_ref, kp_ref, km_ref, kn_ref, kc_ref, vp_ref, vm_ref, vn_ref, vc_ref, sink_ref,
                   o_ref, *, seq):
    t = pl.program_id(1)
    kloc = jnp.concatenate([kp_ref[...], km_ref[...], kn_ref[...]], axis=0)
    vloc = jnp.concatenate([vp_ref[...], vm_ref[...], vn_ref[...]], axis=1)
    kctx, vctx = kc_ref[...], vc_ref[...]
    nloc = kloc.shape[0]
    r = lax.broadcasted_iota(jnp.int32, (nloc, TM), 0)
    c = lax.broadcasted_iota(jnp.int32, (nloc, TM), 1)
    rel = r - A_WINDOW - c
    kpos = (t - 1) * TM - A_WINDOW + r
    valid = (jnp.abs(rel) <= A_WINDOW) & (kpos >= 0) & (kpos < seq) & (t >= 1)
    group = A_HEADS // A_KV_HEADS
    per = LANES // A_HEAD_DIM
    q_zero = jnp.zeros((A_HEAD_DIM, TM), BF16)

    def scores(hd):
        q_t = q_ref[hd * A_HEAD_DIM:(hd + 1) * A_HEAD_DIM, :]
        g = hd // group
        qm = jnp.concatenate([q_t if i == g else q_zero for i in range(A_KV_HEADS)], axis=0)
        return _dot(kloc, qm), _dot(kctx, qm)

    cur = scores(0)
    parts = []
    for hd in range(A_HEADS):
        nxt = scores(hd + 1) if hd + 1 < A_HEADS else None
        s_loc = jnp.where(valid, cur[0], _NEG_INF)
        s_ctx = cur[1]
        sink = sink_ref[hd:hd + 1, 0:1] * LOG2E
        m = jnp.maximum(jnp.maximum(jnp.max(s_loc, axis=0, keepdims=True),
                                    jnp.max(s_ctx, axis=0, keepdims=True)), sink)
        p_loc = jnp.exp2((s_loc - m).astype(BF16))
        p_ctx = jnp.exp2((s_ctx - m).astype(BF16))
        g = hd // group
        vs = slice(g * A_V_ROWS, (g + 1) * A_V_ROWS)
        o_t = _dot(vloc[vs], p_loc) + _dot(vctx[vs], p_ctx)
        denom = o_t[A_HEAD_DIM:A_HEAD_DIM + 1] + jnp.exp2(sink - m)
        parts.append(o_t[0:A_HEAD_DIM] / denom)
        if len(parts) == per:
            c = hd // per
            o_ref[:, c * LANES:(c + 1) * LANES] = jnp.concatenate(parts, axis=0).T.astype(BF16)
            parts = []
        cur = nxt


def _attn_a(qa, ka, va, sink, lc):
    B, qw, T = qa.shape
    NT = T // TM
    seq = T - lc
    kvw = ka.shape[2]
    vrows = va.shape[1]
    per_tm = TM // A_WINDOW
    last = T // A_WINDOW - 1
    prev_i = lambda t: jnp.maximum(t * per_tm - 1, 0)
    next_i = lambda t: jnp.minimum((t + 1) * per_tm, last)
    kprev = pl.BlockSpec((None, A_WINDOW, kvw), lambda b, t: (b, prev_i(t), 0))
    kmain = pl.BlockSpec((None, TM, kvw), lambda b, t: (b, t, 0))
    knext = pl.BlockSpec((None, A_WINDOW, kvw), lambda b, t: (b, next_i(t), 0))
    kctx = pl.BlockSpec((None, lc, kvw), lambda b, t: (b, 0, 0))
    vprev = pl.BlockSpec((None, vrows, A_WINDOW), lambda b, t: (b, 0, prev_i(t)))
    vmain = pl.BlockSpec((None, vrows, TM), lambda b, t: (b, 0, t))
    vnext = pl.BlockSpec((None, vrows, A_WINDOW), lambda b, t: (b, 0, next_i(t)))
    vctx = pl.BlockSpec((None, vrows, lc), lambda b, t: (b, 0, 0))
    return pl.pallas_call(
        functools.partial(_attn_a_kernel, seq=seq),
        grid=(B, NT),
        in_specs=[pl.BlockSpec((None, qw, TM), lambda b, t: (b, 0, t)),
                  kprev, kmain, knext, kctx, vprev, vmain, vnext, vctx,
                  pl.BlockSpec(sink.shape, lambda b, t: (0, 0))],
        out_specs=pl.BlockSpec((None, TM, qw), lambda b, t: (b, t, 0)),
        out_shape=jax.ShapeDtypeStruct((B, T, qw), BF16),
        compiler_params=pltpu.CompilerParams(dimension_semantics=("parallel", "parallel")),
    )(qa, ka, ka, ka, ka, va, va, va, va, sink)


def _mla_kernel(q_ref, k_ref, v_ref, o_ref, m_ref, acc_ref, sa_ref, sb_ref, *, lc, tk):
    t = pl.program_id(1)
    total = k_ref.shape[0]
    nchunks = jnp.where(t == 0, lc // tk, total // tk)
    m_ref[...] = jnp.full(m_ref.shape, _NEG_INF, F32)
    acc_ref[...] = jnp.zeros(acc_ref.shape, F32)

    def stage(dst_ref, j, hd):
        qs = slice(hd * B_HEAD_PAD, (hd + 1) * B_HEAD_PAD)
        s = _dot(k_ref[pl.ds(pl.multiple_of(j * tk, tk), tk), qs], q_ref[qs, :])
        dst_ref[hd, 0:tk, :] = s
        dst_ref[hd, tk:tk + SUBLANES, :] = jnp.broadcast_to(jnp.max(s, axis=0, keepdims=True), (SUBLANES, TM))

    for hd in range(B_HEADS):
        stage(sa_ref, 0, hd)

    def step(j, cur_ref, nxt_ref, last=False):
        for hd in range(B_HEADS):
            vs = slice(hd * B_V_ROWS, (hd + 1) * B_V_ROWS)
            if not last:
                stage(nxt_ref, j + 1, hd)
            m_prev = m_ref[hd]
            m_new = jnp.maximum(m_prev, cur_ref[hd, tk:tk + SUBLANES, :])
            alpha = jnp.exp2(m_prev - m_new)
            p = jnp.exp2((cur_ref[hd, 0:tk, :] - m_new[0:1]).astype(BF16))
            m_ref[hd] = m_new
            acc_ref[vs, :] = acc_ref[vs, :] * alpha[0:1] + _dot(v_ref[j, vs, :], p)

    bufs = (sa_ref, sb_ref)

    def body(g, carry):
        for i in range(MLA_UNROLL):
            step(MLA_UNROLL * g + i, bufs[i % 2], bufs[(i + 1) % 2])
        return carry

    ngroups = (nchunks - 1) // MLA_UNROLL
    lax.fori_loop(0, ngroups, body, 0)
    tail = (total // tk - 1) % MLA_UNROLL + 1
    assert tail == (lc // tk - 1) % MLA_UNROLL + 1
    for i in range(tail):
        step(MLA_UNROLL * ngroups + i, bufs[i % 2], bufs[(i + 1) % 2], last=(i == tail - 1))
    per = LANES // B_V_DIM
    for c in range(B_HEADS // per):
        o_t = jnp.concatenate(
            [acc_ref[hd * B_V_ROWS:hd * B_V_ROWS + B_V_DIM, :]
             / acc_ref[hd * B_V_ROWS + B_V_DIM:hd * B_V_ROWS + B_V_DIM + 1, :]
             for hd in range(c * per, (c + 1) * per)], axis=0)
        o_ref[:, c * LANES:(c + 1) * LANES] = o_t.T.astype(BF16)


def _mla(qb, kb, vb, lc):
    B, qw, T = qb.shape
    NT = T // TM
    _, nck, vw, tk = vb.shape
    ow = B_HEADS * B_V_DIM
    vmem = (T * (qw + vw) * 2 + 4 * TM * qw * 2 + (B_HEADS * SUBLANES * TM + 3 * TM * vw) * 4
            + 2 * B_HEADS * (tk + SUBLANES) * TM * 4 + (16 << 20))
    return pl.pallas_call(
        functools.partial(_mla_kernel, lc=lc, tk=tk),
        grid=(B, NT),
        in_specs=[pl.BlockSpec((None, qw, TM), lambda b, t: (b, 0, t)),
                  pl.BlockSpec((None, T, qw), lambda b, t: (b, 0, 0), pipeline_mode=pl.Buffered(1)),
                  pl.BlockSpec((None, nck, vw, tk), lambda b, t: (b, 0, 0, 0), pipeline_mode=pl.Buffered(1))],
        out_specs=pl.BlockSpec((None, TM, ow), lambda b, t: (b, t, 0)),
        out_shape=jax.ShapeDtypeStruct((B, T, ow), BF16),
        scratch_shapes=[pltpu.VMEM((B_HEADS, SUBLANES, TM), F32),
                        pltpu.VMEM((vw, TM), F32),
                        pltpu.VMEM((B_HEADS, tk + SUBLANES, TM), F32),
                        pltpu.VMEM((B_HEADS, tk + SUBLANES, TM), F32)],
        compiler_params=pltpu.CompilerParams(
            dimension_semantics=("parallel", "arbitrary"), vmem_limit_bytes=_vmem_limit(vmem)),
    )(qb, kb, vb)


def _gla_prep(q, k, b, reverse):
    C = q.shape[0]
    dk2 = q.shape[1]
    nsub = C // GLA_SUB
    b_end = b[0:1] if reverse else b[C - 1:C]
    row = lax.broadcasted_iota(jnp.int32, (C, dk2), 0)
    lane = lax.broadcasted_iota(jnp.int32, (C, dk2), 1)

    qe = (q * jnp.exp(b)).astype(BF16)
    kd_t = (k * jnp.exp(b_end - b)).T.astype(BF16)
    dec = jnp.exp(jnp.broadcast_to(b_end, (dk2, dk2)).T)
    dec2 = jnp.concatenate([dec] * (2 * C_VAL_DIM // dk2), axis=1)

    refs = [blk * GLA_SUB + (GLA_SUB - 1 if reverse else 0) for blk in range(nsub)]
    b_ref = jnp.concatenate([jnp.broadcast_to(b[n:n + 1], (GLA_SUB, dk2)) for n in refs], axis=0)
    q_sc = q * jnp.exp(b - b_ref)
    k_parts = []
    for blk, n in enumerate(refs):
        in_range = (row >= blk * GLA_SUB) if reverse else (row < (blk + 1) * GLA_SUB)
        k_parts.append(jnp.where(in_range, k * jnp.exp(b[n:n + 1] - b), 0.0))
    k_big = jnp.concatenate(k_parts, axis=1).astype(BF16)
    q_halves = []
    for hf in range(2):
        in_half = (lane >= hf * C_KEY_DIM) & (lane < (hf + 1) * C_KEY_DIM)
        q_halves.append(jnp.concatenate(
            [jnp.where(in_half & (row >= blk * GLA_SUB) & (row < (blk + 1) * GLA_SUB), q_sc, 0.0)
             for blk in range(nsub)], axis=1))
    q_big = jnp.concatenate(q_halves, axis=0).astype(BF16)
    return qe, kd_t, dec2, q_big, k_big


def _gla_kernel(qf_ref, kf_ref, vf_ref, gf_ref, qr_ref, kr_ref, vr_ref, gr_ref, of_ref, or_ref,
                sf_ref, sr_ref):
    t = pl.program_id(1)

    @pl.when(t == 0)
    def _():
        sf_ref[...] = jnp.zeros(sf_ref.shape, F32)
        sr_ref[...] = jnp.zeros(sr_ref.shape, F32)

    C = C_CHUNK
    nchunk = TM // C
    npair = C_HEADS // 2
    kw, vw = 2 * C_KEY_DIM, 2 * C_VAL_DIM
    dirs = ((qf_ref, kf_ref, vf_ref, gf_ref, of_ref, sf_ref, False),
            (qr_ref, kr_ref, vr_ref, gr_ref, or_ref, sr_ref, True))

    ii = lax.broadcasted_iota(jnp.int32, (TM, TM), 0)
    jj = lax.broadcasted_iota(jnp.int32, (TM, TM), 1)
    same_chunk = (ii // C) == (jj // C)
    b_all = []
    for (_, _, _, g_ref, _, _, reverse) in dirs:
        tri = jnp.where(same_chunk & ((jj >= ii) if reverse else (jj <= ii)), 1.0, 0.0).astype(BF16)
        g_hi, g_mid, g_lo = _split_bf16(g_ref[...], 3)
        b_all.append(_dot(tri, g_hi) + (_dot(tri, g_mid) + _dot(tri, g_lo)))

    units = []
    for d, (q_ref, k_ref, v_ref, _, o_ref, _, reverse) in enumerate(dirs):
        for p in range(npair):
            ks, vsl = slice(p * kw, (p + 1) * kw), slice(p * vw, (p + 1) * vw)
            for c in range(nchunk):
                rows = slice(c * C, (c + 1) * C)
                prep = _gla_prep(q_ref[rows, ks], k_ref[rows, ks], b_all[d][rows, ks], reverse)
                units.append((d, p, c, rows, vsl, v_ref, o_ref, reverse, prep))

    srow = lax.broadcasted_iota(jnp.int32, (kw, vw), 0)
    scol = lax.broadcasted_iota(jnp.int32, (kw, vw), 1)
    on_diag = (srow < C_KEY_DIM) == (scol < C_VAL_DIM)
    qi = lax.broadcasted_iota(jnp.int32, (2 * C, C), 0) % C
    kj = lax.broadcasted_iota(jnp.int32, (2 * C, C), 1)
    kvs, atts = [], []
    for (d, p, c, rows, vsl, v_ref, o_ref, reverse, prep) in units:
        _, kd_t, _, q_big, k_big = prep
        kvs.append(jnp.where(on_diag, _dot(kd_t, v_ref[rows, vsl]), 0.0))
        keep = (kj >= qi) if reverse else (kj <= qi)
        atts.append(jnp.where(keep, _dot_nt(q_big, k_big), 0.0).astype(BF16))

    intra = {}
    for u, (d, p, c, rows, vsl, v_ref, o_ref, reverse, prep) in enumerate(units):
        pv = _dot(atts[u], v_ref[rows, vsl])
        intra[(d, p, c)] = (u, jnp.concatenate([pv[0:C, 0:C_VAL_DIM], pv[C:2 * C, C_VAL_DIM:]], axis=1))

    states = {(d, p): dirs[d][5][p] for d in range(len(dirs)) for p in range(npair)}
    for step in range(nchunk):
        for d in range(len(dirs)):
            reverse = dirs[d][6]
            c = nchunk - 1 - step if reverse else step
            for p in range(npair):
                u, o_intra = intra[(d, p, c)]
                _, _, _, rows, vsl, _, o_ref, _, prep = units[u]
                qe, _, dec2, _, _ = prep
                o_ref[rows, vsl] = o_intra + _dot(qe, states[(d, p)].astype(BF16))
                states[(d, p)] = dec2 * states[(d, p)] + kvs[u]
    for d in range(len(dirs)):
        for p in range(npair):
            dirs[d][5][p] = states[(d, p)]


def _gla(cq, ck, cv, g):
    B, T, kw = cq.shape
    NT = T // TM
    vw = cv.shape[2]

    def fwd(n, col=0):
        return pl.BlockSpec((None, TM, n), lambda b, t: (b, t, col))

    def rev(n, col=0):
        return pl.BlockSpec((None, TM, n), lambda b, t: (b, jnp.where(t == 0, 0, NT - t), col))

    return pl.pallas_call(
        _gla_kernel,
        grid=(B, NT),
        in_specs=[fwd(kw), fwd(kw), fwd(vw), fwd(kw, 0), rev(kw), rev(kw), rev(vw), rev(kw, 1)],
        out_specs=[fwd(vw), rev(vw)],
        out_shape=[jax.ShapeDtypeStruct((B, T, vw), F32)] * 2,
        scratch_shapes=[pltpu.VMEM((C_HEADS // 2, 2 * C_KEY_DIM, 2 * C_VAL_DIM), F32)] * 2,
        compiler_params=pltpu.CompilerParams(dimension_semantics=("parallel", "arbitrary")),
    )(cq, ck, cv, g, cq, ck, cv, g)


def _merge_kernel(xc_ref, xl_ref, mod_ref, ya_ref, yb_ref, of_ref, or_ref, cr_ref, gate_ref, hn_ref,
                  wa_ref, wb_ref, wc_ref, wo_ref, o_ref):
    d = xl_ref.shape[1]
    o = of_ref[...] + or_ref[...]
    r = cr_ref[...].astype(F32)
    parts = []
    for hd in range(C_HEADS):
        sl = slice(hd * C_VAL_DIM, (hd + 1) * C_VAL_DIM)
        parts.append(_rms(o[:, sl], hn_ref[:, sl]))
    yc = (jnp.concatenate(parts, axis=1) * (r * jax.nn.sigmoid(r))).astype(BF16)

    def gate(i):
        return jax.nn.sigmoid(gate_ref[:, i * d:(i + 1) * d].astype(F32))

    m = (gate(0) * _dot(ya_ref[...], wa_ref[...]) + gate(1) * _dot(yb_ref[...], wb_ref[...])
         + gate(2) * _dot(yc, wc_ref[...]))
    y = _dot(m.astype(BF16), wo_ref[...])
    o_ref[...] = _stream_tile(xc_ref, xl_ref) + mod_ref[2:3, :] * y


def _merge(Xc, Xl, mod, ya, yb, of, orv, cr, gates, hn, layer, wa, wb, wc, wo):
    spec_c, spec_l, T = _stream_specs(Xc, Xl)
    B, _, D = Xl.shape
    NT = T // TM
    row = lambda n: pl.BlockSpec((None, TM, n), lambda b, t: (b, t, 0))
    wbytes = sum(w.size // w.shape[0] for w in (wa, wb, wc, wo)) * 2
    vmem = wbytes + 2 * TM * (2 * D + 3 * D + 4 * ya.shape[2]) * 4 + (16 << 20)
    return pl.pallas_call(
        _merge_kernel,
        grid=(B, NT),
        in_specs=[spec_c, spec_l, pl.BlockSpec((None, 6, D), _mod_index),
                  row(ya.shape[2]), row(yb.shape[2]), row(of.shape[2]), row(orv.shape[2]),
                  row(cr.shape[2]), row(gates.shape[2]), _const_spec(hn.shape),
                  _layer_spec(wa, layer), _layer_spec(wb, layer), _layer_spec(wc, layer), _layer_spec(wo, layer)],
        out_specs=row(D),
        out_shape=jax.ShapeDtypeStruct((B, T, D), F32),
        compiler_params=pltpu.CompilerParams(
            dimension_semantics=("parallel", "parallel"), vmem_limit_bytes=_vmem_limit(vmem)),
    )(Xc, Xl, mod, ya, yb, of, orv, cr, gates, hn, wa, wb, wc, wo)


def _ffn_kernel(x_ref, xp_ref, xn_ref, mod_ref, nw_ref, wup_ref, cw_ref, cb_ref, wdn_ref, fw_ref, o_ref,
                u_ref, act_ref, *, nt, ncb, t0, final):
    t = pl.program_id(1) + t0
    mod = mod_ref[...]
    nw = nw_ref[...]
    halo = xp_ref.shape[0]
    ffn = wdn_ref.shape[0]
    cb = ffn // ncb

    def norm(xv):
        return _rms(xv, nw) * (1.0 + mod[4:5]) + mod[3:4]

    has_prev = (t >= 2)
    has_next = (t >= 1) & (t < nt - 1)
    hb = jnp.concatenate([norm(x_ref[...]).astype(BF16),
                          jnp.concatenate([jnp.where(has_prev, norm(xp_ref[...]), 0.0),
                                           jnp.where(has_next, norm(xn_ref[...]), 0.0)], axis=0).astype(BF16)], axis=0)
    row8 = lax.broadcasted_iota(jnp.int32, (SUBLANES, cb), 0)

    def up(j):
        u_ref[j % 2, 0] = _dot(hb, wup_ref[:, j * cb:(j + 1) * cb])
        u_ref[j % 2, 1] = _dot(hb, wup_ref[:, ffn + j * cb:ffn + (j + 1) * cb])

    def conv(slot, half, lo):
        w = cw_ref[:, lo:lo + cb]
        um = u_ref[slot, half, 0:TM, :]
        prev_row = u_ref[slot, half, TM + halo - 1:TM + halo, :]
        next_row = u_ref[slot, half, TM + halo:TM + halo + 1, :]
        rd = pltpu.roll(um, 1, 0)
        ru = pltpu.roll(um, TM - 1, 0)
        sd = jnp.concatenate([jnp.where(row8 == 0, prev_row, rd[0:SUBLANES]), rd[SUBLANES:]], axis=0)
        su = jnp.concatenate([ru[:TM - SUBLANES], jnp.where(row8 == SUBLANES - 1, next_row, ru[TM - SUBLANES:])],
                             axis=0)
        return sd * w[0:1] + um * w[1:2] + su * w[2:3] + cb_ref[:, lo:lo + cb]

    def act(j):
        gt = conv(j % 2, 0, j * cb)
        val = conv(j % 2, 1, ffn + j * cb)
        act_ref[:, j * cb:(j + 1) * cb] = (gt * jax.nn.sigmoid(gt) * val).astype(BF16)

    up(0)
    for j in range(1, ncb):
        up(j)
        act(j - 1)
    act(ncb - 1)
    y = x_ref[...] + mod[5:6] * _dot(act_ref[...], wdn_ref[...])
    o_ref[...] = _rms(y, fw_ref[...]) if final else y


def _ffn(X, mod, nw, layer, wup, cw, cb, wdn, fw, final):
    B, T, D = X.shape
    NT = T // TM
    t0 = 1 if final else 0
    halo = SUBLANES
    per = TM // halo
    last = T // halo - 1
    ffn = wdn.shape[1]
    cbw = FFN_COLS
    ncb = ffn // cbw
    vmem = 3 * D * ffn * 2 + 4 * (TM + 2 * halo) * cbw * 4 + TM * ffn * 2 + 8 * TM * D * 4 + (16 << 20)
    return pl.pallas_call(
        functools.partial(_ffn_kernel, nt=NT, ncb=ncb, t0=t0, final=final),
        grid=(B, NT - t0),
        in_specs=[pl.BlockSpec((None, TM, D), lambda b, t: (b, t + t0, 0)),
                  pl.BlockSpec((None, halo, D), lambda b, t: (b, jnp.maximum((t + t0) * per - 1, 0), 0)),
                  pl.BlockSpec((None, halo, D), lambda b, t: (b, jnp.minimum((t + t0 + 1) * per, last), 0)),
                  pl.BlockSpec((None, 6, D), lambda b, t: _mod_index(b, t + t0)),
                  _const_spec(nw.shape), _layer_spec(wup, layer), _const_spec(cw.shape), _const_spec(cb.shape),
                  _layer_spec(wdn, layer), _const_spec(fw.shape)],
        out_specs=pl.BlockSpec((None, TM, D), lambda b, t: (b, t, 0)),
        out_shape=jax.ShapeDtypeStruct((B, T - t0 * TM, D), F32),
        scratch_shapes=[pltpu.VMEM((2, 2, TM + 2 * halo, cbw), F32), pltpu.VMEM((TM, ffn), BF16)],
        compiler_params=pltpu.CompilerParams(
            dimension_semantics=("parallel", "parallel"), vmem_limit_bytes=_vmem_limit(vmem)),
    )(X, X, X, mod, nw, wup, cw, cb, wdn, fw)


def _rope_tables(seq, lc):
    rows = seq // GRID_W
    row = jnp.broadcast_to(jnp.arange(rows)[:, None], (rows, GRID_W)).reshape(-1).astype(F32)
    col = jnp.broadcast_to(jnp.arange(GRID_W)[None, :], (rows, GRID_W)).reshape(-1).astype(F32)

    def cs(rot_dim):
        n_freq = rot_dim // 4
        inv = ROPE_BASE ** (-jnp.arange(n_freq, dtype=F32) / n_freq)
        ang = jnp.concatenate([row[:, None] * inv, col[:, None] * inv], axis=-1)
        return jnp.cos(ang), jnp.sin(ang)

    ca, sa = cs(A_HEAD_DIM)
    cb, sb = cs(B_ROPE_DIM)
    reps = LANES // A_HEAD_DIM
    cos_a = jnp.tile(jnp.concatenate([ca, ca], axis=1), (1, reps))
    sin_a = jnp.tile(jnp.concatenate([-sa, sa], axis=1), (1, reps))
    one = jnp.ones((seq, B_NOPE_DIM), F32)
    pad = LANES - B_NOPE_DIM - B_ROPE_DIM
    cos_b = jnp.concatenate([one, cb, cb, jnp.ones((seq, pad), F32)], axis=1)
    sin_b = jnp.concatenate([0 * one, -sb, sb, jnp.zeros((seq, pad), F32)], axis=1)
    ident = jnp.stack([jnp.ones((lc, LANES), F32), jnp.zeros((lc, LANES), F32)] * 2)
    return jnp.concatenate([ident, jnp.stack([cos_a, sin_a, cos_b, sin_b])], axis=1)


def _prep_w_in(w_in):
    o_kv = _C_BKV
    o_kr = o_kv + B_KV_RANK
    o_cq = o_kr + B_ROPE_DIM
    o_cg = o_cq + (_C_GATE - _C_CQ)
    o_gate = o_cg + 2 * C_GATE_RANK
    zeros = lambda n: jnp.zeros(w_in.shape[:2] + (n,), w_in.dtype)
    gap = B_NOPE_DIM - 2 * C_GATE_RANK
    wkv = jnp.concatenate([w_in[:, :, o_kv:o_kr], w_in[:, :, o_cg:o_gate], zeros(gap), w_in[:, :, o_kr:o_cq],
                           zeros(LANES - B_NOPE_DIM - B_ROPE_DIM)], axis=-1)
    return tuple(p.astype(BF16) for p in (w_in[:, :, :o_kv], wkv, w_in[:, :, o_cq:o_cg], w_in[:, :, o_gate:]))


def kernel(x, c, ctx, c_ctx, w_mod, b_mod, norm_mix, norm_ffn, w_in, a_sink, b_q_norm, b_kv_norm, b_w_uq, b_w_ukv, c_w_gate, c_b_gate, c_head_norm, w_br_a, w_br_b, w_br_c, w_out, w_up, conv_w, conv_b, w_down, final_norm):
    B, S, D = x.shape
    lc = ctx.shape[1]
    L = w_mod.shape[0]
    assert lc == TM and S % TM == 0 and S % GRID_W == 0 and B + 1 <= SUBLANES

    Xc, Xl = ctx, x
    cvec = jnp.zeros((SUBLANES, D), F32).at[0].set(c_ctx).at[1:B + 1].set(c)
    mods = _modulation(cvec, w_mod, b_mod).reshape(L, SUBLANES, 6, D)

    rope = _rope_tables(S, lc)
    w_in_r = _prep_w_in(w_in)
    sink = jnp.broadcast_to(a_sink[:, :, None], (L, A_HEADS, LANES)).astype(F32)
    wa, wb, wc, wo = w_br_a.astype(BF16), w_br_b.astype(BF16), w_br_c.astype(BF16), w_out.astype(BF16)
    qk = B_NOPE_DIM + B_ROPE_DIM
    wuq = jnp.pad(b_w_uq.reshape(L, B_Q_RANK, B_HEADS, qk),
                  ((0, 0), (0, 0), (0, 0), (0, B_HEAD_PAD - qk))).reshape(L, B_Q_RANK, -1).astype(BF16)
    ukv = b_w_ukv.reshape(L, B_KV_RANK, B_HEADS, B_NOPE_DIM + B_V_DIM)
    wkn = jnp.pad(ukv[..., :B_NOPE_DIM],
                  ((0, 0), (0, 0), (0, 0), (0, B_HEAD_PAD - B_NOPE_DIM))).reshape(L, B_KV_RANK, -1).astype(BF16)
    wvb = ukv[..., B_NOPE_DIM:].reshape(L, B_KV_RANK, -1).astype(BF16)
    nqk = C_HEADS * C_KEY_DIM
    wg = jnp.zeros((L, LANES, 2 * nqk), F32)
    wg = wg.at[:, 0:C_GATE_RANK, 0:nqk].set(c_w_gate[:, 0]).at[:, C_GATE_RANK:2 * C_GATE_RANK, nqk:].set(c_w_gate[:, 1])
    bg = c_b_gate.reshape(L, 1, 2 * nqk)
    wup, wdn = w_up.astype(BF16), w_down.astype(BF16)

    for l in range(L):
        mod = mods[l]
        (qa, ka, va, qb, kb, vb, cq, ck, cv, cr, g, gates) = _inproj(
            Xc, Xl, mod, norm_mix[l][None], l, w_in_r, rope, b_q_norm[l][None], b_kv_norm[l][None],
            wuq, wkn, wvb, wg, bg[l])
        ya = _attn_a(qa, ka, va, sink[l], lc)
        yb = _mla(qb, kb, vb, lc)
        of, orv = _gla(cq, ck, cv, g)
        X = _merge(Xc, Xl, mod, ya, yb, of, orv, cr, gates, c_head_norm[l][None], l, wa, wb, wc, wo)
        X = _ffn(X, mod, norm_ffn[l][None], l, wup, conv_w[l], conv_b[l][None], wdn, final_norm[None],
                 final=(l == L - 1))
        Xc = Xl = X
    return X
```

```python
import functools

import jax
import jax.numpy as jnp
from jax import lax
from jax.experimental import pallas as pl
from jax.experimental.pallas import tpu as pltpu

GRID_W = 64
EPS = 1e-6
ROPE_BASE = 10000.0
A_HEADS, A_KV_HEADS, A_HEAD_DIM, A_WINDOW = 8, 2, 64, 128
B_HEADS, B_Q_RANK, B_KV_RANK, B_NOPE_DIM, B_ROPE_DIM, B_V_DIM = 8, 256, 128, 64, 32, 64
C_HEADS, C_KEY_DIM, C_VAL_DIM, C_GATE_RANK, C_GATE_TAU, C_CHUNK = 4, 64, 128, 16, 16.0, 64
CONV_W = 3

LANES = 128
SUBLANES = 8
VMEM_BYTES = 64 * 1024 * 1024

TM = 256
GLA_SUB = 16
MLA_TK = 256
MLA_UNROLL = 32
MOD_COLS = 3072
FFN_COLS = 256
F32 = jnp.float32
BF16 = jnp.bfloat16
_HI = lax.Precision.HIGHEST
_NEG_INF = float("-inf")


def _dot(a, b):
    return jnp.dot(a, b, preferred_element_type=F32)


def _dot_nt(a, b):
    return lax.dot_general(a, b, (((1,), (1,)), ((), ())), preferred_element_type=F32)


def _split_bf16(x, terms):
    parts = []
    for _ in range(terms):
        p = x.astype(BF16)
        parts.append(p)
        x = x - p.astype(F32)
    return parts


def _dot_split(a, b):
    a_hi, a_lo = _split_bf16(a, 2)
    b_hi, b_lo = _split_bf16(b, 2)
    return _dot(a_hi, b_hi) + (_dot(a_lo, b_hi) + _dot(a_hi, b_lo))


def _rms(x, w):
    return x * lax.rsqrt(jnp.mean(x * x, axis=-1, keepdims=True) + EPS) * w


def _swap_lane_groups(x, half):
    lane = lax.broadcasted_iota(jnp.int32, x.shape, 1)
    up = pltpu.roll(x, LANES - half, 1)
    down = pltpu.roll(x, half, 1)
    return jnp.where((lane & half) == 0, up, down)


def _vmem_limit(nbytes):
    return int(min(VMEM_BYTES - (4 << 20), max(nbytes, 32 << 20)))


def _layer_spec(stacked, layer):
    shape = stacked.shape[1:]
    nd = len(shape)
    return pl.BlockSpec((None,) + shape, lambda *_: (layer,) + (0,) * nd, pipeline_mode=pl.Buffered(1))


def _const_spec(shape):
    nd = len(shape)
    return pl.BlockSpec(shape, lambda *_: (0,) * nd, pipeline_mode=pl.Buffered(1))


def _mod_kernel(c_ref, w_ref, b_ref, o_ref):
    cv = c_ref[...]
    sc = cv * jax.nn.sigmoid(cv)
    o_ref[...] = jnp.dot(sc, w_ref[...], precision=_HI, preferred_element_type=F32) + b_ref[...]


def _modulation(cvec, w_mod, b_mod):
    L, D, N = w_mod.shape
    nb = N // MOD_COLS
    return pl.pallas_call(
        _mod_kernel,
        grid=(L, nb),
        in_specs=[
            pl.BlockSpec((SUBLANES, D), lambda l, j: (0, 0)),
            pl.BlockSpec((None, D, MOD_COLS), lambda l, j: (l, 0, j)),
            pl.BlockSpec((None, 1, MOD_COLS), lambda l, j: (l, 0, j)),
        ],
        out_specs=pl.BlockSpec((None, SUBLANES, MOD_COLS), lambda l, j: (l, 0, j)),
        out_shape=jax.ShapeDtypeStruct((L, SUBLANES, N), F32),
        compiler_params=pltpu.CompilerParams(
            dimension_semantics=("parallel", "parallel"),
            vmem_limit_bytes=_vmem_limit(3 * D * MOD_COLS * 4)),
    )(cvec, w_mod, b_mod.reshape(L, 1, N))


_C_AQ = 0
_C_AK = _C_AQ + A_HEADS * A_HEAD_DIM
_C_AV = _C_AK + A_KV_HEADS * A_HEAD_DIM
_C_BQ = _C_AV + A_KV_HEADS * A_HEAD_DIM
_C_BKV = _C_BQ + B_Q_RANK
_C_BKR = _C_BKV + B_KV_RANK
_C_CQ = _C_BKR + LANES
_C_CK = _C_CQ + C_HEADS * C_KEY_DIM
_C_CV = _C_CK + C_HEADS * C_KEY_DIM
_C_CR = _C_CV + C_HEADS * C_VAL_DIM
_C_GATE = _C_CR + C_HEADS * C_VAL_DIM
B_HEAD_PAD = LANES
B_V_ROWS = B_V_DIM + 2 * SUBLANES
A_V_ROWS = A_HEAD_DIM + 2 * SUBLANES
LOG2E = 1.4426950408889634


def _stream_specs(xc, xl):
    off = 0 if xc is xl else 1
    d = xl.shape[2]
    spec_c = pl.BlockSpec((None, TM, d), lambda b, t: (b, 0, 0))
    spec_l = pl.BlockSpec((None, TM, d), lambda b, t: (b, jnp.maximum(t - off, 0), 0))
    return spec_c, spec_l, xl.shape[1] + off * TM


def _stream_tile(xc_ref, xl_ref):
    return jnp.where(pl.program_id(1) == 0, xc_ref[...], xl_ref[...])


def _inproj_kernel(xc_ref, xl_ref, mod_ref, nw_ref, w1_ref, wkv_ref, w2_ref, w3_ref, rope_ref,
                   bqn_ref, bkvn_ref, wuq_ref, wkn_ref,
                   wvb_ref, wg_ref, bg_ref,
                   qa_ref, ka_ref, va_ref, qb_ref, kb_ref, vb_ref, cq_ref, ck_ref, cv_ref, cr_ref,
                   g_ref, gate_ref, *, d_model):
    mod = mod_ref[...]
    h = _rms(_stream_tile(xc_ref, xl_ref), nw_ref[...]) * (1.0 + mod[1:2]) + mod[0:1]
    hb = h.astype(BF16)

    pieces = ((_C_AQ, w1_ref), (_C_BKV, wkv_ref), (_C_CQ, w2_ref), (_C_GATE, w3_ref))

    def proj(lo, hi):
        base, ref = [(b0, r) for b0, r in pieces if b0 <= lo][-1]
        assert hi - base <= ref.shape[1]
        return _dot(hb, ref[:, lo - base:hi - base])

    lane = lax.broadcasted_iota(jnp.int32, (TM, LANES), 1)

    cos_a, sin_a, cos_b, sin_b = rope_ref[0], rope_ref[1], rope_ref[2], rope_ref[3]

    def rope_a(t):
        return t * cos_a + _swap_lane_groups(t, A_HEAD_DIM // 2) * sin_a

    def rope_b(t):
        return t * cos_b + _swap_lane_groups(t, B_ROPE_DIM // 2) * sin_b

    a_scale = A_HEAD_DIM ** -0.5 * LOG2E
    aq = proj(_C_AQ, _C_AK)
    for c in range(A_HEADS * A_HEAD_DIM // LANES):
        qa_ref[c * LANES:(c + 1) * LANES, :] = (rope_a(aq[:, c * LANES:(c + 1) * LANES]) * a_scale).T.astype(BF16)
    akv = proj(_C_AK, _C_BQ)
    kvw = A_KV_HEADS * A_HEAD_DIM
    ka_ref[...] = rope_a(akv[:, 0:kvw]).astype(BF16)
    va_t = akv[:, kvw:2 * kvw].T.astype(BF16)
    a_ones = jnp.where(lax.broadcasted_iota(jnp.int32, (A_V_ROWS - A_HEAD_DIM, va_t.shape[1]), 0) == 0,
                       1.0, 0.0).astype(BF16)
    for g in range(A_KV_HEADS):
        va_ref[g * A_V_ROWS:g * A_V_ROWS + A_HEAD_DIM, :] = va_t[g * A_HEAD_DIM:(g + 1) * A_HEAD_DIM]
        va_ref[g * A_V_ROWS + A_HEAD_DIM:(g + 1) * A_V_ROWS, :] = a_ones

    b_scale = (B_NOPE_DIM + B_ROPE_DIM) ** -0.5 * LOG2E
    cqn = _rms(proj(_C_BQ, _C_BKV), bqn_ref[...]).astype(BF16)
    bkv = proj(_C_BKV, _C_CQ)
    ckvn = _rms(bkv[:, 0:B_KV_RANK], bkvn_ref[...]).astype(BF16)
    mixed = bkv[:, B_KV_RANK:B_KV_RANK + LANES]
    g_low = mixed
    kr = rope_b(jnp.where(lane >= B_NOPE_DIM, mixed, 0.0))
    for pair in range(B_HEADS // 2):
        sl2 = slice(2 * pair * B_HEAD_PAD, 2 * (pair + 1) * B_HEAD_PAD)
        q2 = _dot(cqn, wuq_ref[:, sl2])
        k2 = _dot(ckvn, wkn_ref[:, sl2])
        for i in range(2):
            sl = slice((2 * pair + i) * B_HEAD_PAD, (2 * pair + i + 1) * B_HEAD_PAD)
            part = slice(i * B_HEAD_PAD, (i + 1) * B_HEAD_PAD)
            qb_ref[sl, :] = (rope_b(q2[:, part]) * b_scale).T.astype(BF16)
            kb_ref[:, sl] = (k2[:, part] + kr).astype(BF16)
    vb = _dot(ckvn, wvb_ref[...])
    tk = vb_ref.shape[2]
    per = LANES // B_V_DIM
    ones_rows = jnp.where(lax.broadcasted_iota(jnp.int32, (B_V_ROWS - B_V_DIM, tk), 0) == 0, 1.0, 0.0).astype(BF16)
    for cc in range(vb_ref.shape[0]):
        for c in range(vb.shape[1] // LANES):
            v_t = vb[cc * tk:(cc + 1) * tk, c * LANES:(c + 1) * LANES].T.astype(BF16)
            for i in range(per):
                r0 = (c * per + i) * B_V_ROWS
                vb_ref[cc, r0:r0 + B_V_DIM, :] = v_t[i * B_V_DIM:(i + 1) * B_V_DIM]
                vb_ref[cc, r0 + B_V_DIM:r0 + B_V_ROWS, :] = ones_rows

    cq_ref[...] = proj(_C_CQ, _C_CK) * (C_KEY_DIM ** -0.5)
    ck_ref[...] = proj(_C_CK, _C_CV)
    cv_ref[...] = proj(_C_CV, _C_CR).astype(BF16)
    cr_ref[...] = proj(_C_CR, _C_GATE).astype(BF16)
    z = _dot_split(g_low, wg_ref[...]) + bg_ref[...]
    g_ref[...] = (jnp.minimum(z, 0.0) - jnp.log1p(jnp.exp(-jnp.abs(z)))) * (1.0 / C_GATE_TAU)

    gate_ref[...] = proj(_C_GATE, _C_GATE + 3 * d_model).astype(BF16)

def _mod_index(b, t):
    return (jnp.where(t == 0, 0, b + 1), 0, 0)


def _inproj(Xc, Xl, mod, nw, layer, w_pieces, rope, bqn, bkvn, wuq, wkn, wvb, wg, bg):
    spec_c, spec_l, T = _stream_specs(Xc, Xl)
    B, _, D = Xl.shape
    NT = T // TM
    n_in = sum(w.shape[2] for w in w_pieces)
    row = lambda n: pl.BlockSpec((None, TM, n), lambda b, t: (b, t, 0))
    widths = [(A_HEADS * A_HEAD_DIM, BF16), (A_KV_HEADS * A_HEAD_DIM, BF16), (A_KV_HEADS * A_HEAD_DIM, BF16),
              (B_HEADS * B_HEAD_PAD, BF16), (B_HEADS * B_HEAD_PAD, BF16), (B_HEADS * B_V_DIM, BF16),
              (C_HEADS * C_KEY_DIM, F32), (C_HEADS * C_KEY_DIM, F32), (C_HEADS * C_VAL_DIM, BF16),
              (C_HEADS * C_VAL_DIM, BF16), (2 * C_HEADS * C_KEY_DIM, F32), (3 * D, BF16)]
    out_bytes = sum(TM * n * jnp.dtype(dt).itemsize for n, dt in widths)
    vmem = D * n_in * 2 + 4 * out_bytes + 6 * TM * D * 4 + (8 << 20)
    out_specs = [row(n) for n, _ in widths]
    out_shape = [jax.ShapeDtypeStruct((B, T, n), dt) for n, dt in widths]
    qw, vw = B_HEADS * B_HEAD_PAD, B_HEADS * B_V_ROWS
    out_specs[3] = pl.BlockSpec((None, qw, TM), lambda b, t: (b, 0, t))
    out_shape[3] = jax.ShapeDtypeStruct((B, qw, T), BF16)
    out_specs[0] = pl.BlockSpec((None, widths[0][0], TM), lambda b, t: (b, 0, t))
    out_shape[0] = jax.ShapeDtypeStruct((B, widths[0][0], T), BF16)
    out_specs[2] = pl.BlockSpec((None, A_KV_HEADS * A_V_ROWS, TM), lambda b, t: (b, 0, t))
    out_shape[2] = jax.ShapeDtypeStruct((B, A_KV_HEADS * A_V_ROWS, T), BF16)
    out_specs[5] = pl.BlockSpec((None, TM // MLA_TK, vw, MLA_TK), lambda b, t: (b, t, 0, 0))
    out_shape[5] = jax.ShapeDtypeStruct((B, T // MLA_TK, vw, MLA_TK), BF16)
    return pl.pallas_call(
        functools.partial(_inproj_kernel, d_model=D),
        grid=(B, NT),
        in_specs=[
            spec_c, spec_l,
            pl.BlockSpec((None, 6, D), _mod_index),
            _const_spec((1, D)),
            *[_layer_spec(w, layer) for w in w_pieces],
            pl.BlockSpec((4, TM, LANES), lambda b, t: (0, t, 0)),
            _const_spec(bqn.shape), _const_spec(bkvn.shape), _layer_spec(wuq, layer),
            _layer_spec(wkn, layer), _layer_spec(wvb, layer), _layer_spec(wg, layer), _const_spec(bg.shape),
        ],
        out_specs=out_specs,
        out_shape=out_shape,
        compiler_params=pltpu.CompilerParams(
            dimension_semantics=("parallel", "parallel"), vmem_limit_bytes=_vmem_limit(vmem)),
    )(Xc, Xl, mod, nw, *w_pieces, rope, bqn, bkvn, wuq, wkn, wvb, wg, bg)


def _attn_a_kernel(q_ref, kp_ref, km_ref, kn_ref, kc_ref, vp_ref, vm_ref, vn_ref, vc_ref, sink_ref,
                   o_ref, *, seq):
    t = pl.program_id(1)
    kloc = jnp.concatenate([kp_ref[...], km_ref[...], kn_ref[...]], axis=0)
    vloc = jnp.concatenate([vp_ref[...], vm_ref[...], vn_ref[...]], axis=1)
    kctx, vctx = kc_ref[...], vc_ref[...]
    nloc = kloc.shape[0]
    r = lax.broadcasted_iota(jnp.int32, (nloc, TM), 0)
    c = lax.broadcasted_iota(jnp.int32, (nloc, TM), 1)
    rel = r - A_WINDOW - c
    kpos = (t - 1) * TM - A_WINDOW + r
    valid = (jnp.abs(rel) <= A_WINDOW) & (kpos >= 0) & (kpos < seq) & (t >= 1)
    group = A_HEADS // A_KV_HEADS
    per = LANES // A_HEAD_DIM
    q_zero = jnp.zeros((A_HEAD_DIM, TM), BF16)

    def scores(hd):
        q_t = q_ref[hd * A_HEAD_DIM:(hd + 1) * A_HEAD_DIM, :]
        g = hd // group
        qm = jnp.concatenate([q_t if i == g else q_zero for i in range(A_KV_HEADS)], axis=0)
        return _dot(kloc, qm), _dot(kctx, qm)

    cur = scores(0)
    parts = []
    for hd in range(A_HEADS):
        nxt = scores(hd + 1) if hd + 1 < A_HEADS else None
        s_loc = jnp.where(valid, cur[0], _NEG_INF)
        s_ctx = cur[1]
        sink = sink_ref[hd:hd + 1, 0:1] * LOG2E
        m = jnp.maximum(jnp.maximum(jnp.max(s_loc, axis=0, keepdims=True),
                                    jnp.max(s_ctx, axis=0, keepdims=True)), sink)
        p_loc = jnp.exp2((s_loc - m).astype(BF16))
        p_ctx = jnp.exp2((s_ctx - m).astype(BF16))
        g = hd // group
        vs = slice(g * A_V_ROWS, (g + 1) * A_V_ROWS)
        o_t = _dot(vloc[vs], p_loc) + _dot(vctx[vs], p_ctx)
        denom = o_t[A_HEAD_DIM:A_HEAD_DIM + 1] + jnp.exp2(sink - m)
        parts.append(o_t[0:A_HEAD_DIM] / denom)
        if len(parts) == per:
            c = hd // per
            o_ref[:, c * LANES:(c + 1) * LANES] = jnp.concatenate(parts, axis=0).T.astype(BF16)
            parts = []
        cur = nxt


def _attn_a(qa, ka, va, sink, lc):
    B, qw, T = qa.shape
    NT = T // TM
    seq = T - lc
    kvw = ka.shape[2]
    vrows = va.shape[1]
    per_tm = TM // A_WINDOW
    last = T // A_WINDOW - 1
    prev_i = lambda t: jnp.maximum(t * per_tm - 1, 0)
    next_i = lambda t: jnp.minimum((t + 1) * per_tm, last)
    kprev = pl.BlockSpec((None, A_WINDOW, kvw), lambda b, t: (b, prev_i(t), 0))
    kmain = pl.BlockSpec((None, TM, kvw), lambda b, t: (b, t, 0))
    knext = pl.BlockSpec((None, A_WINDOW, kvw), lambda b, t: (b, next_i(t), 0))
    kctx = pl.BlockSpec((None, lc, kvw), lambda b, t: (b, 0, 0))
    vprev = pl.BlockSpec((None, vrows, A_WINDOW), lambda b, t: (b, 0, prev_i(t)))
    vmain = pl.BlockSpec((None, vrows, TM), lambda b, t: (b, 0, t))
    vnext = pl.BlockSpec((None, vrows, A_WINDOW), lambda b, t: (b, 0, next_i(t)))
    vctx = pl.BlockSpec((None, vrows, lc), lambda b, t: (b, 0, 0))
    return pl.pallas_call(
        functools.partial(_attn_a_kernel, seq=seq),
        grid=(B, NT),
        in_specs=[pl.BlockSpec((None, qw, TM), lambda b, t: (b, 0, t)),
                  kprev, kmain, knext, kctx, vprev, vmain, vnext, vctx,
                  pl.BlockSpec(sink.shape, lambda b, t: (0, 0))],
        out_specs=pl.BlockSpec((None, TM, qw), lambda b, t: (b, t, 0)),
        out_shape=jax.ShapeDtypeStruct((B, T, qw), BF16),
        compiler_params=pltpu.CompilerParams(dimension_semantics=("parallel", "parallel")),
    )(qa, ka, ka, ka, ka, va, va, va, va, sink)


def _mla_kernel(q_ref, k_ref, v_ref, o_ref, m_ref, acc_ref, sa_ref, sb_ref, *, lc, tk):
    t = pl.program_id(1)
    total = k_ref.shape[0]
    nchunks = jnp.where(t == 0, lc // tk, total // tk)
    m_ref[...] = jnp.full(m_ref.shape, _NEG_INF, F32)
    acc_ref[...] = jnp.zeros(acc_ref.shape, F32)

    def stage(dst_ref, j, hd):
        qs = slice(hd * B_HEAD_PAD, (hd + 1) * B_HEAD_PAD)
        s = _dot(k_ref[pl.ds(pl.multiple_of(j * tk, tk), tk), qs], q_ref[qs, :])
        dst_ref[hd, 0:tk, :] = s
        dst_ref[hd, tk:tk + SUBLANES, :] = jnp.broadcast_to(jnp.max(s, axis=0, keepdims=True), (SUBLANES, TM))

    for hd in range(B_HEADS):
        stage(sa_ref, 0, hd)

    def step(j, cur_ref, nxt_ref, last=False):
        for hd in range(B_HEADS):
            vs = slice(hd * B_V_ROWS, (hd + 1) * B_V_ROWS)
            if not last:
                stage(nxt_ref, j + 1, hd)
            m_prev = m_ref[hd]
            m_new = jnp.maximum(m_prev, cur_ref[hd, tk:tk + SUBLANES, :])
            alpha = jnp.exp2(m_prev - m_new)
            p = jnp.exp2((cur_ref[hd, 0:tk, :] - m_new[0:1]).astype(BF16))
            m_ref[hd] = m_new
            acc_ref[vs, :] = acc_ref[vs, :] * alpha[0:1] + _dot(v_ref[j, vs, :], p)

    bufs = (sa_ref, sb_ref)

    def body(g, carry):
        for i in range(MLA_UNROLL):
            step(MLA_UNROLL * g + i, bufs[i % 2], bufs[(i + 1) % 2])
        return carry

    ngroups = (nchunks - 1) // MLA_UNROLL
    lax.fori_loop(0, ngroups, body, 0)
    tail = (total // tk - 1) % MLA_UNROLL + 1
    assert tail == (lc // tk - 1) % MLA_UNROLL + 1
    for i in range(tail):
        step(MLA_UNROLL * ngroups + i, bufs[i % 2], bufs[(i + 1) % 2], last=(i == tail - 1))
    per = LANES // B_V_DIM
    for c in range(B_HEADS // per):
        o_t = jnp.concatenate(
            [acc_ref[hd * B_V_ROWS:hd * B_V_ROWS + B_V_DIM, :]
             / acc_ref[hd * B_V_ROWS + B_V_DIM:hd * B_V_ROWS + B_V_DIM + 1, :]
             for hd in range(c * per, (c + 1) * per)], axis=0)
        o_ref[:, c * LANES:(c + 1) * LANES] = o_t.T.astype(BF16)


def _mla(qb, kb, vb, lc):
    B, qw, T = qb.shape
    NT = T // TM
    _, nck, vw, tk = vb.shape
    ow = B_HEADS * B_V_DIM
    vmem = (T * (qw + vw) * 2 + 4 * TM * qw * 2 + (B_HEADS * SUBLANES * TM + 3 * TM * vw) * 4
            + 2 * B_HEADS * (tk + SUBLANES) * TM * 4 + (16 << 20))
    return pl.pallas_call(
        functools.partial(_mla_kernel, lc=lc, tk=tk),
        grid=(B, NT),
        in_specs=[pl.BlockSpec((None, qw, TM), lambda b, t: (b, 0, t)),
                  pl.BlockSpec((None, T, qw), lambda b, t: (b, 0, 0), pipeline_mode=pl.Buffered(1)),
                  pl.BlockSpec((None, nck, vw, tk), lambda b, t: (b, 0, 0, 0), pipeline_mode=pl.Buffered(1))],
        out_specs=pl.BlockSpec((None, TM, ow), lambda b, t: (b, t, 0)),
        out_shape=jax.ShapeDtypeStruct((B, T, ow), BF16),
        scratch_shapes=[pltpu.VMEM((B_HEADS, SUBLANES, TM), F32),
                        pltpu.VMEM((vw, TM), F32),
                        pltpu.VMEM((B_HEADS, tk + SUBLANES, TM), F32),
                        pltpu.VMEM((B_HEADS, tk + SUBLANES, TM), F32)],
        compiler_params=pltpu.CompilerParams(
            dimension_semantics=("parallel", "arbitrary"), vmem_limit_bytes=_vmem_limit(vmem)),
    )(qb, kb, vb)


def _gla_prep(q, k, b, reverse):
    C = q.shape[0]
    dk2 = q.shape[1]
    nsub = C // GLA_SUB
    b_end = b[0:1] if reverse else b[C - 1:C]
    row = lax.broadcasted_iota(jnp.int32, (C, dk2), 0)
    lane = lax.broadcasted_iota(jnp.int32, (C, dk2), 1)

    qe = (q * jnp.exp(b)).astype(BF16)
    kd_t = (k * jnp.exp(b_end - b)).T.astype(BF16)
    dec = jnp.exp(jnp.broadcast_to(b_end, (dk2, dk2)).T)
    dec2 = jnp.concatenate([dec] * (2 * C_VAL_DIM // dk2), axis=1)

    refs = [blk * GLA_SUB + (GLA_SUB - 1 if reverse else 0) for blk in range(nsub)]
    b_ref = jnp.concatenate([jnp.broadcast_to(b[n:n + 1], (GLA_SUB, dk2)) for n in refs], axis=0)
    q_sc = q * jnp.exp(b - b_ref)
    k_parts = []
    for blk, n in enumerate(refs):
        in_range = (row >= blk * GLA_SUB) if reverse else (row < (blk + 1) * GLA_SUB)
        k_parts.append(jnp.where(in_range, k * jnp.exp(b[n:n + 1] - b), 0.0))
    k_big = jnp.concatenate(k_parts, axis=1).astype(BF16)
    q_halves = []
    for hf in range(2):
        in_half = (lane >= hf * C_KEY_DIM) & (lane < (hf + 1) * C_KEY_DIM)
        q_halves.append(jnp.concatenate(
            [jnp.where(in_half & (row >= blk * GLA_SUB) & (row < (blk + 1) * GLA_SUB), q_sc, 0.0)
             for blk in range(nsub)], axis=1))
    q_big = jnp.concatenate(q_halves, axis=0).astype(BF16)
    return qe, kd_t, dec2, q_big, k_big


def _gla_kernel(qf_ref, kf_ref, vf_ref, gf_ref, qr_ref, kr_ref, vr_ref, gr_ref, of_ref, or_ref,
                sf_ref, sr_ref):
    t = pl.program_id(1)

    @pl.when(t == 0)
    def _():
        sf_ref[...] = jnp.zeros(sf_ref.shape, F32)
        sr_ref[...] = jnp.zeros(sr_ref.shape, F32)

    C = C_CHUNK
    nchunk = TM // C
    npair = C_HEADS // 2
    kw, vw = 2 * C_KEY_DIM, 2 * C_VAL_DIM
    dirs = ((qf_ref, kf_ref, vf_ref, gf_ref, of_ref, sf_ref, False),
            (qr_ref, kr_ref, vr_ref, gr_ref, or_ref, sr_ref, True))

    ii = lax.broadcasted_iota(jnp.int32, (TM, TM), 0)
    jj = lax.broadcasted_iota(jnp.int32, (TM, TM), 1)
    same_chunk = (ii // C) == (jj // C)
    b_all = []
    for (_, _, _, g_ref, _, _, reverse) in dirs:
        tri = jnp.where(same_chunk & ((jj >= ii) if reverse else (jj <= ii)), 1.0, 0.0).astype(BF16)
        g_hi, g_mid, g_lo = _split_bf16(g_ref[...], 3)
        b_all.append(_dot(tri, g_hi) + (_dot(tri, g_mid) + _dot(tri, g_lo)))

    units = []
    for d, (q_ref, k_ref, v_ref, _, o_ref, _, reverse) in enumerate(dirs):
        for p in range(npair):
            ks, vsl = slice(p * kw, (p + 1) * kw), slice(p * vw, (p + 1) * vw)
            for c in range(nchunk):
                rows = slice(c * C, (c + 1) * C)
                prep = _gla_prep(q_ref[rows, ks], k_ref[rows, ks], b_all[d][rows, ks], reverse)
                units.append((d, p, c, rows, vsl, v_ref, o_ref, reverse, prep))

    srow = lax.broadcasted_iota(jnp.int32, (kw, vw), 0)
    scol = lax.broadcasted_iota(jnp.int32, (kw, vw), 1)
    on_diag = (srow < C_KEY_DIM) == (scol < C_VAL_DIM)
    qi = lax.broadcasted_iota(jnp.int32, (2 * C, C), 0) % C
    kj = lax.broadcasted_iota(jnp.int32, (2 * C, C), 1)
    kvs, atts = [], []
    for (d, p, c, rows, vsl, v_ref, o_ref, reverse, prep) in units:
        _, kd_t, _, q_big, k_big = prep
        kvs.append(jnp.where(on_diag, _dot(kd_t, v_ref[rows, vsl]), 0.0))
        keep = (kj >= qi) if reverse else (kj <= qi)
        atts.append(jnp.where(keep, _dot_nt(q_big, k_big), 0.0).astype(BF16))

    intra = {}
    for u, (d, p, c, rows, vsl, v_ref, o_ref, reverse, prep) in enumerate(units):
        pv = _dot(atts[u], v_ref[rows, vsl])
        intra[(d, p, c)] = (u, jnp.concatenate([pv[0:C, 0:C_VAL_DIM], pv[C:2 * C, C_VAL_DIM:]], axis=1))

    states = {(d, p): dirs[d][5][p] for d in range(len(dirs)) for p in range(npair)}
    for step in range(nchunk):
        for d in range(len(dirs)):
            reverse = dirs[d][6]
            c = nchunk - 1 - step if reverse else step
            for p in range(npair):
                u, o_intra = intra[(d, p, c)]
                _, _, _, rows, vsl, _, o_ref, _, prep = units[u]
                qe, _, dec2, _, _ = prep
                o_ref[rows, vsl] = (o_intra + _dot(qe, states[(d, p)].astype(BF16))).astype(o_ref.dtype)
                states[(d, p)] = dec2 * states[(d, p)] + kvs[u]
    for d in range(len(dirs)):
        for p in range(npair):
            dirs[d][5][p] = states[(d, p)]


def _gla(cq, ck, cv, g):
    B, T, kw = cq.shape
    NT = T // TM
    vw = cv.shape[2]

    def fwd(n, col=0):
        return pl.BlockSpec((None, TM, n), lambda b, t: (b, t, col))

    def rev(n, col=0):
        return pl.BlockSpec((None, TM, n), lambda b, t: (b, jnp.where(t == 0, 0, NT - t), col))

    return pl.pallas_call(
        _gla_kernel,
        grid=(B, NT),
        in_specs=[fwd(kw), fwd(kw), fwd(vw), fwd(kw, 0), rev(kw), rev(kw), rev(vw), rev(kw, 1)],
        out_specs=[fwd(vw), rev(vw)],
        out_shape=[jax.ShapeDtypeStruct((B, T, vw), BF16)] * 2,
        scratch_shapes=[pltpu.VMEM((C_HEADS // 2, 2 * C_KEY_DIM, 2 * C_VAL_DIM), F32)] * 2,
        compiler_params=pltpu.CompilerParams(dimension_semantics=("parallel", "arbitrary")),
    )(cq, ck, cv, g, cq, ck, cv, g)


def _merge_kernel(xc_ref, xl_ref, mod_ref, ya_ref, yb_ref, of_ref, or_ref, cr_ref, gate_ref, hn_ref,
                  wa_ref, wb_ref, wc_ref, wo_ref, o_ref):
    d = xl_ref.shape[1]
    o = of_ref[...].astype(F32) + or_ref[...].astype(F32)
    r = cr_ref[...].astype(F32)
    parts = []
    for hd in range(C_HEADS):
        sl = slice(hd * C_VAL_DIM, (hd + 1) * C_VAL_DIM)
        parts.append(_rms(o[:, sl], hn_ref[:, sl]))
    yc = (jnp.concatenate(parts, axis=1) * (r * jax.nn.sigmoid(r))).astype(BF16)

    def gate(i):
        return jax.nn.sigmoid(gate_ref[:, i * d:(i + 1) * d].astype(F32))

    m = (gate(0) * _dot(ya_ref[...], wa_ref[...]) + gate(1) * _dot(yb_ref[...], wb_ref[...])
         + gate(2) * _dot(yc, wc_ref[...]))
    y = _dot(m.astype(BF16), wo_ref[...])
    o_ref[...] = _stream_tile(xc_ref, xl_ref) + mod_ref[2:3, :] * y


def _merge(Xc, Xl, mod, ya, yb, of, orv, cr, gates, hn, layer, wa, wb, wc, wo):
    spec_c, spec_l, T = _stream_specs(Xc, Xl)
    B, _, D = Xl.shape
    NT = T // TM
    row = lambda n: pl.BlockSpec((None, TM, n), lambda b, t: (b, t, 0))
    wbytes = sum(w.size // w.shape[0] for w in (wa, wb, wc, wo)) * 2
    vmem = wbytes + 2 * TM * (2 * D + 3 * D + 4 * ya.shape[2]) * 4 + (16 << 20)
    return pl.pallas_call(
        _merge_kernel,
        grid=(B, NT),
        in_specs=[spec_c, spec_l, pl.BlockSpec((None, 6, D), _mod_index),
                  row(ya.shape[2]), row(yb.shape[2]), row(of.shape[2]), row(orv.shape[2]),
                  row(cr.shape[2]), row(gates.shape[2]), _const_spec(hn.shape),
                  _layer_spec(wa, layer), _layer_spec(wb, layer), _layer_spec(wc, layer), _layer_spec(wo, layer)],
        out_specs=row(D),
        out_shape=jax.ShapeDtypeStruct((B, T, D), F32),
        compiler_params=pltpu.CompilerParams(
            dimension_semantics=("parallel", "parallel"), vmem_limit_bytes=_vmem_limit(vmem)),
    )(Xc, Xl, mod, ya, yb, of, orv, cr, gates, hn, wa, wb, wc, wo)


def _ffn_kernel(x_ref, xp_ref, xn_ref, mod_ref, nw_ref, wup_ref, cw_ref, cb_ref, wdn_ref, fw_ref, o_ref,
                u_ref, act_ref, *, nt, ncb, t0, final):
    t = pl.program_id(1) + t0
    mod = mod_ref[...]
    nw = nw_ref[...]
    halo = xp_ref.shape[0]
    ffn = wdn_ref.shape[0]
    cb = ffn // ncb

    def norm(xv):
        return _rms(xv, nw) * (1.0 + mod[4:5]) + mod[3:4]

    has_prev = (t >= 2)
    has_next = (t >= 1) & (t < nt - 1)
    hb = jnp.concatenate([norm(x_ref[...]).astype(BF16),
                          jnp.concatenate([jnp.where(has_prev, norm(xp_ref[...]), 0.0),
                                           jnp.where(has_next, norm(xn_ref[...]), 0.0)], axis=0).astype(BF16)], axis=0)
    row8 = lax.broadcasted_iota(jnp.int32, (SUBLANES, cb), 0)

    def up(j):
        u_ref[j % 2, 0] = _dot(hb, wup_ref[:, j * cb:(j + 1) * cb])
        u_ref[j % 2, 1] = _dot(hb, wup_ref[:, ffn + j * cb:ffn + (j + 1) * cb])

    def conv(slot, half, lo):
        w = cw_ref[:, lo:lo + cb]
        um = u_ref[slot, half, 0:TM, :]
        prev_row = u_ref[slot, half, TM + halo - 1:TM + halo, :]
        next_row = u_ref[slot, half, TM + halo:TM + halo + 1, :]
        rd = pltpu.roll(um, 1, 0)
        ru = pltpu.roll(um, TM - 1, 0)
        sd = jnp.concatenate([jnp.where(row8 == 0, prev_row, rd[0:SUBLANES]), rd[SUBLANES:]], axis=0)
        su = jnp.concatenate([ru[:TM - SUBLANES], jnp.where(row8 == SUBLANES - 1, next_row, ru[TM - SUBLANES:])],
                             axis=0)
        return sd * w[0:1] + um * w[1:2] + su * w[2:3] + cb_ref[:, lo:lo + cb]

    def act(j):
        gt = conv(j % 2, 0, j * cb)
        val = conv(j % 2, 1, ffn + j * cb)
        act_ref[:, j * cb:(j + 1) * cb] = (gt * jax.nn.sigmoid(gt) * val).astype(BF16)

    up(0)
    for j in range(1, ncb):
        up(j)
        act(j - 1)
    act(ncb - 1)
    y = x_ref[...] + mod[5:6] * _dot(act_ref[...], wdn_ref[...])
    o_ref[...] = _rms(y, fw_ref[...]) if final else y


def _ffn(X, mod, nw, layer, wup, cw, cb, wdn, fw, final):
    B, T, D = X.shape
    NT = T // TM
    t0 = 1 if final else 0
    halo = SUBLANES
    per = TM // halo
    last = T // halo - 1
    ffn = wdn.shape[1]
    cbw = FFN_COLS
    ncb = ffn // cbw
    vmem = 3 * D * ffn * 2 + 4 * (TM + 2 * halo) * cbw * 4 + TM * ffn * 2 + 8 * TM * D * 4 + (16 << 20)
    return pl.pallas_call(
        functools.partial(_ffn_kernel, nt=NT, ncb=ncb, t0=t0, final=final),
        grid=(B, NT - t0),
        in_specs=[pl.BlockSpec((None, TM, D), lambda b, t: (b, t + t0, 0)),
                  pl.BlockSpec((None, halo, D), lambda b, t: (b, jnp.maximum((t + t0) * per - 1, 0), 0)),
                  pl.BlockSpec((None, halo, D), lambda b, t: (b, jnp.minimum((t + t0 + 1) * per, last), 0)),
                  pl.BlockSpec((None, 6, D), lambda b, t: _mod_index(b, t + t0)),
                  _const_spec(nw.shape), _layer_spec(wup, layer), _const_spec(cw.shape), _const_spec(cb.shape),
                  _layer_spec(wdn, layer), _const_spec(fw.shape)],
        out_specs=pl.BlockSpec((None, TM, D), lambda b, t: (b, t, 0)),
        out_shape=jax.ShapeDtypeStruct((B, T - t0 * TM, D), F32),
        scratch_shapes=[pltpu.VMEM((2, 2, TM + 2 * halo, cbw), F32), pltpu.VMEM((TM, ffn), BF16)],
        compiler_params=pltpu.CompilerParams(
            dimension_semantics=("parallel", "parallel"), vmem_limit_bytes=_vmem_limit(vmem)),
    )(X, X, X, mod, nw, wup, cw, cb, wdn, fw)


def _rope_tables(seq, lc):
    rows = seq // GRID_W
    row = jnp.broadcast_to(jnp.arange(rows)[:, None], (rows, GRID_W)).reshape(-1).astype(F32)
    col = jnp.broadcast_to(jnp.arange(GRID_W)[None, :], (rows, GRID_W)).reshape(-1).astype(F32)

    def cs(rot_dim):
        n_freq = rot_dim // 4
        inv = ROPE_BASE ** (-jnp.arange(n_freq, dtype=F32) / n_freq)
        ang = jnp.concatenate([row[:, None] * inv, col[:, None] * inv], axis=-1)
        return jnp.cos(ang), jnp.sin(ang)

    ca, sa = cs(A_HEAD_DIM)
    cb, sb = cs(B_ROPE_DIM)
    reps = LANES // A_HEAD_DIM
    cos_a = jnp.tile(jnp.concatenate([ca, ca], axis=1), (1, reps))
    sin_a = jnp.tile(jnp.concatenate([-sa, sa], axis=1), (1, reps))
    one = jnp.ones((seq, B_NOPE_DIM), F32)
    pad = LANES - B_NOPE_DIM - B_ROPE_DIM
    cos_b = jnp.concatenate([one, cb, cb, jnp.ones((seq, pad), F32)], axis=1)
    sin_b = jnp.concatenate([0 * one, -sb, sb, jnp.zeros((seq, pad), F32)], axis=1)
    ident = jnp.stack([jnp.ones((lc, LANES), F32), jnp.zeros((lc, LANES), F32)] * 2)
    return jnp.concatenate([ident, jnp.stack([cos_a, sin_a, cos_b, sin_b])], axis=1)


def _prep_w_in(w_in):
    o_kv = _C_BKV
    o_kr = o_kv + B_KV_RANK
    o_cq = o_kr + B_ROPE_DIM
    o_cg = o_cq + (_C_GATE - _C_CQ)
    o_gate = o_cg + 2 * C_GATE_RANK
    zeros = lambda n: jnp.zeros(w_in.shape[:2] + (n,), w_in.dtype)
    gap = B_NOPE_DIM - 2 * C_GATE_RANK
    wkv = jnp.concatenate([w_in[:, :, o_kv:o_kr], w_in[:, :, o_cg:o_gate], zeros(gap), w_in[:, :, o_kr:o_cq],
                           zeros(LANES - B_NOPE_DIM - B_ROPE_DIM)], axis=-1)
    return tuple(p.astype(BF16) for p in (w_in[:, :, :o_kv], wkv, w_in[:, :, o_cq:o_cg], w_in[:, :, o_gate:]))


def kernel(x, c, ctx, c_ctx, w_mod, b_mod, norm_mix, norm_ffn, w_in, a_sink, b_q_norm, b_kv_norm, b_w_uq, b_w_ukv, c_w_gate, c_b_gate, c_head_norm, w_br_a, w_br_b, w_br_c, w_out, w_up, conv_w, conv_b, w_down, final_norm):
    B, S, D = x.shape
    lc = ctx.shape[1]
    L = w_mod.shape[0]
    assert lc == TM and S % TM == 0 and S % GRID_W == 0 and B + 1 <= SUBLANES

    Xc, Xl = ctx, x
    cvec = jnp.zeros((SUBLANES, D), F32).at[0].set(c_ctx).at[1:B + 1].set(c)
    mods = _modulation(cvec, w_mod, b_mod).reshape(L, SUBLANES, 6, D)

    rope = _rope_tables(S, lc)
    w_in_r = _prep_w_in(w_in)
    sink = jnp.broadcast_to(a_sink[:, :, None], (L, A_HEADS, LANES)).astype(F32)
    wa, wb, wc, wo = w_br_a.astype(BF16), w_br_b.astype(BF16), w_br_c.astype(BF16), w_out.astype(BF16)
    qk = B_NOPE_DIM + B_ROPE_DIM
    wuq = jnp.pad(b_w_uq.reshape(L, B_Q_RANK, B_HEADS, qk),
                  ((0, 0), (0, 0), (0, 0), (0, B_HEAD_PAD - qk))).reshape(L, B_Q_RANK, -1).astype(BF16)
    ukv = b_w_ukv.reshape(L, B_KV_RANK, B_HEADS, B_NOPE_DIM + B_V_DIM)
    wkn = jnp.pad(ukv[..., :B_NOPE_DIM],
                  ((0, 0), (0, 0), (0, 0), (0, B_HEAD_PAD - B_NOPE_DIM))).reshape(L, B_KV_RANK, -1).astype(BF16)
    wvb = ukv[..., B_NOPE_DIM:].reshape(L, B_KV_RANK, -1).astype(BF16)
    nqk = C_HEADS * C_KEY_DIM
    wg = jnp.zeros((L, LANES, 2 * nqk), F32)
    wg = wg.at[:, 0:C_GATE_RANK, 0:nqk].set(c_w_gate[:, 0]).at[:, C_GATE_RANK:2 * C_GATE_RANK, nqk:].set(c_w_gate[:, 1])
    bg = c_b_gate.reshape(L, 1, 2 * nqk)
    wup, wdn = w_up.astype(BF16), w_down.astype(BF16)

    for l in range(L):
        mod = mods[l]
        (qa, ka, va, qb, kb, vb, cq, ck, cv, cr, g, gates) = _inproj(
            Xc, Xl, mod, norm_mix[l][None], l, w_in_r, rope, b_q_norm[l][None], b_kv_norm[l][None],
            wuq, wkn, wvb, wg, bg[l])
        ya = _attn_a(qa, ka, va, sink[l], lc)
        yb = _mla(qb, kb, vb, lc)
        of, orv = _gla(cq, ck, cv, g)
        X = _merge(Xc, Xl, mod, ya, yb, of, orv, cr, gates, c_head_norm[l][None], l, wa, wb, wc, wo)
        X = _ffn(X, mod, norm_ffn[l][None], l, wup, conv_w[l], conv_b[l][None], wdn, final_norm[None],
                 final=(l == L - 1))
        Xc = Xl = X
    return X
```

```python
import functools

import jax
import jax.numpy as jnp
from jax import lax
from jax.experimental import pallas as pl
from jax.experimental.pallas import tpu as pltpu

GRID_W = 64
EPS = 1e-6
ROPE_BASE = 10000.0
A_HEADS, A_KV_HEADS, A_HEAD_DIM, A_WINDOW = 8, 2, 64, 128
B_HEADS, B_Q_RANK, B_KV_RANK, B_NOPE_DIM, B_ROPE_DIM, B_V_DIM = 8, 256, 128, 64, 32, 64
C_HEADS, C_KEY_DIM, C_VAL_DIM, C_GATE_RANK, C_GATE_TAU, C_CHUNK = 4, 64, 128, 16, 16.0, 64
CONV_W = 3

LANES = 128
SUBLANES = 8
VMEM_BYTES = 64 * 1024 * 1024

TM = 256
GLA_SUB = 16
A_AHEAD = 4
MLA_TK = 256
MLA_UNROLL = 32
MOD_COLS = 3072
FFN_COLS = 256
F32 = jnp.float32
BF16 = jnp.bfloat16
_HI = lax.Precision.HIGHEST
_NEG_INF = float("-inf")


def _dot(a, b):
    return jnp.dot(a, b, preferred_element_type=F32)


def _dot_nt(a, b):
    return lax.dot_general(a, b, (((1,), (1,)), ((), ())), preferred_element_type=F32)


def _split_bf16(x, terms):
    parts = []
    for _ in range(terms):
        p = x.astype(BF16)
        parts.append(p)
        x = x - p.astype(F32)
    return parts


def _dot_split(a, b):
    a_hi, a_lo = _split_bf16(a, 2)
    b_hi, b_lo = _split_bf16(b, 2)
    return _dot(a_hi, b_hi) + (_dot(a_lo, b_hi) + _dot(a_hi, b_lo))


def _rms(x, w):
    return x * lax.rsqrt(jnp.mean(x * x, axis=-1, keepdims=True) + EPS) * w


def _swap_lane_groups(x, half):
    lane = lax.broadcasted_iota(jnp.int32, x.shape, 1)
    up = pltpu.roll(x, LANES - half, 1)
    down = pltpu.roll(x, half, 1)
    return jnp.where((lane & half) == 0, up, down)


def _vmem_limit(nbytes):
    return int(min(VMEM_BYTES - (4 << 20), max(nbytes, 32 << 20)))


def _layer_spec(stacked, layer):
    shape = stacked.shape[1:]
    nd = len(shape)
    return pl.BlockSpec((None,) + shape, lambda *_: (layer,) + (0,) * nd, pipeline_mode=pl.Buffered(1))


def _const_spec(shape):
    nd = len(shape)
    return pl.BlockSpec(shape, lambda *_: (0,) * nd, pipeline_mode=pl.Buffered(1))


def _mod_kernel(c_ref, w_ref, b_ref, o_ref):
    cv = c_ref[...]
    sc = cv * jax.nn.sigmoid(cv)
    o_ref[...] = jnp.dot(sc, w_ref[...], precision=_HI, preferred_element_type=F32) + b_ref[...]


def _modulation(cvec, w_mod, b_mod):
    L, D, N = w_mod.shape
    nb = N // MOD_COLS
    return pl.pallas_call(
        _mod_kernel,
        grid=(L, nb),
        in_specs=[
            pl.BlockSpec((SUBLANES, D), lambda l, j: (0, 0)),
            pl.BlockSpec((None, D, MOD_COLS), lambda l, j: (l, 0, j)),
            pl.BlockSpec((None, 1, MOD_COLS), lambda l, j: (l, 0, j)),
        ],
        out_specs=pl.BlockSpec((None, SUBLANES, MOD_COLS), lambda l, j: (l, 0, j)),
        out_shape=jax.ShapeDtypeStruct((L, SUBLANES, N), F32),
        compiler_params=pltpu.CompilerParams(
            dimension_semantics=("parallel", "parallel"),
            vmem_limit_bytes=_vmem_limit(3 * D * MOD_COLS * 4)),
    )(cvec, w_mod, b_mod.reshape(L, 1, N))


_C_AQ = 0
_C_AK = _C_AQ + A_HEADS * A_HEAD_DIM
_C_AV = _C_AK + A_KV_HEADS * A_HEAD_DIM
_C_BQ = _C_AV + A_KV_HEADS * A_HEAD_DIM
_C_BKV = _C_BQ + B_Q_RANK
_C_BKR = _C_BKV + B_KV_RANK
_C_CQ = _C_BKR + LANES
_C_CK = _C_CQ + C_HEADS * C_KEY_DIM
_C_CV = _C_CK + C_HEADS * C_KEY_DIM
_C_CR = _C_CV + C_HEADS * C_VAL_DIM
_C_GATE = _C_CR + C_HEADS * C_VAL_DIM
B_HEAD_PAD = LANES
B_V_ROWS = B_V_DIM + 2 * SUBLANES
A_V_ROWS = A_HEAD_DIM + 2 * SUBLANES
LOG2E = 1.4426950408889634


def _stream_specs(xc, xl):
    off = 0 if xc is xl else 1
    d = xl.shape[2]
    spec_c = pl.BlockSpec((None, TM, d), lambda b, t: (b, 0, 0))
    spec_l = pl.BlockSpec((None, TM, d), lambda b, t: (b, jnp.maximum(t - off, 0), 0))
    return spec_c, spec_l, xl.shape[1] + off * TM


def _stream_tile(xc_ref, xl_ref):
    return jnp.where(pl.program_id(1) == 0, xc_ref[...], xl_ref[...])


def _inproj_kernel(xc_ref, xl_ref, mod_ref, nw_ref, w1_ref, wkv_ref, w2_ref, w3_ref, rope_ref,
                   bqn_ref, bkvn_ref, wuq_ref, wkn_ref,
                   wvb_ref, wg_ref, bg_ref,
                   qa_ref, ka_ref, va_ref, qb_ref, kb_ref, vb_ref, cq_ref, ck_ref, cv_ref, cr_ref,
                   g_ref, gate_ref, *, d_model):
    mod = mod_ref[...]
    h = _rms(_stream_tile(xc_ref, xl_ref), nw_ref[...]) * (1.0 + mod[1:2]) + mod[0:1]
    hb = h.astype(BF16)

    pieces = ((_C_AQ, w1_ref), (_C_BKV, wkv_ref), (_C_CQ, w2_ref), (_C_GATE, w3_ref))

    def proj(lo, hi):
        base, ref = [(b0, r) for b0, r in pieces if b0 <= lo][-1]
        assert hi - base <= ref.shape[1]
        return _dot(hb, ref[:, lo - base:hi - base])

    lane = lax.broadcasted_iota(jnp.int32, (TM, LANES), 1)

    cos_a, sin_a, cos_b, sin_b = rope_ref[0], rope_ref[1], rope_ref[2], rope_ref[3]

    def rope_a(t):
        return t * cos_a + _swap_lane_groups(t, A_HEAD_DIM // 2) * sin_a

    def rope_b(t):
        return t * cos_b + _swap_lane_groups(t, B_ROPE_DIM // 2) * sin_b

    a_scale = A_HEAD_DIM ** -0.5 * LOG2E
    aq = proj(_C_AQ, _C_AK)
    for c in range(A_HEADS * A_HEAD_DIM // LANES):
        qa_ref[c * LANES:(c + 1) * LANES, :] = (rope_a(aq[:, c * LANES:(c + 1) * LANES]) * a_scale).T.astype(BF16)
    akv = proj(_C_AK, _C_BQ)
    kvw = A_KV_HEADS * A_HEAD_DIM
    ka_ref[...] = rope_a(akv[:, 0:kvw]).astype(BF16)
    va_t = akv[:, kvw:2 * kvw].T.astype(BF16)
    a_ones = jnp.where(lax.broadcasted_iota(jnp.int32, (A_V_ROWS - A_HEAD_DIM, va_t.shape[1]), 0) == 0,
                       1.0, 0.0).astype(BF16)
    for g in range(A_KV_HEADS):
        va_ref[g * A_V_ROWS:g * A_V_ROWS + A_HEAD_DIM, :] = va_t[g * A_HEAD_DIM:(g + 1) * A_HEAD_DIM]
        va_ref[g * A_V_ROWS + A_HEAD_DIM:(g + 1) * A_V_ROWS, :] = a_ones

    b_scale = (B_NOPE_DIM + B_ROPE_DIM) ** -0.5 * LOG2E
    cqn = _rms(proj(_C_BQ, _C_BKV), bqn_ref[...]).astype(BF16)
    bkv = proj(_C_BKV, _C_CQ)
    ckvn = _rms(bkv[:, 0:B_KV_RANK], bkvn_ref[...]).astype(BF16)
    mixed = bkv[:, B_KV_RANK:B_KV_RANK + LANES]
    g_low = mixed
    kr = rope_b(jnp.where(lane >= B_NOPE_DIM, mixed, 0.0))
    for pair in range(B_HEADS // 2):
        sl2 = slice(2 * pair * B_HEAD_PAD, 2 * (pair + 1) * B_HEAD_PAD)
        q2 = _dot(cqn, wuq_ref[:, sl2])
        k2 = _dot(ckvn, wkn_ref[:, sl2])
        for i in range(2):
            sl = slice((2 * pair + i) * B_HEAD_PAD, (2 * pair + i + 1) * B_HEAD_PAD)
            part = slice(i * B_HEAD_PAD, (i + 1) * B_HEAD_PAD)
            qb_ref[sl, :] = (rope_b(q2[:, part]) * b_scale).T.astype(BF16)
            kb_ref[:, sl] = (k2[:, part] + kr).astype(BF16)
    vb = _dot(ckvn, wvb_ref[...])
    tk = vb_ref.shape[2]
    per = LANES // B_V_DIM
    ones_rows = jnp.where(lax.broadcasted_iota(jnp.int32, (B_V_ROWS - B_V_DIM, tk), 0) == 0, 1.0, 0.0).astype(BF16)
    for cc in range(vb_ref.shape[0]):
        for c in range(vb.shape[1] // LANES):
            v_t = vb[cc * tk:(cc + 1) * tk, c * LANES:(c + 1) * LANES].T.astype(BF16)
            for i in range(per):
                r0 = (c * per + i) * B_V_ROWS
                vb_ref[cc, r0:r0 + B_V_DIM, :] = v_t[i * B_V_DIM:(i + 1) * B_V_DIM]
                vb_ref[cc, r0 + B_V_DIM:r0 + B_V_ROWS, :] = ones_rows

    cq_ref[...] = proj(_C_CQ, _C_CK) * (C_KEY_DIM ** -0.5)
    ck_ref[...] = proj(_C_CK, _C_CV)
    cv_ref[...] = proj(_C_CV, _C_CR).astype(BF16)
    cr_ref[...] = proj(_C_CR, _C_GATE).astype(BF16)
    z = _dot_split(g_low, wg_ref[...]) + bg_ref[...]
    g_ref[...] = (jnp.minimum(z, 0.0) - jnp.log1p(jnp.exp(-jnp.abs(z)))) * (1.0 / C_GATE_TAU)

    gate_ref[...] = proj(_C_GATE, _C_GATE + 3 * d_model).astype(BF16)

def _mod_index(b, t):
    return (jnp.where(t == 0, 0, b + 1), 0, 0)


def _inproj(Xc, Xl, mod, nw, layer, w_pieces, rope, bqn, bkvn, wuq, wkn, wvb, wg, bg):
    spec_c, spec_l, T = _stream_specs(Xc, Xl)
    B, _, D = Xl.shape
    NT = T // TM
    n_in = sum(w.shape[2] for w in w_pieces)
    row = lambda n: pl.BlockSpec((None, TM, n), lambda b, t: (b, t, 0))
    widths = [(A_HEADS * A_HEAD_DIM, BF16), (A_KV_HEADS * A_HEAD_DIM, BF16), (A_KV_HEADS * A_HEAD_DIM, BF16),
              (B_HEADS * B_HEAD_PAD, BF16), (B_HEADS * B_HEAD_PAD, BF16), (B_HEADS * B_V_DIM, BF16),
              (C_HEADS * C_KEY_DIM, F32), (C_HEADS * C_KEY_DIM, F32), (C_HEADS * C_VAL_DIM, BF16),
              (C_HEADS * C_VAL_DIM, BF16), (2 * C_HEADS * C_KEY_DIM, F32), (3 * D, BF16)]
    out_bytes = sum(TM * n * jnp.dtype(dt).itemsize for n, dt in widths)
    vmem = D * n_in * 2 + 4 * out_bytes + 6 * TM * D * 4 + (8 << 20)
    out_specs = [row(n) for n, _ in widths]
    out_shape = [jax.ShapeDtypeStruct((B, T, n), dt) for n, dt in widths]
    qw, vw = B_HEADS * B_HEAD_PAD, B_HEADS * B_V_ROWS
    out_specs[3] = pl.BlockSpec((None, qw, TM), lambda b, t: (b, 0, t))
    out_shape[3] = jax.ShapeDtypeStruct((B, qw, T), BF16)
    out_specs[0] = pl.BlockSpec((None, widths[0][0], TM), lambda b, t: (b, 0, t))
    out_shape[0] = jax.ShapeDtypeStruct((B, widths[0][0], T), BF16)
    out_specs[2] = pl.BlockSpec((None, A_KV_HEADS * A_V_ROWS, TM), lambda b, t: (b, 0, t))
    out_shape[2] = jax.ShapeDtypeStruct((B, A_KV_HEADS * A_V_ROWS, T), BF16)
    out_specs[5] = pl.BlockSpec((None, TM // MLA_TK, vw, MLA_TK), lambda b, t: (b, t, 0, 0))
    out_shape[5] = jax.ShapeDtypeStruct((B, T // MLA_TK, vw, MLA_TK), BF16)
    return pl.pallas_call(
        functools.partial(_inproj_kernel, d_model=D),
        grid=(B, NT),
        in_specs=[
            spec_c, spec_l,
            pl.BlockSpec((None, 6, D), _mod_index),
            _const_spec((1, D)),
            *[_layer_spec(w, layer) for w in w_pieces],
            pl.BlockSpec((4, TM, LANES), lambda b, t: (0, t, 0)),
            _const_spec(bqn.shape), _const_spec(bkvn.shape), _layer_spec(wuq, layer),
            _layer_spec(wkn, layer), _layer_spec(wvb, layer), _layer_spec(wg, layer), _const_spec(bg.shape),
        ],
        out_specs=out_specs,
        out_shape=out_shape,
        compiler_params=pltpu.CompilerParams(
            dimension_semantics=("parallel", "parallel"), vmem_limit_bytes=_vmem_limit(vmem)),
    )(Xc, Xl, mod, nw, *w_pieces, rope, bqn, bkvn, wuq, wkn, wvb, wg, bg)


def _attn_a_kernel(q_ref, kp_ref, km_ref, kn_ref, kc_ref, vp_ref, vm_ref, vn_ref, vc_ref, sink_ref,
                   o_ref, *, seq):
    t = pl.program_id(1)
    kloc = jnp.concatenate([kp_ref[...], km_ref[...], kn_ref[...]], axis=0)
    vloc = jnp.concatenate([vp_ref[...], vm_ref[...], vn_ref[...]], axis=1)
    kctx, vctx = kc_ref[...], vc_ref[...]
    nloc = kloc.shape[0]
    r = lax.broadcasted_iota(jnp.int32, (nloc, TM), 0)
    c = lax.broadcasted_iota(jnp.int32, (nloc, TM), 1)
    rel = r - A_WINDOW - c
    kpos = (t - 1) * TM - A_WINDOW + r
    valid = (jnp.abs(rel) <= A_WINDOW) & (kpos >= 0) & (kpos < seq) & (t >= 1)
    group = A_HEADS // A_KV_HEADS
    per = LANES // A_HEAD_DIM
    q_zero = jnp.zeros((A_HEAD_DIM, TM), BF16)

    def scores(hd):
        q_t = q_ref[hd * A_HEAD_DIM:(hd + 1) * A_HEAD_DIM, :]
        g = hd // group
        qm = jnp.concatenate([q_t if i == g else q_zero for i in range(A_KV_HEADS)], axis=0)
        return _dot(kloc, qm), _dot(kctx, qm)

    queue = [scores(hd) for hd in range(A_AHEAD)]
    parts = []
    for hd in range(A_HEADS):
        if hd + A_AHEAD < A_HEADS:
            queue.append(scores(hd + A_AHEAD))
        cur = queue.pop(0)
        s_loc = jnp.where(valid, cur[0], _NEG_INF)
        s_ctx = cur[1]
        sink = sink_ref[hd:hd + 1, 0:1] * LOG2E
        m = jnp.maximum(jnp.maximum(jnp.max(s_loc, axis=0, keepdims=True),
                                    jnp.max(s_ctx, axis=0, keepdims=True)), sink)
        p_loc = jnp.exp2((s_loc - m).astype(BF16))
        p_ctx = jnp.exp2((s_ctx - m).astype(BF16))
        g = hd // group
        vs = slice(g * A_V_ROWS, (g + 1) * A_V_ROWS)
        o_t = _dot(vloc[vs], p_loc) + _dot(vctx[vs], p_ctx)
        denom = o_t[A_HEAD_DIM:A_HEAD_DIM + 1] + jnp.exp2(sink - m)
        parts.append(o_t[0:A_HEAD_DIM] / denom)
        if len(parts) == per:
            c = hd // per
            o_ref[:, c * LANES:(c + 1) * LANES] = jnp.concatenate(parts, axis=0).T.astype(BF16)
            parts = []


def _attn_a(qa, ka, va, sink, lc):
    B, qw, T = qa.shape
    NT = T // TM
    seq = T - lc
    kvw = ka.shape[2]
    vrows = va.shape[1]
    per_tm = TM // A_WINDOW
    last = T // A_WINDOW - 1
    prev_i = lambda t: jnp.maximum(t * per_tm - 1, 0)
    next_i = lambda t: jnp.minimum((t + 1) * per_tm, last)
    kprev = pl.BlockSpec((None, A_WINDOW, kvw), lambda b, t: (b, prev_i(t), 0))
    kmain = pl.BlockSpec((None, TM, kvw), lambda b, t: (b, t, 0))
    knext = pl.BlockSpec((None, A_WINDOW, kvw), lambda b, t: (b, next_i(t), 0))
    kctx = pl.BlockSpec((None, lc, kvw), lambda b, t: (b, 0, 0))
    vprev = pl.BlockSpec((None, vrows, A_WINDOW), lambda b, t: (b, 0, prev_i(t)))
    vmain = pl.BlockSpec((None, vrows, TM), lambda b, t: (b, 0, t))
    vnext = pl.BlockSpec((None, vrows, A_WINDOW), lambda b, t: (b, 0, next_i(t)))
    vctx = pl.BlockSpec((None, vrows, lc), lambda b, t: (b, 0, 0))
    return pl.pallas_call(
        functools.partial(_attn_a_kernel, seq=seq),
        grid=(B, NT),
        in_specs=[pl.BlockSpec((None, qw, TM), lambda b, t: (b, 0, t)),
                  kprev, kmain, knext, kctx, vprev, vmain, vnext, vctx,
                  pl.BlockSpec(sink.shape, lambda b, t: (0, 0))],
        out_specs=pl.BlockSpec((None, TM, qw), lambda b, t: (b, t, 0)),
        out_shape=jax.ShapeDtypeStruct((B, T, qw), BF16),
        compiler_params=pltpu.CompilerParams(dimension_semantics=("parallel", "parallel")),
    )(qa, ka, ka, ka, ka, va, va, va, va, sink)


def _mla_kernel(q_ref, k_ref, v_ref, o_ref, m_ref, acc_ref, sa_ref, sb_ref, *, lc, tk):
    t = pl.program_id(1)
    total = k_ref.shape[0]
    nchunks = jnp.where(t == 0, lc // tk, total // tk)
    m_ref[...] = jnp.full(m_ref.shape, _NEG_INF, F32)
    acc_ref[...] = jnp.zeros(acc_ref.shape, F32)

    def stage(dst_ref, j, hd):
        qs = slice(hd * B_HEAD_PAD, (hd + 1) * B_HEAD_PAD)
        s = _dot(k_ref[pl.ds(pl.multiple_of(j * tk, tk), tk), qs], q_ref[qs, :])
        dst_ref[hd, 0:tk, :] = s
        dst_ref[hd, tk:tk + SUBLANES, :] = jnp.broadcast_to(jnp.max(s, axis=0, keepdims=True), (SUBLANES, TM))

    for hd in range(B_HEADS):
        stage(sa_ref, 0, hd)

    def step(j, cur_ref, nxt_ref, last=False):
        for hd in range(B_HEADS):
            vs = slice(hd * B_V_ROWS, (hd + 1) * B_V_ROWS)
            if not last:
                stage(nxt_ref, j + 1, hd)
            m_prev = m_ref[hd]
            m_new = jnp.maximum(m_prev, cur_ref[hd, tk:tk + SUBLANES, :])
            alpha = jnp.exp2(m_prev - m_new)
            p = jnp.exp2((cur_ref[hd, 0:tk, :] - m_new[0:1]).astype(BF16))
            m_ref[hd] = m_new
            acc_ref[vs, :] = acc_ref[vs, :] * alpha[0:1] + _dot(v_ref[j, vs, :], p)

    bufs = (sa_ref, sb_ref)

    def body(g, carry):
        for i in range(MLA_UNROLL):
            step(MLA_UNROLL * g + i, bufs[i % 2], bufs[(i + 1) % 2])
        return carry

    ngroups = (nchunks - 1) // MLA_UNROLL
    lax.fori_loop(0, ngroups, body, 0)
    tail = (total // tk - 1) % MLA_UNROLL + 1
    assert tail == (lc // tk - 1) % MLA_UNROLL + 1
    for i in range(tail):
        step(MLA_UNROLL * ngroups + i, bufs[i % 2], bufs[(i + 1) % 2], last=(i == tail - 1))
    per = LANES // B_V_DIM
    for c in range(B_HEADS // per):
        o_t = jnp.concatenate(
            [acc_ref[hd * B_V_ROWS:hd * B_V_ROWS + B_V_DIM, :]
             / acc_ref[hd * B_V_ROWS + B_V_DIM:hd * B_V_ROWS + B_V_DIM + 1, :]
             for hd in range(c * per, (c + 1) * per)], axis=0)
        o_ref[:, c * LANES:(c + 1) * LANES] = o_t.T.astype(BF16)


def _mla(qb, kb, vb, lc):
    B, qw, T = qb.shape
    NT = T // TM
    _, nck, vw, tk = vb.shape
    ow = B_HEADS * B_V_DIM
    vmem = (T * (qw + vw) * 2 + 4 * TM * qw * 2 + (B_HEADS * SUBLANES * TM + 3 * TM * vw) * 4
            + 2 * B_HEADS * (tk + SUBLANES) * TM * 4 + (16 << 20))
    return pl.pallas_call(
        functools.partial(_mla_kernel, lc=lc, tk=tk),
        grid=(B, NT),
        in_specs=[pl.BlockSpec((None, qw, TM), lambda b, t: (b, 0, t)),
                  pl.BlockSpec((None, T, qw), lambda b, t: (b, 0, 0), pipeline_mode=pl.Buffered(1)),
                  pl.BlockSpec((None, nck, vw, tk), lambda b, t: (b, 0, 0, 0), pipeline_mode=pl.Buffered(1))],
        out_specs=pl.BlockSpec((None, TM, ow), lambda b, t: (b, t, 0)),
        out_shape=jax.ShapeDtypeStruct((B, T, ow), BF16),
        scratch_shapes=[pltpu.VMEM((B_HEADS, SUBLANES, TM), F32),
                        pltpu.VMEM((vw, TM), F32),
                        pltpu.VMEM((B_HEADS, tk + SUBLANES, TM), F32),
                        pltpu.VMEM((B_HEADS, tk + SUBLANES, TM), F32)],
        compiler_params=pltpu.CompilerParams(
            dimension_semantics=("parallel", "arbitrary"), vmem_limit_bytes=_vmem_limit(vmem)),
    )(qb, kb, vb)


def _gla_prep(q, k, b, reverse):
    C = q.shape[0]
    dk2 = q.shape[1]
    nsub = C // GLA_SUB
    b_end = b[0:1] if reverse else b[C - 1:C]
    row = lax.broadcasted_iota(jnp.int32, (C, dk2), 0)
    lane = lax.broadcasted_iota(jnp.int32, (C, dk2), 1)

    qe = (q * jnp.exp(b)).astype(BF16)
    kd_t = (k * jnp.exp(b_end - b)).T.astype(BF16)
    dec = jnp.exp(jnp.broadcast_to(b_end, (dk2, dk2)).T)
    dec2 = jnp.concatenate([dec] * (2 * C_VAL_DIM // dk2), axis=1)

    refs = [blk * GLA_SUB + (GLA_SUB - 1 if reverse else 0) for blk in range(nsub)]
    b_ref = jnp.concatenate([jnp.broadcast_to(b[n:n + 1], (GLA_SUB, dk2)) for n in refs], axis=0)
    q_sc = q * jnp.exp(b - b_ref)
    k_parts = []
    for blk, n in enumerate(refs):
        in_range = (row >= blk * GLA_SUB) if reverse else (row < (blk + 1) * GLA_SUB)
        k_parts.append(jnp.where(in_range, k * jnp.exp(b[n:n + 1] - b), 0.0))
    k_big = jnp.concatenate(k_parts, axis=1).astype(BF16)
    q_halves = []
    for hf in range(2):
        in_half = (lane >= hf * C_KEY_DIM) & (lane < (hf + 1) * C_KEY_DIM)
        q_halves.append(jnp.concatenate(
            [jnp.where(in_half & (row >= blk * GLA_SUB) & (row < (blk + 1) * GLA_SUB), q_sc, 0.0)
             for blk in range(nsub)], axis=1))
    q_big = jnp.concatenate(q_halves, axis=0).astype(BF16)
    return qe, kd_t, dec2, q_big, k_big


def _gla_kernel(qf_ref, kf_ref, vf_ref, gf_ref, qr_ref, kr_ref, vr_ref, gr_ref, of_ref, or_ref,
                sf_ref, sr_ref):
    t = pl.program_id(1)

    @pl.when(t == 0)
    def _():
        sf_ref[...] = jnp.zeros(sf_ref.shape, F32)
        sr_ref[...] = jnp.zeros(sr_ref.shape, F32)

    C = C_CHUNK
    nchunk = TM // C
    npair = C_HEADS // 2
    kw, vw = 2 * C_KEY_DIM, 2 * C_VAL_DIM
    dirs = ((qf_ref, kf_ref, vf_ref, gf_ref, of_ref, sf_ref, False),
            (qr_ref, kr_ref, vr_ref, gr_ref, or_ref, sr_ref, True))

    ii = lax.broadcasted_iota(jnp.int32, (TM, TM), 0)
    jj = lax.broadcasted_iota(jnp.int32, (TM, TM), 1)
    same_chunk = (ii // C) == (jj // C)
    b_all = []
    for (_, _, _, g_ref, _, _, reverse) in dirs:
        tri = jnp.where(same_chunk & ((jj >= ii) if reverse else (jj <= ii)), 1.0, 0.0).astype(BF16)
        g_hi, g_mid, g_lo = _split_bf16(g_ref[...], 3)
        b_all.append(_dot(tri, g_hi) + (_dot(tri, g_mid) + _dot(tri, g_lo)))

    units = []
    for d, (q_ref, k_ref, v_ref, _, o_ref, _, reverse) in enumerate(dirs):
        for p in range(npair):
            ks, vsl = slice(p * kw, (p + 1) * kw), slice(p * vw, (p + 1) * vw)
            for c in range(nchunk):
                rows = slice(c * C, (c + 1) * C)
                prep = _gla_prep(q_ref[rows, ks], k_ref[rows, ks], b_all[d][rows, ks], reverse)
                units.append((d, p, c, rows, vsl, v_ref, o_ref, reverse, prep))

    srow = lax.broadcasted_iota(jnp.int32, (kw, vw), 0)
    scol = lax.broadcasted_iota(jnp.int32, (kw, vw), 1)
    on_diag = (srow < C_KEY_DIM) == (scol < C_VAL_DIM)
    qi = lax.broadcasted_iota(jnp.int32, (2 * C, C), 0) % C
    kj = lax.broadcasted_iota(jnp.int32, (2 * C, C), 1)
    kvs, atts = [], []
    for (d, p, c, rows, vsl, v_ref, o_ref, reverse, prep) in units:
        _, kd_t, _, q_big, k_big = prep
        kvs.append(jnp.where(on_diag, _dot(kd_t, v_ref[rows, vsl]), 0.0))
        keep = (kj >= qi) if reverse else (kj <= qi)
        atts.append(jnp.where(keep, _dot_nt(q_big, k_big), 0.0).astype(BF16))

    intra = {}
    for u, (d, p, c, rows, vsl, v_ref, o_ref, reverse, prep) in enumerate(units):
        pv = _dot(atts[u], v_ref[rows, vsl])
        intra[(d, p, c)] = (u, jnp.concatenate([pv[0:C, 0:C_VAL_DIM], pv[C:2 * C, C_VAL_DIM:]], axis=1))

    states = {(d, p): dirs[d][5][p] for d in range(len(dirs)) for p in range(npair)}
    for step in range(nchunk):
        for d in range(len(dirs)):
            reverse = dirs[d][6]
            c = nchunk - 1 - step if reverse else step
            for p in range(npair):
                u, o_intra = intra[(d, p, c)]
                _, _, _, rows, vsl, _, o_ref, _, prep = units[u]
                qe, _, dec2, _, _ = prep
                o_ref[rows, vsl] = (o_intra + _dot(qe, states[(d, p)].astype(BF16))).astype(o_ref.dtype)
                states[(d, p)] = dec2 * states[(d, p)] + kvs[u]
    for d in range(len(dirs)):
        for p in range(npair):
            dirs[d][5][p] = states[(d, p)]


def _gla(cq, ck, cv, g):
    B, T, kw = cq.shape
    NT = T // TM
    vw = cv.shape[2]

    def fwd(n, col=0):
        return pl.BlockSpec((None, TM, n), lambda b, t: (b, t, col))

    def rev(n, col=0):
        return pl.BlockSpec((None, TM, n), lambda b, t: (b, jnp.where(t == 0, 0, NT - t), col))

    return pl.pallas_call(
        _gla_kernel,
        grid=(B, NT),
        in_specs=[fwd(kw), fwd(kw), fwd(vw), fwd(kw, 0), rev(kw), rev(kw), rev(vw), rev(kw, 1)],
        out_specs=[fwd(vw), rev(vw)],
        out_shape=[jax.ShapeDtypeStruct((B, T, vw), BF16)] * 2,
        scratch_shapes=[pltpu.VMEM((C_HEADS // 2, 2 * C_KEY_DIM, 2 * C_VAL_DIM), F32)] * 2,
        compiler_params=pltpu.CompilerParams(dimension_semantics=("parallel", "arbitrary")),
    )(cq, ck, cv, g, cq, ck, cv, g)


def _merge_kernel(xc_ref, xl_ref, mod_ref, ya_ref, yb_ref, of_ref, or_ref, cr_ref, gate_ref, hn_ref,
                  wa_ref, wb_ref, wc_ref, wo_ref, o_ref):
    d = xl_ref.shape[1]
    o = of_ref[...].astype(F32) + or_ref[...].astype(F32)
    r = cr_ref[...].astype(F32)
    parts = []
    for hd in range(C_HEADS):
        sl = slice(hd * C_VAL_DIM, (hd + 1) * C_VAL_DIM)
        parts.append(_rms(o[:, sl], hn_ref[:, sl]))
    yc = (jnp.concatenate(parts, axis=1) * (r * jax.nn.sigmoid(r))).astype(BF16)

    def gate(i):
        return jax.nn.sigmoid(gate_ref[:, i * d:(i + 1) * d].astype(F32))

    m = (gate(0) * _dot(ya_ref[...], wa_ref[...]) + gate(1) * _dot(yb_ref[...], wb_ref[...])
         + gate(2) * _dot(yc, wc_ref[...]))
    y = _dot(m.astype(BF16), wo_ref[...])
    o_ref[...] = _stream_tile(xc_ref, xl_ref) + mod_ref[2:3, :] * y


def _merge(Xc, Xl, mod, ya, yb, of, orv, cr, gates, hn, layer, wa, wb, wc, wo):
    spec_c, spec_l, T = _stream_specs(Xc, Xl)
    B, _, D = Xl.shape
    NT = T // TM
    row = lambda n: pl.BlockSpec((None, TM, n), lambda b, t: (b, t, 0))
    wbytes = sum(w.size // w.shape[0] for w in (wa, wb, wc, wo)) * 2
    vmem = wbytes + 2 * TM * (2 * D + 3 * D + 4 * ya.shape[2]) * 4 + (16 << 20)
    return pl.pallas_call(
        _merge_kernel,
        grid=(B, NT),
        in_specs=[spec_c, spec_l, pl.BlockSpec((None, 6, D), _mod_index),
                  row(ya.shape[2]), row(yb.shape[2]), row(of.shape[2]), row(orv.shape[2]),
                  row(cr.shape[2]), row(gates.shape[2]), _const_spec(hn.shape),
                  _layer_spec(wa, layer), _layer_spec(wb, layer), _layer_spec(wc, layer), _layer_spec(wo, layer)],
        out_specs=row(D),
        out_shape=jax.ShapeDtypeStruct((B, T, D), F32),
        compiler_params=pltpu.CompilerParams(
            dimension_semantics=("parallel", "parallel"), vmem_limit_bytes=_vmem_limit(vmem)),
    )(Xc, Xl, mod, ya, yb, of, orv, cr, gates, hn, wa, wb, wc, wo)


def _ffn_kernel(x_ref, xp_ref, xn_ref, mod_ref, nw_ref, wup_ref, cw_ref, cb_ref, wdn_ref, fw_ref, o_ref,
                u_ref, act_ref, *, nt, ncb, t0, final):
    t = pl.program_id(1) + t0
    mod = mod_ref[...]
    nw = nw_ref[...]
    halo = xp_ref.shape[0]
    ffn = wdn_ref.shape[0]
    cb = ffn // ncb

    def norm(xv):
        return _rms(xv, nw) * (1.0 + mod[4:5]) + mod[3:4]

    has_prev = (t >= 2)
    has_next = (t >= 1) & (t < nt - 1)
    hb = jnp.concatenate([norm(x_ref[...]).astype(BF16),
                          jnp.concatenate([jnp.where(has_prev, norm(xp_ref[...]), 0.0),
                                           jnp.where(has_next, norm(xn_ref[...]), 0.0)], axis=0).astype(BF16)], axis=0)
    row8 = lax.broadcasted_iota(jnp.int32, (SUBLANES, cb), 0)

    def up(j):
        u_ref[j % 2, 0] = _dot(hb, wup_ref[:, j * cb:(j + 1) * cb])
        u_ref[j % 2, 1] = _dot(hb, wup_ref[:, ffn + j * cb:ffn + (j + 1) * cb])

    def conv(slot, half, lo):
        w = cw_ref[:, lo:lo + cb]
        um = u_ref[slot, half, 0:TM, :]
        prev_row = u_ref[slot, half, TM + halo - 1:TM + halo, :]
        next_row = u_ref[slot, half, TM + halo:TM + halo + 1, :]
        rd = pltpu.roll(um, 1, 0)
        ru = pltpu.roll(um, TM - 1, 0)
        sd = jnp.concatenate([jnp.where(row8 == 0, prev_row, rd[0:SUBLANES]), rd[SUBLANES:]], axis=0)
        su = jnp.concatenate([ru[:TM - SUBLANES], jnp.where(row8 == SUBLANES - 1, next_row, ru[TM - SUBLANES:])],
                             axis=0)
        return sd * w[0:1] + um * w[1:2] + su * w[2:3] + cb_ref[:, lo:lo + cb]

    def act(j):
        gt = conv(j % 2, 0, j * cb)
        val = conv(j % 2, 1, ffn + j * cb)
        act_ref[:, j * cb:(j + 1) * cb] = (gt * jax.nn.sigmoid(gt) * val).astype(BF16)

    up(0)
    for j in range(1, ncb):
        up(j)
        act(j - 1)
    act(ncb - 1)
    y = x_ref[...] + mod[5:6] * _dot(act_ref[...], wdn_ref[...])
    o_ref[...] = _rms(y, fw_ref[...]) if final else y


def _ffn(X, mod, nw, layer, wup, cw, cb, wdn, fw, final):
    B, T, D = X.shape
    NT = T // TM
    t0 = 1 if final else 0
    halo = SUBLANES
    per = TM // halo
    last = T // halo - 1
    ffn = wdn.shape[1]
    cbw = FFN_COLS
    ncb = ffn // cbw
    vmem = 3 * D * ffn * 2 + 4 * (TM + 2 * halo) * cbw * 4 + TM * ffn * 2 + 8 * TM * D * 4 + (16 << 20)
    return pl.pallas_call(
        functools.partial(_ffn_kernel, nt=NT, ncb=ncb, t0=t0, final=final),
        grid=(B, NT - t0),
        in_specs=[pl.BlockSpec((None, TM, D), lambda b, t: (b, t + t0, 0)),
                  pl.BlockSpec((None, halo, D), lambda b, t: (b, jnp.maximum((t + t0) * per - 1, 0), 0)),
                  pl.BlockSpec((None, halo, D), lambda b, t: (b, jnp.minimum((t + t0 + 1) * per, last), 0)),
                  pl.BlockSpec((None, 6, D), lambda b, t: _mod_index(b, t + t0)),
                  _const_spec(nw.shape), _layer_spec(wup, layer), _const_spec(cw.shape), _const_spec(cb.shape),
                  _layer_spec(wdn, layer), _const_spec(fw.shape)],
        out_specs=pl.BlockSpec((None, TM, D), lambda b, t: (b, t, 0)),
        out_shape=jax.ShapeDtypeStruct((B, T - t0 * TM, D), F32),
        scratch_shapes=[pltpu.VMEM((2, 2, TM + 2 * halo, cbw), F32), pltpu.VMEM((TM, ffn), BF16)],
        compiler_params=pltpu.CompilerParams(
            dimension_semantics=("parallel", "parallel"), vmem_limit_bytes=_vmem_limit(vmem)),
    )(X, X, X, mod, nw, wup, cw, cb, wdn, fw)


def _rope_tables(seq, lc):
    rows = seq // GRID_W
    row = jnp.broadcast_to(jnp.arange(rows)[:, None], (rows, GRID_W)).reshape(-1).astype(F32)
    col = jnp.broadcast_to(jnp.arange(GRID_W)[None, :], (rows, GRID_W)).reshape(-1).astype(F32)

    def cs(rot_dim):
        n_freq = rot_dim // 4
        inv = ROPE_BASE ** (-jnp.arange(n_freq, dtype=F32) / n_freq)
        ang = jnp.concatenate([row[:, None] * inv, col[:, None] * inv], axis=-1)
        return jnp.cos(ang), jnp.sin(ang)

    ca, sa = cs(A_HEAD_DIM)
    cb, sb = cs(B_ROPE_DIM)
    reps = LANES // A_HEAD_DIM
    cos_a = jnp.tile(jnp.concatenate([ca, ca], axis=1), (1, reps))
    sin_a = jnp.tile(jnp.concatenate([-sa, sa], axis=1), (1, reps))
    one = jnp.ones((seq, B_NOPE_DIM), F32)
    pad = LANES - B_NOPE_DIM - B_ROPE_DIM
    cos_b = jnp.concatenate([one, cb, cb, jnp.ones((seq, pad), F32)], axis=1)
    sin_b = jnp.concatenate([0 * one, -sb, sb, jnp.zeros((seq, pad), F32)], axis=1)
    ident = jnp.stack([jnp.ones((lc, LANES), F32), jnp.zeros((lc, LANES), F32)] * 2)
    return jnp.concatenate([ident, jnp.stack([cos_a, sin_a, cos_b, sin_b])], axis=1)


def _prep_w_in(w_in):
    o_kv = _C_BKV
    o_kr = o_kv + B_KV_RANK
    o_cq = o_kr + B_ROPE_DIM
    o_cg = o_cq + (_C_GATE - _C_CQ)
    o_gate = o_cg + 2 * C_GATE_RANK
    zeros = lambda n: jnp.zeros(w_in.shape[:2] + (n,), w_in.dtype)
    gap = B_NOPE_DIM - 2 * C_GATE_RANK
    wkv = jnp.concatenate([w_in[:, :, o_kv:o_kr], w_in[:, :, o_cg:o_gate], zeros(gap), w_in[:, :, o_kr:o_cq],
                           zeros(LANES - B_NOPE_DIM - B_ROPE_DIM)], axis=-1)
    return tuple(p.astype(BF16) for p in (w_in[:, :, :o_kv], wkv, w_in[:, :, o_cq:o_cg], w_in[:, :, o_gate:]))


def kernel(x, c, ctx, c_ctx, w_mod, b_mod, norm_mix, norm_ffn, w_in, a_sink, b_q_norm, b_kv_norm, b_w_uq, b_w_ukv, c_w_gate, c_b_gate, c_head_norm, w_br_a, w_br_b, w_br_c, w_out, w_up, conv_w, conv_b, w_down, final_norm):
    B, S, D = x.shape
    lc = ctx.shape[1]
    L = w_mod.shape[0]
    assert lc == TM and S % TM == 0 and S % GRID_W == 0 and B + 1 <= SUBLANES

    Xc, Xl = ctx, x
    cvec = jnp.zeros((SUBLANES, D), F32).at[0].set(c_ctx).at[1:B + 1].set(c)
    mods = _modulation(cvec, w_mod, b_mod).reshape(L, SUBLANES, 6, D)

    rope = _rope_tables(S, lc)
    w_in_r = _prep_w_in(w_in)
    sink = jnp.broadcast_to(a_sink[:, :, None], (L, A_HEADS, LANES)).astype(F32)
    wa, wb, wc, wo = w_br_a.astype(BF16), w_br_b.astype(BF16), w_br_c.astype(BF16), w_out.astype(BF16)
    qk = B_NOPE_DIM + B_ROPE_DIM
    wuq = jnp.pad(b_w_uq.reshape(L, B_Q_RANK, B_HEADS, qk),
                  ((0, 0), (0, 0), (0, 0), (0, B_HEAD_PAD - qk))).reshape(L, B_Q_RANK, -1).astype(BF16)
    ukv = b_w_ukv.reshape(L, B_KV_RANK, B_HEADS, B_NOPE_DIM + B_V_DIM)
    wkn = jnp.pad(ukv[..., :B_NOPE_DIM],
                  ((0, 0), (0, 0), (0, 0), (0, B_HEAD_PAD - B_NOPE_DIM))).reshape(L, B_KV_RANK, -1).astype(BF16)
    wvb = ukv[..., B_NOPE_DIM:].reshape(L, B_KV_RANK, -1).astype(BF16)
    nqk = C_HEADS * C_KEY_DIM
    wg = jnp.zeros((L, LANES, 2 * nqk), F32)
    wg = wg.at[:, 0:C_GATE_RANK, 0:nqk].set(c_w_gate[:, 0]).at[:, C_GATE_RANK:2 * C_GATE_RANK, nqk:].set(c_w_gate[:, 1])
    bg = c_b_gate.reshape(L, 1, 2 * nqk)
    wup, wdn = w_up.astype(BF16), w_down.astype(BF16)

    for l in range(L):
        mod = mods[l]
        (qa, ka, va, qb, kb, vb, cq, ck, cv, cr, g, gates) = _inproj(
            Xc, Xl, mod, norm_mix[l][None], l, w_in_r, rope, b_q_norm[l][None], b_kv_norm[l][None],
            wuq, wkn, wvb, wg, bg[l])
        ya = _attn_a(qa, ka, va, sink[l], lc)
        yb = _mla(qb, kb, vb, lc)
        of, orv = _gla(cq, ck, cv, g)
        X = _merge(Xc, Xl, mod, ya, yb, of, orv, cr, gates, c_head_norm[l][None], l, wa, wb, wc, wo)
        X = _ffn(X, mod, norm_ffn[l][None], l, wup, conv_w[l], conv_b[l][None], wdn, final_norm[None],
                 final=(l == L - 1))
        Xc = Xl = X
    return X
```

```python
import functools

import jax
import jax.numpy as jnp
from jax import lax
from jax.experimental import pallas as pl
from jax.experimental.pallas import tpu as pltpu

GRID_W = 64
EPS = 1e-6
ROPE_BASE = 10000.0
A_HEADS, A_KV_HEADS, A_HEAD_DIM, A_WINDOW = 8, 2, 64, 128
B_HEADS, B_Q_RANK, B_KV_RANK, B_NOPE_DIM, B_ROPE_DIM, B_V_DIM = 8, 256, 128, 64, 32, 64
C_HEADS, C_KEY_DIM, C_VAL_DIM, C_GATE_RANK, C_GATE_TAU, C_CHUNK = 4, 64, 128, 16, 16.0, 64
CONV_W = 3

LANES = 128
SUBLANES = 8
VMEM_BYTES = 64 * 1024 * 1024

TM = 256
GLA_SUB = 16
A_AHEAD = 4
MLA_TK = 256
MLA_UNROLL = 32
MOD_COLS = 3072
FFN_COLS = 256
F32 = jnp.float32
BF16 = jnp.bfloat16
_HI = lax.Precision.HIGHEST
_NEG_INF = float("-inf")


def _dot(a, b):
    return jnp.dot(a, b, preferred_element_type=F32)


def _dot_nt(a, b):
    return lax.dot_general(a, b, (((1,), (1,)), ((), ())), preferred_element_type=F32)


def _split_bf16(x, terms):
    parts = []
    for _ in range(terms):
        p = x.astype(BF16)
        parts.append(p)
        x = x - p.astype(F32)
    return parts


def _dot_split(a, b):
    a_hi, a_lo = _split_bf16(a, 2)
    b_hi, b_lo = _split_bf16(b, 2)
    return _dot(a_hi, b_hi) + (_dot(a_lo, b_hi) + _dot(a_hi, b_lo))


def _rms(x, w):
    return x * lax.rsqrt(jnp.mean(x * x, axis=-1, keepdims=True) + EPS) * w


def _swap_lane_groups(x, half):
    lane = lax.broadcasted_iota(jnp.int32, x.shape, 1)
    up = pltpu.roll(x, LANES - half, 1)
    down = pltpu.roll(x, half, 1)
    return jnp.where((lane & half) == 0, up, down)


def _vmem_limit(nbytes):
    return int(min(VMEM_BYTES - (4 << 20), max(nbytes, 32 << 20)))


def _layer_spec(stacked, layer):
    shape = stacked.shape[1:]
    nd = len(shape)
    return pl.BlockSpec((None,) + shape, lambda *_: (layer,) + (0,) * nd, pipeline_mode=pl.Buffered(1))


def _const_spec(shape):
    nd = len(shape)
    return pl.BlockSpec(shape, lambda *_: (0,) * nd, pipeline_mode=pl.Buffered(1))


def _mod_kernel(c_ref, w_ref, b_ref, o_ref, *, nrows):
    cv = c_ref[...]
    sc = cv * jax.nn.sigmoid(cv)
    w = w_ref[...]
    rid = lax.broadcasted_iota(jnp.int32, o_ref.shape, 0)
    out = jnp.zeros(o_ref.shape, F32)
    for r in range(nrows):
        out = jnp.where(rid == r, jnp.sum(w * sc[:, r:r + 1], axis=0, keepdims=True), out)
    o_ref[...] = out + b_ref[...]


def _modulation(cvec, nrows, w_mod, b_mod):
    L, D, N = w_mod.shape
    nb = N // MOD_COLS
    return pl.pallas_call(
        functools.partial(_mod_kernel, nrows=nrows),
        grid=(L, nb),
        in_specs=[
            pl.BlockSpec((D, SUBLANES), lambda l, j: (0, 0)),
            pl.BlockSpec((None, D, MOD_COLS), lambda l, j: (l, 0, j)),
            pl.BlockSpec((None, 1, MOD_COLS), lambda l, j: (l, 0, j)),
        ],
        out_specs=pl.BlockSpec((None, SUBLANES, MOD_COLS), lambda l, j: (l, 0, j)),
        out_shape=jax.ShapeDtypeStruct((L, SUBLANES, N), F32),
        compiler_params=pltpu.CompilerParams(
            dimension_semantics=("parallel", "parallel"),
            vmem_limit_bytes=_vmem_limit(3 * D * MOD_COLS * 4)),
    )(cvec, w_mod, b_mod.reshape(L, 1, N))


_C_AQ = 0
_C_AK = _C_AQ + A_HEADS * A_HEAD_DIM
_C_AV = _C_AK + A_KV_HEADS * A_HEAD_DIM
_C_BQ = _C_AV + A_KV_HEADS * A_HEAD_DIM
_C_BKV = _C_BQ + B_Q_RANK
_C_BKR = _C_BKV + B_KV_RANK
_C_CQ = _C_BKR + LANES
_C_CK = _C_CQ + C_HEADS * C_KEY_DIM
_C_CV = _C_CK + C_HEADS * C_KEY_DIM
_C_CR = _C_CV + C_HEADS * C_VAL_DIM
_C_GATE = _C_CR + C_HEADS * C_VAL_DIM
B_HEAD_PAD = LANES
B_V_ROWS = B_V_DIM + 2 * SUBLANES
A_V_ROWS = A_HEAD_DIM + 2 * SUBLANES
LOG2E = 1.4426950408889634


def _stream_specs(xc, xl):
    off = 0 if xc is xl else 1
    d = xl.shape[2]
    spec_c = pl.BlockSpec((None, TM, d), lambda b, t: (b, 0, 0))
    spec_l = pl.BlockSpec((None, TM, d), lambda b, t: (b, jnp.maximum(t - off, 0), 0))
    return spec_c, spec_l, xl.shape[1] + off * TM


def _stream_tile(xc_ref, xl_ref):
    return jnp.where(pl.program_id(1) == 0, xc_ref[...], xl_ref[...])


def _inproj_kernel(xc_ref, xl_ref, mod_ref, nw_ref, w1_ref, wkv_ref, w2_ref, w3_ref, rope_ref,
                   bqn_ref, bkvn_ref, wuq_ref, wkn_ref,
                   wvb_ref, wg_ref, bg_ref,
                   qa_ref, ka_ref, va_ref, qb_ref, kb_ref, vb_ref, cq_ref, ck_ref, cv_ref, cr_ref,
                   g_ref, gate_ref, *, d_model):
    mod = mod_ref[...]
    h = _rms(_stream_tile(xc_ref, xl_ref), nw_ref[...]) * (1.0 + mod[1:2]) + mod[0:1]
    hb = h.astype(BF16)

    pieces = ((_C_AQ, w1_ref), (_C_BKV, wkv_ref), (_C_CQ, w2_ref), (_C_GATE, w3_ref))

    def proj(lo, hi):
        base, ref = [(b0, r) for b0, r in pieces if b0 <= lo][-1]
        assert hi - base <= ref.shape[1]
        return _dot(hb, ref[:, lo - base:hi - base])

    lane = lax.broadcasted_iota(jnp.int32, (TM, LANES), 1)

    cos_a, sin_a, cos_b, sin_b = rope_ref[0], rope_ref[1], rope_ref[2], rope_ref[3]

    def rope_a(t):
        return t * cos_a + _swap_lane_groups(t, A_HEAD_DIM // 2) * sin_a

    def rope_b(t):
        return t * cos_b + _swap_lane_groups(t, B_ROPE_DIM // 2) * sin_b

    a_scale = A_HEAD_DIM ** -0.5 * LOG2E
    aq = proj(_C_AQ, _C_AK)
    for c in range(A_HEADS * A_HEAD_DIM // LANES):
        qa_ref[c * LANES:(c + 1) * LANES, :] = (rope_a(aq[:, c * LANES:(c + 1) * LANES]) * a_scale).T.astype(BF16)
    akv = proj(_C_AK, _C_BQ)
    kvw = A_KV_HEADS * A_HEAD_DIM
    ka_ref[...] = rope_a(akv[:, 0:kvw]).astype(BF16)
    va_t = akv[:, kvw:2 * kvw].T.astype(BF16)
    a_ones = jnp.where(lax.broadcasted_iota(jnp.int32, (A_V_ROWS - A_HEAD_DIM, va_t.shape[1]), 0) == 0,
                       1.0, 0.0).astype(BF16)
    for g in range(A_KV_HEADS):
        va_ref[g * A_V_ROWS:g * A_V_ROWS + A_HEAD_DIM, :] = va_t[g * A_HEAD_DIM:(g + 1) * A_HEAD_DIM]
        va_ref[g * A_V_ROWS + A_HEAD_DIM:(g + 1) * A_V_ROWS, :] = a_ones

    b_scale = (B_NOPE_DIM + B_ROPE_DIM) ** -0.5 * LOG2E
    cqn = _rms(proj(_C_BQ, _C_BKV), bqn_ref[...]).astype(BF16)
    bkv = proj(_C_BKV, _C_CQ)
    ckvn = _rms(bkv[:, 0:B_KV_RANK], bkvn_ref[...]).astype(BF16)
    mixed = bkv[:, B_KV_RANK:B_KV_RANK + LANES]
    g_low = mixed
    kr = rope_b(jnp.where(lane >= B_NOPE_DIM, mixed, 0.0))
    for pair in range(B_HEADS // 2):
        sl2 = slice(2 * pair * B_HEAD_PAD, 2 * (pair + 1) * B_HEAD_PAD)
        q2 = _dot(cqn, wuq_ref[:, sl2])
        k2 = _dot(ckvn, wkn_ref[:, sl2])
        for i in range(2):
            sl = slice((2 * pair + i) * B_HEAD_PAD, (2 * pair + i + 1) * B_HEAD_PAD)
            part = slice(i * B_HEAD_PAD, (i + 1) * B_HEAD_PAD)
            qb_ref[sl, :] = (rope_b(q2[:, part]) * b_scale).T.astype(BF16)
            kb_ref[:, sl] = (k2[:, part] + kr).astype(BF16)
    vb = _dot(ckvn, wvb_ref[...])
    tk = vb_ref.shape[2]
    per = LANES // B_V_DIM
    ones_rows = jnp.where(lax.broadcasted_iota(jnp.int32, (B_V_ROWS - B_V_DIM, tk), 0) == 0, 1.0, 0.0).astype(BF16)
    for cc in range(vb_ref.shape[0]):
        for c in range(vb.shape[1] // LANES):
            v_t = vb[cc * tk:(cc + 1) * tk, c * LANES:(c + 1) * LANES].T.astype(BF16)
            for i in range(per):
                r0 = (c * per + i) * B_V_ROWS
                vb_ref[cc, r0:r0 + B_V_DIM, :] = v_t[i * B_V_DIM:(i + 1) * B_V_DIM]
                vb_ref[cc, r0 + B_V_DIM:r0 + B_V_ROWS, :] = ones_rows

    cq_ref[...] = proj(_C_CQ, _C_CK) * (C_KEY_DIM ** -0.5)
    ck_ref[...] = proj(_C_CK, _C_CV)
    cv_ref[...] = proj(_C_CV, _C_CR).astype(BF16)
    cr_ref[...] = proj(_C_CR, _C_GATE).astype(BF16)
    z = _dot_split(g_low, wg_ref[...]) + bg_ref[...]
    g_ref[...] = (jnp.minimum(z, 0.0) - jnp.log1p(jnp.exp(-jnp.abs(z)))) * (1.0 / C_GATE_TAU)

    gate_ref[...] = proj(_C_GATE, _C_GATE + 3 * d_model).astype(BF16)

def _mod_index(b, t):
    return (jnp.where(t == 0, 0, b + 1), 0, 0)


def _inproj(Xc, Xl, mod, nw, layer, w_pieces, rope, bqn, bkvn, wuq, wkn, wvb, wg, bg):
    spec_c, spec_l, T = _stream_specs(Xc, Xl)
    B, _, D = Xl.shape
    NT = T // TM
    n_in = sum(w.shape[2] for w in w_pieces)
    row = lambda n: pl.BlockSpec((None, TM, n), lambda b, t: (b, t, 0))
    widths = [(A_HEADS * A_HEAD_DIM, BF16), (A_KV_HEADS * A_HEAD_DIM, BF16), (A_KV_HEADS * A_HEAD_DIM, BF16),
              (B_HEADS * B_HEAD_PAD, BF16), (B_HEADS * B_HEAD_PAD, BF16), (B_HEADS * B_V_DIM, BF16),
              (C_HEADS * C_KEY_DIM, F32), (C_HEADS * C_KEY_DIM, F32), (C_HEADS * C_VAL_DIM, BF16),
              (C_HEADS * C_VAL_DIM, BF16), (2 * C_HEADS * C_KEY_DIM, F32), (3 * D, BF16)]
    out_bytes = sum(TM * n * jnp.dtype(dt).itemsize for n, dt in widths)
    vmem = D * n_in * 2 + 4 * out_bytes + 6 * TM * D * 4 + (8 << 20)
    out_specs = [row(n) for n, _ in widths]
    out_shape = [jax.ShapeDtypeStruct((B, T, n), dt) for n, dt in widths]
    qw, vw = B_HEADS * B_HEAD_PAD, B_HEADS * B_V_ROWS
    out_specs[3] = pl.BlockSpec((None, qw, TM), lambda b, t: (b, 0, t))
    out_shape[3] = jax.ShapeDtypeStruct((B, qw, T), BF16)
    out_specs[0] = pl.BlockSpec((None, widths[0][0], TM), lambda b, t: (b, 0, t))
    out_shape[0] = jax.ShapeDtypeStruct((B, widths[0][0], T), BF16)
    out_specs[2] = pl.BlockSpec((None, A_KV_HEADS * A_V_ROWS, TM), lambda b, t: (b, 0, t))
    out_shape[2] = jax.ShapeDtypeStruct((B, A_KV_HEADS * A_V_ROWS, T), BF16)
    out_specs[5] = pl.BlockSpec((None, TM // MLA_TK, vw, MLA_TK), lambda b, t: (b, t, 0, 0))
    out_shape[5] = jax.ShapeDtypeStruct((B, T // MLA_TK, vw, MLA_TK), BF16)
    return pl.pallas_call(
        functools.partial(_inproj_kernel, d_model=D),
        grid=(B, NT),
        in_specs=[
            spec_c, spec_l,
            pl.BlockSpec((None, 6, D), _mod_index),
            _const_spec((1, D)),
            *[_layer_spec(w, layer) for w in w_pieces],
            pl.BlockSpec((4, TM, LANES), lambda b, t: (0, t, 0)),
            _const_spec(bqn.shape), _const_spec(bkvn.shape), _layer_spec(wuq, layer),
            _layer_spec(wkn, layer), _layer_spec(wvb, layer), _layer_spec(wg, layer), _const_spec(bg.shape),
        ],
        out_specs=out_specs,
        out_shape=out_shape,
        compiler_params=pltpu.CompilerParams(
            dimension_semantics=("parallel", "parallel"), vmem_limit_bytes=_vmem_limit(vmem)),
    )(Xc, Xl, mod, nw, *w_pieces, rope, bqn, bkvn, wuq, wkn, wvb, wg, bg)


def _attn_a_kernel(q_ref, kp_ref, km_ref, kn_ref, kc_ref, vp_ref, vm_ref, vn_ref, vc_ref, sink_ref,
                   o_ref, *, seq):
    t = pl.program_id(1)
    kloc = jnp.concatenate([kp_ref[...], km_ref[...], kn_ref[...]], axis=0)
    vloc = jnp.concatenate([vp_ref[...], vm_ref[...], vn_ref[...]], axis=1)
    kctx, vctx = kc_ref[...], vc_ref[...]
    nloc = kloc.shape[0]
    r = lax.broadcasted_iota(jnp.int32, (nloc, TM), 0)
    c = lax.broadcasted_iota(jnp.int32, (nloc, TM), 1)
    rel = r - A_WINDOW - c
    kpos = (t - 1) * TM - A_WINDOW + r
    valid = (jnp.abs(rel) <= A_WINDOW) & (kpos >= 0) & (kpos < seq) & (t >= 1)
    group = A_HEADS // A_KV_HEADS
    per = LANES // A_HEAD_DIM
    q_zero = jnp.zeros((A_HEAD_DIM, TM), BF16)

    def scores(hd):
        q_t = q_ref[hd * A_HEAD_DIM:(hd + 1) * A_HEAD_DIM, :]
        g = hd // group
        qm = jnp.concatenate([q_t if i == g else q_zero for i in range(A_KV_HEADS)], axis=0)
        return _dot(kloc, qm), _dot(kctx, qm)

    queue = [scores(hd) for hd in range(A_AHEAD)]
    parts = []
    for hd in range(A_HEADS):
        if hd + A_AHEAD < A_HEADS:
            queue.append(scores(hd + A_AHEAD))
        cur = queue.pop(0)
        s_loc = jnp.where(valid, cur[0], _NEG_INF)
        s_ctx = cur[1]
        sink = sink_ref[hd:hd + 1, 0:1] * LOG2E
        m = jnp.maximum(jnp.maximum(jnp.max(s_loc, axis=0, keepdims=True),
                                    jnp.max(s_ctx, axis=0, keepdims=True)), sink)
        p_loc = jnp.exp2((s_loc - m).astype(BF16))
        p_ctx = jnp.exp2((s_ctx - m).astype(BF16))
        g = hd // group
        vs = slice(g * A_V_ROWS, (g + 1) * A_V_ROWS)
        o_t = _dot(vloc[vs], p_loc) + _dot(vctx[vs], p_ctx)
        denom = o_t[A_HEAD_DIM:A_HEAD_DIM + 1] + jnp.exp2(sink - m)
        parts.append(o_t[0:A_HEAD_DIM] / denom)
        if len(parts) == per:
            c = hd // per
            o_ref[:, c * LANES:(c + 1) * LANES] = jnp.concatenate(parts, axis=0).T.astype(BF16)
            parts = []


def _attn_a(qa, ka, va, sink, lc):
    B, qw, T = qa.shape
    NT = T // TM
    seq = T - lc
    kvw = ka.shape[2]
    vrows = va.shape[1]
    per_tm = TM // A_WINDOW
    last = T // A_WINDOW - 1
    prev_i = lambda t: jnp.maximum(t * per_tm - 1, 0)
    next_i = lambda t: jnp.minimum((t + 1) * per_tm, last)
    kprev = pl.BlockSpec((None, A_WINDOW, kvw), lambda b, t: (b, prev_i(t), 0))
    kmain = pl.BlockSpec((None, TM, kvw), lambda b, t: (b, t, 0))
    knext = pl.BlockSpec((None, A_WINDOW, kvw), lambda b, t: (b, next_i(t), 0))
    kctx = pl.BlockSpec((None, lc, kvw), lambda b, t: (b, 0, 0))
    vprev = pl.BlockSpec((None, vrows, A_WINDOW), lambda b, t: (b, 0, prev_i(t)))
    vmain = pl.BlockSpec((None, vrows, TM), lambda b, t: (b, 0, t))
    vnext = pl.BlockSpec((None, vrows, A_WINDOW), lambda b, t: (b, 0, next_i(t)))
    vctx = pl.BlockSpec((None, vrows, lc), lambda b, t: (b, 0, 0))
    return pl.pallas_call(
        functools.partial(_attn_a_kernel, seq=seq),
        grid=(B, NT),
        in_specs=[pl.BlockSpec((None, qw, TM), lambda b, t: (b, 0, t)),
                  kprev, kmain, knext, kctx, vprev, vmain, vnext, vctx,
                  pl.BlockSpec(sink.shape, lambda b, t: (0, 0))],
        out_specs=pl.BlockSpec((None, TM, qw), lambda b, t: (b, t, 0)),
        out_shape=jax.ShapeDtypeStruct((B, T, qw), BF16),
        compiler_params=pltpu.CompilerParams(dimension_semantics=("parallel", "parallel")),
    )(qa, ka, ka, ka, ka, va, va, va, va, sink)


def _mla_kernel(q_ref, k_ref, v_ref, o_ref, m_ref, acc_ref, sa_ref, sb_ref, *, lc, tk):
    t = pl.program_id(1)
    total = k_ref.shape[0]
    nchunks = jnp.where(t == 0, lc // tk, total // tk)
    m_ref[...] = jnp.full(m_ref.shape, _NEG_INF, F32)
    acc_ref[...] = jnp.zeros(acc_ref.shape, F32)

    def stage(dst_ref, j, hd):
        qs = slice(hd * B_HEAD_PAD, (hd + 1) * B_HEAD_PAD)
        s = _dot(k_ref[pl.ds(pl.multiple_of(j * tk, tk), tk), qs], q_ref[qs, :])
        dst_ref[hd, 0:tk, :] = s
        dst_ref[hd, tk:tk + SUBLANES, :] = jnp.broadcast_to(jnp.max(s, axis=0, keepdims=True), (SUBLANES, TM))

    for hd in range(B_HEADS):
        stage(sa_ref, 0, hd)

    def step(j, cur_ref, nxt_ref, last=False):
        for hd in range(B_HEADS):
            vs = slice(hd * B_V_ROWS, (hd + 1) * B_V_ROWS)
            if not last:
                stage(nxt_ref, j + 1, hd)
            m_prev = m_ref[hd]
            m_new = jnp.maximum(m_prev, cur_ref[hd, tk:tk + SUBLANES, :])
            alpha = jnp.exp2(m_prev - m_new)
            p = jnp.exp2((cur_ref[hd, 0:tk, :] - m_new[0:1]).astype(BF16))
            m_ref[hd] = m_new
            acc_ref[vs, :] = acc_ref[vs, :] * alpha[0:1] + _dot(v_ref[j, vs, :], p)

    bufs = (sa_ref, sb_ref)

    def body(g, carry):
        for i in range(MLA_UNROLL):
            step(MLA_UNROLL * g + i, bufs[i % 2], bufs[(i + 1) % 2])
        return carry

    ngroups = (nchunks - 1) // MLA_UNROLL
    lax.fori_loop(0, ngroups, body, 0)
    tail = (total // tk - 1) % MLA_UNROLL + 1
    assert tail == (lc // tk - 1) % MLA_UNROLL + 1
    for i in range(tail):
        step(MLA_UNROLL * ngroups + i, bufs[i % 2], bufs[(i + 1) % 2], last=(i == tail - 1))
    per = LANES // B_V_DIM
    for c in range(B_HEADS // per):
        o_t = jnp.concatenate(
            [acc_ref[hd * B_V_ROWS:hd * B_V_ROWS + B_V_DIM, :]
             / acc_ref[hd * B_V_ROWS + B_V_DIM:hd * B_V_ROWS + B_V_DIM + 1, :]
             for hd in range(c * per, (c + 1) * per)], axis=0)
        o_ref[:, c * LANES:(c + 1) * LANES] = o_t.T.astype(BF16)


def _mla(qb, kb, vb, lc):
    B, qw, T = qb.shape
    NT = T // TM
    _, nck, vw, tk = vb.shape
    ow = B_HEADS * B_V_DIM
    vmem = (T * (qw + vw) * 2 + 4 * TM * qw * 2 + (B_HEADS * SUBLANES * TM + 3 * TM * vw) * 4
            + 2 * B_HEADS * (tk + SUBLANES) * TM * 4 + (16 << 20))
    return pl.pallas_call(
        functools.partial(_mla_kernel, lc=lc, tk=tk),
        grid=(B, NT),
        in_specs=[pl.BlockSpec((None, qw, TM), lambda b, t: (b, 0, t)),
                  pl.BlockSpec((None, T, qw), lambda b, t: (b, 0, 0), pipeline_mode=pl.Buffered(1)),
                  pl.BlockSpec((None, nck, vw, tk), lambda b, t: (b, 0, 0, 0), pipeline_mode=pl.Buffered(1))],
        out_specs=pl.BlockSpec((None, TM, ow), lambda b, t: (b, t, 0)),
        out_shape=jax.ShapeDtypeStruct((B, T, ow), BF16),
        scratch_shapes=[pltpu.VMEM((B_HEADS, SUBLANES, TM), F32),
                        pltpu.VMEM((vw, TM), F32),
                        pltpu.VMEM((B_HEADS, tk + SUBLANES, TM), F32),
                        pltpu.VMEM((B_HEADS, tk + SUBLANES, TM), F32)],
        compiler_params=pltpu.CompilerParams(
            dimension_semantics=("parallel", "arbitrary"), vmem_limit_bytes=_vmem_limit(vmem)),
    )(qb, kb, vb)


def _gla_prep(q, k, b, reverse):
    C = q.shape[0]
    dk2 = q.shape[1]
    nsub = C // GLA_SUB
    b_end = b[0:1] if reverse else b[C - 1:C]
    row = lax.broadcasted_iota(jnp.int32, (C, dk2), 0)
    lane = lax.broadcasted_iota(jnp.int32, (C, dk2), 1)

    qe = (q * jnp.exp(b)).astype(BF16)
    kd_t = (k * jnp.exp(b_end - b)).T.astype(BF16)
    dec = jnp.exp(jnp.broadcast_to(b_end, (dk2, dk2)).T)
    dec2 = jnp.concatenate([dec] * (2 * C_VAL_DIM // dk2), axis=1)

    refs = [blk * GLA_SUB + (GLA_SUB - 1 if reverse else 0) for blk in range(nsub)]
    b_ref = jnp.concatenate([jnp.broadcast_to(b[n:n + 1], (GLA_SUB, dk2)) for n in refs], axis=0)
    q_sc = q * jnp.exp(b - b_ref)
    k_parts = []
    for blk, n in enumerate(refs):
        in_range = (row >= blk * GLA_SUB) if reverse else (row < (blk + 1) * GLA_SUB)
        k_parts.append(jnp.where(in_range, k * jnp.exp(b[n:n + 1] - b), 0.0))
    k_big = jnp.concatenate(k_parts, axis=1).astype(BF16)
    q_halves = []
    for hf in range(2):
        in_half = (lane >= hf * C_KEY_DIM) & (lane < (hf + 1) * C_KEY_DIM)
        q_halves.append(jnp.concatenate(
            [jnp.where(in_half & (row >= blk * GLA_SUB) & (row < (blk + 1) * GLA_SUB), q_sc, 0.0)
             for blk in range(nsub)], axis=1))
    q_big = jnp.concatenate(q_halves, axis=0).astype(BF16)
    return qe, kd_t, dec2, q_big, k_big


def _gla_kernel(qf_ref, kf_ref, vf_ref, gf_ref, qr_ref, kr_ref, vr_ref, gr_ref, of_ref, or_ref,
                sf_ref, sr_ref):
    t = pl.program_id(1)

    @pl.when(t == 0)
    def _():
        sf_ref[...] = jnp.zeros(sf_ref.shape, F32)
        sr_ref[...] = jnp.zeros(sr_ref.shape, F32)

    C = C_CHUNK
    nchunk = TM // C
    npair = C_HEADS // 2
    kw, vw = 2 * C_KEY_DIM, 2 * C_VAL_DIM
    dirs = ((qf_ref, kf_ref, vf_ref, gf_ref, of_ref, sf_ref, False),
            (qr_ref, kr_ref, vr_ref, gr_ref, or_ref, sr_ref, True))

    ii = lax.broadcasted_iota(jnp.int32, (TM, TM), 0)
    jj = lax.broadcasted_iota(jnp.int32, (TM, TM), 1)
    same_chunk = (ii // C) == (jj // C)
    b_all = []
    for (_, _, _, g_ref, _, _, reverse) in dirs:
        tri = jnp.where(same_chunk & ((jj >= ii) if reverse else (jj <= ii)), 1.0, 0.0).astype(BF16)
        g_hi, g_mid, g_lo = _split_bf16(g_ref[...], 3)
        b_all.append(_dot(tri, g_hi) + (_dot(tri, g_mid) + _dot(tri, g_lo)))

    units = []
    for d, (q_ref, k_ref, v_ref, _, o_ref, _, reverse) in enumerate(dirs):
        for p in range(npair):
            ks, vsl = slice(p * kw, (p + 1) * kw), slice(p * vw, (p + 1) * vw)
            for c in range(nchunk):
                rows = slice(c * C, (c + 1) * C)
                prep = _gla_prep(q_ref[rows, ks], k_ref[rows, ks], b_all[d][rows, ks], reverse)
                units.append((d, p, c, rows, vsl, v_ref, o_ref, reverse, prep))

    srow = lax.broadcasted_iota(jnp.int32, (kw, vw), 0)
    scol = lax.broadcasted_iota(jnp.int32, (kw, vw), 1)
    on_diag = (srow < C_KEY_DIM) == (scol < C_VAL_DIM)
    qi = lax.broadcasted_iota(jnp.int32, (2 * C, C), 0) % C
    kj = lax.broadcasted_iota(jnp.int32, (2 * C, C), 1)
    kvs, atts = [], []
    for (d, p, c, rows, vsl, v_ref, o_ref, reverse, prep) in units:
        _, kd_t, _, q_big, k_big = prep
        kvs.append(jnp.where(on_diag, _dot(kd_t, v_ref[rows, vsl]), 0.0))
        keep = (kj >= qi) if reverse else (kj <= qi)
        atts.append(jnp.where(keep, _dot_nt(q_big, k_big), 0.0).astype(BF16))

    intra = {}
    for u, (d, p, c, rows, vsl, v_ref, o_ref, reverse, prep) in enumerate(units):
        pv = _dot(atts[u], v_ref[rows, vsl])
        intra[(d, p, c)] = (u, jnp.concatenate([pv[0:C, 0:C_VAL_DIM], pv[C:2 * C, C_VAL_DIM:]], axis=1))

    states = {(d, p): dirs[d][5][p] for d in range(len(dirs)) for p in range(npair)}
    for step in range(nchunk):
        for d in range(len(dirs)):
            reverse = dirs[d][6]
            c = nchunk - 1 - step if reverse else step
            for p in range(npair):
                u, o_intra = intra[(d, p, c)]
                _, _, _, rows, vsl, _, o_ref, _, prep = units[u]
                qe, _, dec2, _, _ = prep
                o_ref[rows, vsl] = (o_intra + _dot(qe, states[(d, p)].astype(BF16))).astype(o_ref.dtype)
                states[(d, p)] = dec2 * states[(d, p)] + kvs[u]
    for d in range(len(dirs)):
        for p in range(npair):
            dirs[d][5][p] = states[(d, p)]


def _gla(cq, ck, cv, g):
    B, T, kw = cq.shape
    NT = T // TM
    vw = cv.shape[2]

    def fwd(n, col=0):
        return pl.BlockSpec((None, TM, n), lambda b, t: (b, t, col))

    def rev(n, col=0):
        return pl.BlockSpec((None, TM, n), lambda b, t: (b, jnp.where(t == 0, 0, NT - t), col))

    return pl.pallas_call(
        _gla_kernel,
        grid=(B, NT),
        in_specs=[fwd(kw), fwd(kw), fwd(vw), fwd(kw, 0), rev(kw), rev(kw), rev(vw), rev(kw, 1)],
        out_specs=[fwd(vw), rev(vw)],
        out_shape=[jax.ShapeDtypeStruct((B, T, vw), BF16)] * 2,
        scratch_shapes=[pltpu.VMEM((C_HEADS // 2, 2 * C_KEY_DIM, 2 * C_VAL_DIM), F32)] * 2,
        compiler_params=pltpu.CompilerParams(dimension_semantics=("parallel", "arbitrary")),
    )(cq, ck, cv, g, cq, ck, cv, g)


def _merge_kernel(xc_ref, xl_ref, mod_ref, ya_ref, yb_ref, of_ref, or_ref, cr_ref, gate_ref, hn_ref,
                  wa_ref, wb_ref, wc_ref, wo_ref, o_ref):
    d = xl_ref.shape[1]
    o = of_ref[...].astype(F32) + or_ref[...].astype(F32)
    r = cr_ref[...].astype(F32)
    parts = []
    for hd in range(C_HEADS):
        sl = slice(hd * C_VAL_DIM, (hd + 1) * C_VAL_DIM)
        parts.append(_rms(o[:, sl], hn_ref[:, sl]))
    yc = (jnp.concatenate(parts, axis=1) * (r * jax.nn.sigmoid(r))).astype(BF16)

    def gate(i):
        return jax.nn.sigmoid(gate_ref[:, i * d:(i + 1) * d].astype(F32))

    m = (gate(0) * _dot(ya_ref[...], wa_ref[...]) + gate(1) * _dot(yb_ref[...], wb_ref[...])
         + gate(2) * _dot(yc, wc_ref[...]))
    y = _dot(m.astype(BF16), wo_ref[...])
    o_ref[...] = _stream_tile(xc_ref, xl_ref) + mod_ref[2:3, :] * y


def _merge(Xc, Xl, mod, ya, yb, of, orv, cr, gates, hn, layer, wa, wb, wc, wo):
    spec_c, spec_l, T = _stream_specs(Xc, Xl)
    B, _, D = Xl.shape
    NT = T // TM
    row = lambda n: pl.BlockSpec((None, TM, n), lambda b, t: (b, t, 0))
    wbytes = sum(w.size // w.shape[0] for w in (wa, wb, wc, wo)) * 2
    vmem = wbytes + 2 * TM * (2 * D + 3 * D + 4 * ya.shape[2]) * 4 + (16 << 20)
    return pl.pallas_call(
        _merge_kernel,
        grid=(B, NT),
        in_specs=[spec_c, spec_l, pl.BlockSpec((None, 6, D), _mod_index),
                  row(ya.shape[2]), row(yb.shape[2]), row(of.shape[2]), row(orv.shape[2]),
                  row(cr.shape[2]), row(gates.shape[2]), _const_spec(hn.shape),
                  _layer_spec(wa, layer), _layer_spec(wb, layer), _layer_spec(wc, layer), _layer_spec(wo, layer)],
        out_specs=row(D),
        out_shape=jax.ShapeDtypeStruct((B, T, D), F32),
        compiler_params=pltpu.CompilerParams(
            dimension_semantics=("parallel", "parallel"), vmem_limit_bytes=_vmem_limit(vmem)),
    )(Xc, Xl, mod, ya, yb, of, orv, cr, gates, hn, wa, wb, wc, wo)


def _ffn_kernel(x_ref, xp_ref, xn_ref, mod_ref, nw_ref, wup_ref, cw_ref, cb_ref, wdn_ref, fw_ref, o_ref,
                u_ref, act_ref, *, nt, ncb, t0, final):
    t = pl.program_id(1) + t0
    mod = mod_ref[...]
    nw = nw_ref[...]
    halo = xp_ref.shape[0]
    ffn = wdn_ref.shape[0]
    cb = ffn // ncb

    def norm(xv):
        return _rms(xv, nw) * (1.0 + mod[4:5]) + mod[3:4]

    has_prev = (t >= 2)
    has_next = (t >= 1) & (t < nt - 1)
    hb = jnp.concatenate([norm(x_ref[...]).astype(BF16),
                          jnp.concatenate([jnp.where(has_prev, norm(xp_ref[...]), 0.0),
                                           jnp.where(has_next, norm(xn_ref[...]), 0.0)], axis=0).astype(BF16)], axis=0)
    row8 = lax.broadcasted_iota(jnp.int32, (SUBLANES, cb), 0)

    def up(j):
        u_ref[j % 2, 0] = _dot(hb, wup_ref[:, j * cb:(j + 1) * cb])
        u_ref[j % 2, 1] = _dot(hb, wup_ref[:, ffn + j * cb:ffn + (j + 1) * cb])

    def conv(slot, half, lo):
        w = cw_ref[:, lo:lo + cb]
        um = u_ref[slot, half, 0:TM, :]
        prev_row = u_ref[slot, half, TM + halo - 1:TM + halo, :]
        next_row = u_ref[slot, half, TM + halo:TM + halo + 1, :]
        rd = pltpu.roll(um, 1, 0)
        ru = pltpu.roll(um, TM - 1, 0)
        sd = jnp.concatenate([jnp.where(row8 == 0, prev_row, rd[0:SUBLANES]), rd[SUBLANES:]], axis=0)
        su = jnp.concatenate([ru[:TM - SUBLANES], jnp.where(row8 == SUBLANES - 1, next_row, ru[TM - SUBLANES:])],
                             axis=0)
        return sd * w[0:1] + um * w[1:2] + su * w[2:3] + cb_ref[:, lo:lo + cb]

    def act(j):
        gt = conv(j % 2, 0, j * cb)
        val = conv(j % 2, 1, ffn + j * cb)
        act_ref[:, j * cb:(j + 1) * cb] = (gt * jax.nn.sigmoid(gt) * val).astype(BF16)

    up(0)
    for j in range(1, ncb):
        up(j)
        act(j - 1)
    act(ncb - 1)
    y = x_ref[...] + mod[5:6] * _dot(act_ref[...], wdn_ref[...])
    o_ref[...] = _rms(y, fw_ref[...]) if final else y


def _ffn(X, mod, nw, layer, wup, cw, cb, wdn, fw, final):
    B, T, D = X.shape
    NT = T // TM
    t0 = 1 if final else 0
    halo = SUBLANES
    per = TM // halo
    last = T // halo - 1
    ffn = wdn.shape[1]
    cbw = FFN_COLS
    ncb = ffn // cbw
    vmem = 3 * D * ffn * 2 + 4 * (TM + 2 * halo) * cbw * 4 + TM * ffn * 2 + 8 * TM * D * 4 + (16 << 20)
    return pl.pallas_call(
        functools.partial(_ffn_kernel, nt=NT, ncb=ncb, t0=t0, final=final),
        grid=(B, NT - t0),
        in_specs=[pl.BlockSpec((None, TM, D), lambda b, t: (b, t + t0, 0)),
                  pl.BlockSpec((None, halo, D), lambda b, t: (b, jnp.maximum((t + t0) * per - 1, 0), 0)),
                  pl.BlockSpec((None, halo, D), lambda b, t: (b, jnp.minimum((t + t0 + 1) * per, last), 0)),
                  pl.BlockSpec((None, 6, D), lambda b, t: _mod_index(b, t + t0)),
                  _const_spec(nw.shape), _layer_spec(wup, layer), _const_spec(cw.shape), _const_spec(cb.shape),
                  _layer_spec(wdn, layer), _const_spec(fw.shape)],
        out_specs=pl.BlockSpec((None, TM, D), lambda b, t: (b, t, 0)),
        out_shape=jax.ShapeDtypeStruct((B, T - t0 * TM, D), F32),
        scratch_shapes=[pltpu.VMEM((2, 2, TM + 2 * halo, cbw), F32), pltpu.VMEM((TM, ffn), BF16)],
        compiler_params=pltpu.CompilerParams(
            dimension_semantics=("parallel", "parallel"), vmem_limit_bytes=_vmem_limit(vmem)),
    )(X, X, X, mod, nw, wup, cw, cb, wdn, fw)


def _rope_tables(seq, lc):
    rows = seq // GRID_W
    row = jnp.broadcast_to(jnp.arange(rows)[:, None], (rows, GRID_W)).reshape(-1).astype(F32)
    col = jnp.broadcast_to(jnp.arange(GRID_W)[None, :], (rows, GRID_W)).reshape(-1).astype(F32)

    def cs(rot_dim):
        n_freq = rot_dim // 4
        inv = ROPE_BASE ** (-jnp.arange(n_freq, dtype=F32) / n_freq)
        ang = jnp.concatenate([row[:, None] * inv, col[:, None] * inv], axis=-1)
        return jnp.cos(ang), jnp.sin(ang)

    ca, sa = cs(A_HEAD_DIM)
    cb, sb = cs(B_ROPE_DIM)
    reps = LANES // A_HEAD_DIM
    cos_a = jnp.tile(jnp.concatenate([ca, ca], axis=1), (1, reps))
    sin_a = jnp.tile(jnp.concatenate([-sa, sa], axis=1), (1, reps))
    one = jnp.ones((seq, B_NOPE_DIM), F32)
    pad = LANES - B_NOPE_DIM - B_ROPE_DIM
    cos_b = jnp.concatenate([one, cb, cb, jnp.ones((seq, pad), F32)], axis=1)
    sin_b = jnp.concatenate([0 * one, -sb, sb, jnp.zeros((seq, pad), F32)], axis=1)
    ident = jnp.stack([jnp.ones((lc, LANES), F32), jnp.zeros((lc, LANES), F32)] * 2)
    return jnp.concatenate([ident, jnp.stack([cos_a, sin_a, cos_b, sin_b])], axis=1)


def _prep_w_in(w_in):
    o_kv = _C_BKV
    o_kr = o_kv + B_KV_RANK
    o_cq = o_kr + B_ROPE_DIM
    o_cg = o_cq + (_C_GATE - _C_CQ)
    o_gate = o_cg + 2 * C_GATE_RANK
    zeros = lambda n: jnp.zeros(w_in.shape[:2] + (n,), w_in.dtype)
    gap = B_NOPE_DIM - 2 * C_GATE_RANK
    wkv = jnp.concatenate([w_in[:, :, o_kv:o_kr], w_in[:, :, o_cg:o_gate], zeros(gap), w_in[:, :, o_kr:o_cq],
                           zeros(LANES - B_NOPE_DIM - B_ROPE_DIM)], axis=-1)
    return tuple(p.astype(BF16) for p in (w_in[:, :, :o_kv], wkv, w_in[:, :, o_cq:o_cg], w_in[:, :, o_gate:]))


def kernel(x, c, ctx, c_ctx, w_mod, b_mod, norm_mix, norm_ffn, w_in, a_sink, b_q_norm, b_kv_norm, b_w_uq, b_w_ukv, c_w_gate, c_b_gate, c_head_norm, w_br_a, w_br_b, w_br_c, w_out, w_up, conv_w, conv_b, w_down, final_norm):
    B, S, D = x.shape
    lc = ctx.shape[1]
    L = w_mod.shape[0]
    assert lc == TM and S % TM == 0 and S % GRID_W == 0 and B + 1 <= SUBLANES

    Xc, Xl = ctx, x
    cvec = jnp.zeros((SUBLANES, D), F32).at[0].set(c_ctx).at[1:B + 1].set(c)
    mods = _modulation(cvec.T, B + 1, w_mod, b_mod).reshape(L, SUBLANES, 6, D)

    rope = _rope_tables(S, lc)
    w_in_r = _prep_w_in(w_in)
    sink = jnp.broadcast_to(a_sink[:, :, None], (L, A_HEADS, LANES)).astype(F32)
    wa, wb, wc, wo = w_br_a.astype(BF16), w_br_b.astype(BF16), w_br_c.astype(BF16), w_out.astype(BF16)
    qk = B_NOPE_DIM + B_ROPE_DIM
    wuq = jnp.pad(b_w_uq.reshape(L, B_Q_RANK, B_HEADS, qk),
                  ((0, 0), (0, 0), (0, 0), (0, B_HEAD_PAD - qk))).reshape(L, B_Q_RANK, -1).astype(BF16)
    ukv = b_w_ukv.reshape(L, B_KV_RANK, B_HEADS, B_NOPE_DIM + B_V_DIM)
    wkn = jnp.pad(ukv[..., :B_NOPE_DIM],
                  ((0, 0), (0, 0), (0, 0), (0, B_HEAD_PAD - B_NOPE_DIM))).reshape(L, B_KV_RANK, -1).astype(BF16)
    wvb = ukv[..., B_NOPE_DIM:].reshape(L, B_KV_RANK, -1).astype(BF16)
    nqk = C_HEADS * C_KEY_DIM
    wg = jnp.zeros((L, LANES, 2 * nqk), F32)
    wg = wg.at[:, 0:C_GATE_RANK, 0:nqk].set(c_w_gate[:, 0]).at[:, C_GATE_RANK:2 * C_GATE_RANK, nqk:].set(c_w_gate[:, 1])
    bg = c_b_gate.reshape(L, 1, 2 * nqk)
    wup, wdn = w_up.astype(BF16), w_down.astype(BF16)

    for l in range(L):
        mod = mods[l]
        (qa, ka, va, qb, kb, vb, cq, ck, cv, cr, g, gates) = _inproj(
            Xc, Xl, mod, norm_mix[l][None], l, w_in_r, rope, b_q_norm[l][None], b_kv_norm[l][None],
            wuq, wkn, wvb, wg, bg[l])
        ya = _attn_a(qa, ka, va, sink[l], lc)
        yb = _mla(qb, kb, vb, lc)
        of, orv = _gla(cq, ck, cv, g)
        X = _merge(Xc, Xl, mod, ya, yb, of, orv, cr, gates, c_head_norm[l][None], l, wa, wb, wc, wo)
        X = _ffn(X, mod, norm_ffn[l][None], l, wup, conv_w[l], conv_b[l][None], wdn, final_norm[None],
                 final=(l == L - 1))
        Xc = Xl = X
    return X
```
